```python
import jax, jax.numpy as jnp
from jax import lax
import numpy as np

D_MODEL = 1024
BATCH = 1
SEQ = 16384
DEPTH = 4
DEC_BATCH = 8
DEC_SEQ = 16
PAST_LEN = 2048

CHUNK = 64
N_MIXERS = 3
N_GLA = (DEPTH + 2) // 3
N_LRU = (DEPTH + 1) // 3
N_SSD = DEPTH // 3
EPS = 1e-6
CONV_W = 4

GLA_HEADS = 4
GLA_QK = D_MODEL // 2
GLA_V = D_MODEL
GLA_DK = GLA_QK // GLA_HEADS
GLA_DV = GLA_V // GLA_HEADS
GLA_RANK = 16
GLA_TAU = 16.0

LRU_WIDTH = ((4 * D_MODEL // 3 + 127) // 128) * 128
LRU_BLOCKS = 16
LRU_BW = LRU_WIDTH // LRU_BLOCKS
LRU_C = 8.0

SSD_DI = 2 * D_MODEL
SSD_HEADDIM = 64
SSD_HEADS = SSD_DI // SSD_HEADDIM
SSD_STATE = 128
SSD_GROUPS = 4
SSD_HPG = SSD_HEADS // SSD_GROUPS
SSD_CONV_DIM = SSD_DI + 2 * SSD_GROUPS * SSD_STATE

kernel_name = "hybrid_gla_rglru_ssd_streaming_step"

F32 = jnp.float32


def rmsnorm(x, w):
    xf = x.astype(F32)
    y = xf * lax.rsqrt(jnp.mean(xf * xf, axis=-1, keepdims=True) + EPS)
    return (y * w.astype(F32)).astype(x.dtype)


def causal_conv(x, buf, w, b):
    L = x.shape[1]
    xp = jnp.concatenate([buf.astype(x.dtype), x], axis=1)
    y = b
    for k in range(CONV_W):
        y = y + w[k] * xp[:, k:k + L]
    return y, xp[:, xp.shape[1] - (CONV_W - 1):]


def scan_chunk_states(s0, decay, ds):
    def step(s, inp):
        dec, d = inp
        return dec * s + d, s
    s_fin, s_prev = lax.scan(step, s0, (jnp.moveaxis(decay, 1, 0), jnp.moveaxis(ds, 1, 0)))
    return s_fin, jnp.moveaxis(s_prev, 0, 1)


def gla_mix(h, s0, w_in, w_gate_up, b_gate, norm_w, w_out):
    B, L, _ = h.shape
    proj = h @ w_in
    q, k, v, g, glr = jnp.split(
        proj, [GLA_QK, 2 * GLA_QK, 2 * GLA_QK + GLA_V, 2 * GLA_QK + 2 * GLA_V], axis=-1)
    log_a = jax.nn.log_sigmoid((glr @ w_gate_up + b_gate).astype(F32)) / GLA_TAU
    c = min(CHUNK, L)
    nc = L // c
    q = q.astype(F32).reshape(B, nc, c, GLA_HEADS, GLA_DK) * (GLA_DK ** -0.5)
    k = k.astype(F32).reshape(B, nc, c, GLA_HEADS, GLA_DK)
    v = v.astype(F32).reshape(B, nc, c, GLA_HEADS, GLA_DV)
    bcum = jnp.cumsum(log_a.reshape(B, nc, c, GLA_HEADS, GLA_DK), axis=2)
    b_last = bcum[:, :, -1:]
    q_d = q * jnp.exp(bcum)
    k_d = k * jnp.exp(-bcum)
    k_end = k * jnp.exp(b_last - bcum)
    mask = jnp.tril(jnp.ones((c, c), dtype=bool))
    att = jnp.where(mask, jnp.einsum('bnihd,bnjhd->bnhij', q_d, k_d), 0.0)
    o = jnp.einsum('bnhij,bnjhv->bnihv', att, v)
    ds = jnp.einsum('bnjhd,bnjhv->bnhdv', k_end, v)
    decay = jnp.exp(b_last[:, :, 0])[..., None]
    s_fin, s_prev = scan_chunk_states(s0.astype(F32), decay, ds)
    o = o + jnp.einsum('bnihd,bnhdv->bnihv', q_d, s_prev)
    o = o * lax.rsqrt(jnp.mean(o * o, axis=-1, keepdims=True) + EPS) * norm_w.astype(F32)
    o = o.reshape(B, L, GLA_V) * jax.nn.silu(g.astype(F32))
    return o.astype(h.dtype) @ w_out, s_fin.astype(s0.dtype)


def lru_mix(h, conv_buf, h0, w_in, conv_w, conv_b, w_a, b_a, w_x, b_x, lam, w_out):
    B, L, _ = h.shape
    xb, gate = jnp.split(h @ w_in, 2, axis=-1)
    xc, new_buf = causal_conv(xb, conv_buf, conv_w, conv_b)
    xf = xc.astype(F32)
    xblk = xf.reshape(B, L, LRU_BLOCKS, LRU_BW)
    r = jax.nn.sigmoid(jnp.einsum('blkc,kcd->blkd', xblk, w_a.astype(F32)).reshape(B, L, LRU_WIDTH) + b_a)
    i = jax.nn.sigmoid(jnp.einsum('blkc,kcd->blkd', xblk, w_x.astype(F32)).reshape(B, L, LRU_WIDTH) + b_x)
    log_a = -LRU_C * r * jax.nn.softplus(-lam.astype(F32))
    a = jnp.exp(log_a)
    u = jnp.sqrt(-jnp.expm1(2.0 * log_a)) * (i * xf)

    def combine(c1, c2):
        a1, b1 = c1
        a2, b2 = c2
        return a1 * a2, a2 * b1 + b2

    a_cum, hs = lax.associative_scan(combine, (a, u), axis=1)
    hs = hs + a_cum * h0.astype(F32)[:, None]
    y = (hs * jax.nn.silu(gate.astype(F32))).astype(h.dtype) @ w_out
    return y, new_buf, hs[:, -1].astype(h0.dtype)


def ssd_mix(h, conv_buf, s0, w_in, conv_w, conv_b, dt_bias, a_log, d_skip, norm_w, w_out):
    B, L, _ = h.shape
    z, xbc, dt = jnp.split(h @ w_in, [SSD_DI, SSD_DI + SSD_CONV_DIM], axis=-1)
    xbc, new_buf = causal_conv(xbc, conv_buf, conv_w, conv_b)
    xbc = jax.nn.silu(xbc.astype(F32))
    x, bm, cm = jnp.split(xbc, [SSD_DI, SSD_DI + SSD_GROUPS * SSD_STATE], axis=-1)
    dt = jax.nn.softplus(dt.astype(F32) + dt_bias.astype(F32))
    A = -jnp.exp(a_log.astype(F32)).reshape(SSD_GROUPS, SSD_HPG)
    c = min(CHUNK, L)
    nc = L // c
    x = x.reshape(B, nc, c, SSD_GROUPS, SSD_HPG, SSD_HEADDIM)
    bm = bm.reshape(B, nc, c, SSD_GROUPS, SSD_STATE)
    cm = cm.reshape(B, nc, c, SSD_GROUPS, SSD_STATE)
    dt = dt.reshape(B, nc, c, SSD_GROUPS, SSD_HPG)
    cum = jnp.cumsum(dt * A, axis=2)
    mask = jnp.tril(jnp.ones((c, c), dtype=bool))[:, :, None, None]
    seg = cum[:, :, :, None] - cum[:, :, None]
    decay = jnp.exp(jnp.where(mask, seg, -jnp.inf))
    cb = jnp.einsum('bnigs,bnjgs->bnijg', cm, bm)
    wgt = cb[..., None] * decay * dt[:, :, None]
    y = jnp.einsum('bnijgr,bnjgrp->bnigrp', wgt, x)
    wB = jnp.exp(cum[:, :, -1:] - cum) * dt
    ds = jnp.einsum('bnjgs,bnjgrp->bngrps', bm, x * wB[..., None])
    chunk_decay = jnp.exp(cum[:, :, -1])[..., None, None]
    s_init = s0.astype(F32).reshape(B, SSD_GROUPS, SSD_HPG, SSD_HEADDIM, SSD_STATE)
    s_fin, s_prev = scan_chunk_states(s_init, chunk_decay, ds)
    y = y + jnp.einsum('bnigs,bngrps->bnigrp', cm, s_prev) * jnp.exp(cum)[..., None]
    y = y + d_skip.astype(F32).reshape(SSD_GROUPS, SSD_HPG)[:, :, None] * x
    yz = y.reshape(B, L, SSD_DI) * jax.nn.silu(z.astype(F32))
    yz = rmsnorm(yz, norm_w)
    y_out = yz.astype(h.dtype) @ w_out
    return y_out, new_buf, s_fin.reshape(s0.shape).astype(s0.dtype)


def trunk(x, st_gla, st_lru_conv, st_lru_h, st_ssd_conv, st_ssd_h,
          norm_w, final_norm_w, gla_p, lru_p, ssd_p):
    new_gla, new_lru_conv, new_lru_h, new_ssd_conv, new_ssd_h = [], [], [], [], []
    for i in range(DEPTH):
        j = i // N_MIXERS
        h = rmsnorm(x, norm_w[i])
        if i % N_MIXERS == 0:
            y, s = gla_mix(h, st_gla[j], *[p[j] for p in gla_p])
            new_gla.append(s)
        elif i % N_MIXERS == 1:
            y, cb, s = lru_mix(h, st_lru_conv[j], st_lru_h[j], *[p[j] for p in lru_p])
            new_lru_conv.append(cb)
            new_lru_h.append(s)
        else:
            y, cb, s = ssd_mix(h, st_ssd_conv[j], st_ssd_h[j], *[p[j] for p in ssd_p])
            new_ssd_conv.append(cb)
            new_ssd_h.append(s)
        x = x + y.astype(x.dtype)
    out = rmsnorm(x, final_norm_w)
    return (out, jnp.stack(new_gla), jnp.stack(new_lru_conv), jnp.stack(new_lru_h),
            jnp.stack(new_ssd_conv), jnp.stack(new_ssd_h))


def setup_inputs(seed: int = 0) -> dict:
    key = jax.random.key(seed)
    ks = list(jax.random.split(key, 40))

    def nk():
        return ks.pop()

    def nrm(shape, scale):
        return scale * jax.random.normal(nk(), shape, F32)

    u_lru = jax.random.uniform(nk(), (N_LRU, LRU_WIDTH), F32, minval=0.9, maxval=0.999)
    a_base = u_lru ** (1.0 / LRU_C)
    dt0 = jnp.exp(jax.random.uniform(nk(), (N_SSD, SSD_HEADS), F32,
                                     minval=float(np.log(1e-3)), maxval=float(np.log(1e-1))))
    return {
        'x_prompt': nrm((BATCH, SEQ, D_MODEL), 1.0),
        'x_sample': nrm((DEC_BATCH, DEC_SEQ, D_MODEL), 1.0),
        'state_gla': nrm((N_GLA, DEC_BATCH, GLA_HEADS, GLA_DK, GLA_DV), 0.5),
        'state_lru_conv': nrm((N_LRU, DEC_BATCH, CONV_W - 1, LRU_WIDTH), 1.0),
        'state_lru_h': nrm((N_LRU, DEC_BATCH, LRU_WIDTH), 0.5),
        'state_ssd_conv': nrm((N_SSD, DEC_BATCH, CONV_W - 1, SSD_CONV_DIM), 1.0),
        'state_ssd_h': nrm((N_SSD, DEC_BATCH, SSD_HEADS, SSD_HEADDIM, SSD_STATE), 0.1),
        'norm_w': 1.0 + nrm((DEPTH, D_MODEL), 0.01),
        'final_norm_w': 1.0 + nrm((D_MODEL,), 0.01),
        'gla_w_in': nrm((N_GLA, D_MODEL, 2 * GLA_QK + 2 * GLA_V + GLA_RANK), D_MODEL ** -0.5),
        'gla_w_gate_up': nrm((N_GLA, GLA_RANK, GLA_QK), GLA_RANK ** -0.5),
        'gla_b_gate': nrm((N_GLA, GLA_QK), 0.1),
        'gla_norm_w': 1.0 + nrm((N_GLA, GLA_DV), 0.01),
        'gla_w_out': nrm((N_GLA, GLA_V, D_MODEL), GLA_V ** -0.5),
        'lru_w_in': nrm((N_LRU, D_MODEL, 2 * LRU_WIDTH), D_MODEL ** -0.5),
        'lru_conv_w': nrm((N_LRU, CONV_W, LRU_WIDTH), 0.5),
        'lru_conv_b': nrm((N_LRU, LRU_WIDTH), 0.02),
        'lru_w_a': nrm((N_LRU, LRU_BLOCKS, LRU_BW, LRU_BW), LRU_BW ** -0.5),
        'lru_b_a': nrm((N_LRU, LRU_WIDTH), 0.02),
        'lru_w_x': nrm((N_LRU, LRU_BLOCKS, LRU_BW, LRU_BW), LRU_BW ** -0.5),
        'lru_b_x': nrm((N_LRU, LRU_WIDTH), 0.02),
        'lru_lambda': jnp.log(a_base) - jnp.log1p(-a_base),
        'lru_w_out': nrm((N_LRU, LRU_WIDTH, D_MODEL), LRU_WIDTH ** -0.5),
        'ssd_w_in': nrm((N_SSD, D_MODEL, SSD_DI + SSD_CONV_DIM + SSD_HEADS), D_MODEL ** -0.5),
        'ssd_conv_w': nrm((N_SSD, CONV_W, SSD_CONV_DIM), 0.5),
        'ssd_conv_b': nrm((N_SSD, SSD_CONV_DIM), 0.02),
        'ssd_dt_bias': dt0 + jnp.log(-jnp.expm1(-dt0)),
        'ssd_a_log': jnp.log(jax.random.uniform(nk(), (N_SSD, SSD_HEADS), F32, minval=1.0, maxval=16.0)),
        'ssd_d': 1.0 + nrm((N_SSD, SSD_HEADS), 0.1),
        'ssd_norm_w': 1.0 + nrm((N_SSD, SSD_DI), 0.01),
        'ssd_w_out': nrm((N_SSD, SSD_DI, D_MODEL), SSD_DI ** -0.5),
    }


def reference(x_prompt, x_sample, state_gla, state_lru_conv, state_lru_h, state_ssd_conv, state_ssd_h,
              norm_w, final_norm_w,
              gla_w_in, gla_w_gate_up, gla_b_gate, gla_norm_w, gla_w_out,
              lru_w_in, lru_conv_w, lru_conv_b, lru_w_a, lru_b_a, lru_w_x, lru_b_x, lru_lambda, lru_w_out,
              ssd_w_in, ssd_conv_w, ssd_conv_b, ssd_dt_bias, ssd_a_log, ssd_d, ssd_norm_w, ssd_w_out):
    gla_p = (gla_w_in, gla_w_gate_up, gla_b_gate, gla_norm_w, gla_w_out)
    lru_p = (lru_w_in, lru_conv_w, lru_conv_b, lru_w_a, lru_b_a, lru_w_x, lru_b_x, lru_lambda, lru_w_out)
    ssd_p = (ssd_w_in, ssd_conv_w, ssd_conv_b, ssd_dt_bias, ssd_a_log, ssd_d, ssd_norm_w, ssd_w_out)
    dt = x_prompt.dtype
    zero_gla = jnp.zeros((N_GLA, BATCH, GLA_HEADS, GLA_DK, GLA_DV), dt)
    zero_lru_conv = jnp.zeros((N_LRU, BATCH, CONV_W - 1, LRU_WIDTH), dt)
    zero_lru_h = jnp.zeros((N_LRU, BATCH, LRU_WIDTH), dt)
    zero_ssd_conv = jnp.zeros((N_SSD, BATCH, CONV_W - 1, SSD_CONV_DIM), dt)
    zero_ssd_h = jnp.zeros((N_SSD, BATCH, SSD_HEADS, SSD_HEADDIM, SSD_STATE), dt)
    y_prompt, gla_p_new, lru_conv_p_new, lru_h_p_new, ssd_conv_p_new, ssd_h_p_new = trunk(
        x_prompt, zero_gla, zero_lru_conv, zero_lru_h, zero_ssd_conv, zero_ssd_h,
        norm_w, final_norm_w, gla_p, lru_p, ssd_p)
    y_sample, gla_s_new, lru_conv_s_new, lru_h_s_new, ssd_conv_s_new, ssd_h_s_new = trunk(
        x_sample, state_gla, state_lru_conv, state_lru_h, state_ssd_conv, state_ssd_h,
        norm_w, final_norm_w, gla_p, lru_p, ssd_p)
    return (y_prompt, y_sample,
            gla_p_new, lru_conv_p_new, lru_h_p_new, ssd_conv_p_new, ssd_h_p_new,
            gla_s_new, lru_conv_s_new, lru_h_s_new, ssd_conv_s_new, ssd_h_s_new)
```

```python
import functools

import numpy as np
import jax
import jax.numpy as jnp
from jax import lax
from jax.experimental import pallas as pl
from jax.experimental.pallas import tpu as pltpu

F32 = jnp.float32
BF16 = jnp.bfloat16

D_MODEL = 1024
CHUNK = 64
EPS = 1e-6
CONV_W = 4

GLA_HEADS = 4
GLA_QK = D_MODEL // 2
GLA_V = D_MODEL
GLA_DK = GLA_QK // GLA_HEADS
GLA_DV = GLA_V // GLA_HEADS
GLA_RANK = 16
GLA_TAU = 16.0

LRU_WIDTH = 1408
LRU_BLOCKS = 16
LRU_BW = LRU_WIDTH // LRU_BLOCKS
LRU_C = 8.0

SSD_DI = 2 * D_MODEL
SSD_HEADDIM = 64
SSD_HEADS = SSD_DI // SSD_HEADDIM
SSD_STATE = 128
SSD_GROUPS = 4
SSD_HPG = SSD_HEADS // SSD_GROUPS
SSD_CONV_DIM = SSD_DI + 2 * SSD_GROUPS * SSD_STATE
SSD_GW = SSD_HPG * SSD_HEADDIM

LANES = 128
SUBLANES = 8
PROMPT_TILE = 256
VMEM_LIMIT_BYTES = 56 * 1024 * 1024
LRU_TILES = LRU_WIDTH // LANES
LRU_BAND = 3


def _bdot(a, b):
    return jnp.dot(a.astype(BF16), b.astype(BF16), preferred_element_type=F32)


def _bdot_nt(a, b):
    return lax.dot_general(a.astype(BF16), b.astype(BF16), (((1,), (1,)), ((), ())),
                           preferred_element_type=F32)


def _bdot_tn(a, b):
    return lax.dot_general(a.astype(BF16), b.astype(BF16), (((0,), (0,)), ((), ())),
                           preferred_element_type=F32)


def _split3(x):
    hi = x.astype(BF16)
    r1 = x - hi.astype(F32)
    mid = r1.astype(BF16)
    lo = (r1 - mid.astype(F32)).astype(BF16)
    return hi, mid, lo


def _sel_dot(sel, x):
    hi, mid, lo = _split3(x)
    d = lambda p: jnp.dot(sel, p, preferred_element_type=F32)
    return d(hi) + d(mid) + d(lo)


def _dot_sel(x, sel):
    hi, mid, lo = _split3(x)
    d = lambda p: jnp.dot(p, sel, preferred_element_type=F32)
    return d(hi) + d(mid) + d(lo)


def _sel_dot_nt(sel, x):
    hi, mid, lo = _split3(x)
    d = lambda p: lax.dot_general(sel, p, (((1,), (1,)), ((), ())), preferred_element_type=F32)
    return d(hi) + d(mid) + d(lo)


def _rms(x, w):
    return x * lax.rsqrt(jnp.mean(x * x, axis=-1, keepdims=True) + EPS) * w


def _softplus(x):
    return jnp.maximum(x, 0.0) + jnp.log1p(jnp.exp(-jnp.abs(x)))


def _silu(x):
    return x * jax.nn.sigmoid(x)


def _conv_taps(xpad_ref, n_rows, cw_ref, cb_ref):
    y = cb_ref[...]
    for k in range(CONV_W):
        y = y + cw_ref[k:k + 1, :] * xpad_ref[SUBLANES - (CONV_W - 1) + k:SUBLANES - (CONV_W - 1) + k + n_rows, :]
    return y


def _causal_conv(xb, xpad_s, xc_s, cw_ref, cb_ref, cbuf_in, cbuf_out, *, T, c, chained, act=None):
    pad0 = SUBLANES - (CONV_W - 1)
    if chained:
        @pl.when(pl.program_id(0) == 0)
        def _():
            xpad_s[0:SUBLANES, :] = jnp.zeros((SUBLANES, xpad_s.shape[1]), F32)
        xpad_s[SUBLANES:SUBLANES + T, :] = xb
        y = _conv_taps(xpad_s, T, cw_ref, cb_ref)
        xc_s[...] = y if act is None else act(y)
        tail = xpad_s[T + pad0:T + SUBLANES, :]
        cbuf_out[...] = tail
        xpad_s[pad0:SUBLANES, :] = tail
    else:
        for b in range(T // c):
            xpad_s[pad0:SUBLANES, :] = cbuf_in[b]
            xpad_s[SUBLANES:SUBLANES + c, :] = xb[b * c:(b + 1) * c, :]
            y = _conv_taps(xpad_s, c, cw_ref, cb_ref)
            xc_s[b * c:(b + 1) * c, :] = y if act is None else act(y)
            cbuf_out[b] = xpad_s[c + pad0:c + SUBLANES, :]


def _gla_body(*refs, T, c, chained, final_norm):
    refs = list(refs)
    (x_ref, nw_ref, wm_ref, wglr_ref, wgu_ref, bg_ref, gnw_ref, wout_ref, tri_ref, eye_ref) = refs[:10]
    refs = refs[10:]
    s0_ref = None if chained else refs.pop(0)
    fnw_ref = refs.pop(0) if final_norm else None
    y_ref, sout_ref, proj_s, bcum_s, o_s = refs
    nc = T // c

    x = x_ref[...]
    hb = _rms(x, nw_ref[...]).astype(BF16)
    proj_s[...] = jnp.dot(hb, wm_ref[...], preferred_element_type=F32)
    glr = jnp.dot(hb, wglr_ref[...], preferred_element_type=F32)
    zg = _bdot(glr, wgu_ref[...]) + bg_ref[...]
    log_a = -_softplus(-zg) / GLA_TAU
    bcum_s[...] = _sel_dot(tri_ref[...], log_a)

    if chained:
        @pl.when(pl.program_id(0) == 0)
        def _():
            sout_ref[...] = jnp.zeros(sout_ref.shape, F32)

    tril = lax.broadcasted_iota(jnp.int32, (c, c), 0) >= lax.broadcasted_iota(jnp.int32, (c, c), 1)
    k_off, v_off, g_off = GLA_QK, 2 * GLA_QK, 2 * GLA_QK + GLA_V

    def chunk(n, carry):
        rows = pl.ds(pl.multiple_of(n * c, c), c)
        for hd in range(GLA_HEADS):
            kc = hd * GLA_DK
            vc = hd * GLA_DV
            bc = bcum_s[rows, kc:kc + GLA_DK]
            bl = bc[c - 1:c, :]
            q = proj_s[rows, kc:kc + GLA_DK]
            k = proj_s[rows, k_off + kc:k_off + kc + GLA_DK]
            v = proj_s[rows, v_off + vc:v_off + vc + GLA_DV]
            g = proj_s[rows, g_off + vc:g_off + vc + GLA_DV]
            qd = q * (GLA_DK ** -0.5) * jnp.exp(bc)
            kd = k * jnp.exp(-bc)
            ke = k * jnp.exp(bl - bc)
            att = jnp.where(tril, _bdot_nt(qd, kd), 0.0)
            s_prev = sout_ref[hd] if chained else s0_ref[n, hd]
            o = _bdot(att, v) + _bdot(qd, s_prev)
            ds = _bdot_tn(ke, v)
            bl_col = _sel_dot_nt(eye_ref[...], jnp.broadcast_to(bl, (SUBLANES, GLA_DK)))[:, 0:1]
            s_new = jnp.exp(bl_col) * s_prev + ds
            if chained:
                sout_ref[hd] = s_new
            else:
                sout_ref[n, hd] = s_new
            o = o * lax.rsqrt(jnp.mean(o * o, axis=-1, keepdims=True) + EPS) * gnw_ref[...]
            o_s[rows, vc:vc + GLA_DV] = o * _silu(g)
        return carry

    lax.fori_loop(0, nc, chunk, 0)
    out = x + jnp.dot(o_s[...].astype(BF16), wout_ref[...], preferred_element_type=F32)
    if final_norm:
        out = _rms(out, fnw_ref[...])
    y_ref[...] = out


def _lru_body(*refs, T, c, chained):
    refs = list(refs)
    (x_ref, nw_ref, win_ref, cw_ref, cb_ref, wband_ref, ba_ref, bx_ref, lam_ref, wout_ref) = refs[:10]
    refs = refs[10:]
    if chained:
        cbuf_in = h0_ref = None
    else:
        cbuf_in = refs.pop(0)
        h0_ref = refs.pop(0)
    y_ref, cbuf_out, h_out, xpad_s, xc_s, a_s, u_s, hs_s, hin_s = refs
    W = LRU_WIDTH
    seg = T // SUBLANES

    x = x_ref[...]
    hb = _rms(x, nw_ref[...]).astype(BF16)
    proj = jnp.dot(hb, win_ref[...], preferred_element_type=F32)
    gate = proj[:, W:]
    _causal_conv(proj[:, :W], xpad_s, xc_s, cw_ref, cb_ref, cbuf_in, cbuf_out, T=T, c=c, chained=chained)

    sp = _softplus(-lam_ref[...])
    for j in range(LRU_TILES):
        st = min(max(j - 1, 0), LRU_TILES - LRU_BAND)
        lo = j * LANES
        xin = xc_s[:, st * LANES:(st + LRU_BAND) * LANES]
        pre = _bdot(xin, wband_ref[j])
        xc = xc_s[:, lo:lo + LANES]
        r = jax.nn.sigmoid(pre[:, :LANES] + ba_ref[:, lo:lo + LANES])
        i = jax.nn.sigmoid(pre[:, LANES:] + bx_ref[:, lo:lo + LANES])
        log_a = -LRU_C * r * sp[:, lo:lo + LANES]
        th = jnp.tanh(log_a)
        one_minus_a2 = -2.0 * th / (1.0 - th)
        a_s[j] = jnp.exp(log_a)
        u_s[j] = jnp.sqrt(one_minus_a2) * (i * xc)

    def seg_rows(t):
        return pl.ds(t, SUBLANES, stride=seg)

    tiles = range(LRU_TILES)
    if chained:
        @pl.when(pl.program_id(0) == 0)
        def _():
            h_out[...] = jnp.zeros(h_out.shape, F32)

        def local_scan(t, carry):
            hs, ps = carry
            a = [a_s[j, seg_rows(t), :] for j in tiles]
            return (tuple(a[j] * hs[j] + u_s[j, seg_rows(t), :] for j in tiles),
                    tuple(ps[j] * a[j] for j in tiles))

        h_end, p_end = lax.fori_loop(
            0, seg, local_scan,
            (tuple(jnp.zeros((SUBLANES, LANES), F32) for _ in tiles),
             tuple(jnp.ones((SUBLANES, LANES), F32) for _ in tiles)))
        h_end = jnp.concatenate(h_end, axis=1)
        p_end = jnp.concatenate(p_end, axis=1)
        hc = h_out[...]
        for s in range(SUBLANES):
            hin_s[s:s + 1, :] = hc
            hc = h_end[s:s + 1, :] + p_end[s:s + 1, :] * hc
        h_out[...] = hc
        h_init = hin_s[...]
    else:
        h_init = h0_ref[...]

    def full_scan(t, hs):
        new = []
        for j in tiles:
            h = a_s[j, seg_rows(t), :] * hs[j] + u_s[j, seg_rows(t), :]
            hs_s[j, seg_rows(t), :] = h
            new.append(h)
        return tuple(new)

    h_fin = lax.fori_loop(0, seg, full_scan, tuple(h_init[:, j * LANES:(j + 1) * LANES] for j in tiles))
    if not chained:
        h_out[...] = jnp.concatenate(h_fin, axis=1)

    for j in tiles:
        lo = j * LANES
        xc_s[:, lo:lo + LANES] = hs_s[j] * _silu(gate[:, lo:lo + LANES])
    y_ref[...] = x + jnp.dot(xc_s[...].astype(BF16), wout_ref[...], preferred_element_type=F32)


def _ssd_body(*refs, T, c, chained):
    refs = list(refs)
    (x_ref, nw_ref, wz_ref, wxbc_ref, wdt_ref, cw_ref, cb_ref, dtb_ref, alog_ref, de_ref, snw_ref,
     wout_ref, tri_ref, eye_ref, exp_ref) = refs[:15]
    refs = refs[15:]
    if chained:
        cbuf_in = s0_ref = None
    else:
        cbuf_in = refs.pop(0)
        s0_ref = refs.pop(0)
    if chained:
        (y_ref, cbuf_out, sout_ref, xpad_s, xbc_s, z_s, cum_s, dt_s, ecum_s, xw_s, yacc_s, cdec_s, st_s) = refs
    else:
        (y_ref, cbuf_out, sout_ref, xpad_s, xbc_s, z_s, cum_s, dt_s, ecum_s, xw_s, yacc_s, cdec_s) = refs
        st_s = None
    nc = T // c
    b_off = SSD_DI
    c_off = SSD_DI + SSD_GROUPS * SSD_STATE

    x = x_ref[...]
    hb = _rms(x, nw_ref[...]).astype(BF16)
    z_s[...] = jnp.dot(hb, wz_ref[...], preferred_element_type=F32)
    xbc_pre = jnp.dot(hb, wxbc_ref[...], preferred_element_type=F32)
    _causal_conv(xbc_pre, xpad_s, xbc_s, cw_ref, cb_ref, cbuf_in, cbuf_out, T=T, c=c, chained=chained, act=_silu)

    dt = _softplus(jnp.dot(hb, wdt_ref[...], preferred_element_type=F32) + dtb_ref[...])
    a_neg = -jnp.exp(alog_ref[...])
    cum = _sel_dot(tri_ref[...], dt * a_neg)
    cum_s[...] = cum
    dt_s[...] = dt
    lasts = [cum[n * c + c - 1:n * c + c, :] for n in range(nc)]
    cl = jnp.concatenate([jnp.broadcast_to(l, (c, LANES)) for l in lasts], axis=0)
    ecum_s[...] = _dot_sel(jnp.exp(cum), exp_ref[...])
    xw_s[...] = xbc_s[:, 0:SSD_DI] * _dot_sel(jnp.exp(cl - cum) * dt, exp_ref[...])
    cl_rows = jnp.concatenate(lasts + [jnp.zeros((SUBLANES - nc, LANES), F32)] * (nc < SUBLANES), axis=0)
    cdec_s[...] = _dot_sel(jnp.exp(cl_rows), exp_ref[...])

    if chained:
        @pl.when(pl.program_id(0) == 0)
        def _():
            st_s[...] = jnp.zeros(st_s.shape, F32)

    tril = lax.broadcasted_iota(jnp.int32, (c, c), 0) >= lax.broadcasted_iota(jnp.int32, (c, c), 1)
    low_half = lax.broadcasted_iota(jnp.int32, (c, LANES), 1) < SSD_HEADDIM

    def chunk(n, carry):
        rows = pl.ds(pl.multiple_of(n * c, c), c)
        cum_c = cum_s[rows, :]
        cum_t = _sel_dot_nt(eye_ref[...], cum_c)
        dt_t = _sel_dot_nt(eye_ref[...], dt_s[rows, :])
        for g in range(SSD_GROUPS):
            gl = g * SSD_GW
            bm = xbc_s[rows, b_off + g * SSD_STATE:b_off + (g + 1) * SSD_STATE]
            cm = xbc_s[rows, c_off + g * SSD_STATE:c_off + (g + 1) * SSD_STATE]
            cb = _bdot_nt(cm, bm)
            if chained:
                st_prev = st_s[g]
            else:
                st_prev = s0_ref[n, g].T
            y_inter = _bdot(cm, st_prev) * ecum_s[rows, gl:gl + SSD_GW]
            for pr in range(SSD_HPG // 2):
                h0 = g * SSD_HPG + 2 * pr
                xl = h0 * SSD_HEADDIM
                xp = xbc_s[rows, xl:xl + LANES]
                acc = y_inter[:, pr * LANES:(pr + 1) * LANES]
                for e in range(2):
                    h = h0 + e
                    seg = cum_c[:, h:h + 1] - cum_t[h:h + 1, :]
                    dec = jnp.exp(jnp.where(tril, seg, -jnp.inf))
                    wgt = cb * dec * dt_t[h:h + 1, :]
                    xm = jnp.where(low_half if e == 0 else jnp.logical_not(low_half), xp, 0.0)
                    acc = acc + _bdot(wgt, xm)
                yacc_s[rows, xl:xl + LANES] = acc
            ds = _bdot_tn(bm, xw_s[rows, gl:gl + SSD_GW])
            st_new = st_prev * cdec_s[pl.ds(n, 1), gl:gl + SSD_GW] + ds
            if chained:
                st_s[g] = st_new
            else:
                sout_ref[n, g] = st_new.T
        return carry

    lax.fori_loop(0, nc, chunk, 0)

    if chained:
        @pl.when(pl.program_id(0) == pl.num_programs(0) - 1)
        def _():
            for g in range(SSD_GROUPS):
                sout_ref[g] = st_s[g].T

    y = yacc_s[...] + de_ref[...] * xbc_s[:, 0:SSD_DI]
    yz = _rms(y * _silu(z_s[...]), snw_ref[...]).astype(BF16)
    y_ref[...] = x + jnp.dot(yz, wout_ref[...], preferred_element_type=F32)


def _const_spec(shape):
    nd = len(shape)
    return pl.BlockSpec(shape, lambda i, _nd=nd: (0,) * _nd, pipeline_mode=pl.Buffered(1))


def _row_spec(T, width):
    return pl.BlockSpec((T, width), lambda i: (i, 0))


def _params():
    return pltpu.CompilerParams(dimension_semantics=("arbitrary",), vmem_limit_bytes=VMEM_LIMIT_BYTES)


def _block_tri(T, c):
    r = np.arange(T)
    return jnp.asarray((r[:, None] // c == r[None, :] // c) & (r[:, None] >= r[None, :]), BF16)


def _eye():
    return jnp.asarray(np.eye(LANES), BF16)


def _gla_layer(x, s0, p, *, T, c, chained, final_norm_w=None):
    rows = x.shape[0]
    consts = [p["norm_w"], p["w_main"], p["w_glr"], p["w_gu"], p["b_gate"], p["gnorm_w"], p["w_out"],
              _block_tri(T, c), _eye()]
    args = [x] + consts
    specs = [_row_spec(T, D_MODEL)] + [_const_spec(a.shape) for a in consts]
    if not chained:
        args.append(s0)
        specs.append(_const_spec(s0.shape))
    if final_norm_w is not None:
        args.append(final_norm_w)
        specs.append(_const_spec(final_norm_w.shape))
    st_shape = (GLA_HEADS, GLA_DK, GLA_DV) if chained else s0.shape
    body = functools.partial(_gla_body, T=T, c=c, chained=chained, final_norm=final_norm_w is not None)
    return pl.pallas_call(
        body,
        grid=(rows // T,),
        in_specs=specs,
        out_specs=[_row_spec(T, D_MODEL), pl.BlockSpec(st_shape, lambda i, _n=len(st_shape): (0,) * _n)],
        out_shape=[jax.ShapeDtypeStruct((rows, D_MODEL), F32), jax.ShapeDtypeStruct(st_shape, F32)],
        scratch_shapes=[pltpu.VMEM((T, 2 * GLA_QK + 2 * GLA_V), F32), pltpu.VMEM((T, GLA_QK), F32),
                        pltpu.VMEM((T, GLA_V), F32)],
        compiler_params=_params(),
        name="gla_chained" if chained else "gla_batched",
    )(*args)


def _lru_layer(x, cbuf, h0, p, *, T, c, chained):
    rows = x.shape[0]
    W = LRU_WIDTH
    consts = [p["norm_w"], p["w_in"], p["conv_w"], p["conv_b"], p["w_band"], p["b_a"], p["b_x"], p["lam"], p["w_out"]]
    args = [x] + consts
    specs = [_row_spec(T, D_MODEL)] + [_const_spec(a.shape) for a in consts]
    if chained:
        cb_shape, h_shape = (CONV_W - 1, W), (1, W)
        pad_rows = SUBLANES + T
    else:
        args += [cbuf, h0]
        specs += [_const_spec(cbuf.shape), _const_spec(h0.shape)]
        cb_shape, h_shape = cbuf.shape, h0.shape
        pad_rows = SUBLANES + c
    body = functools.partial(_lru_body, T=T, c=c, chained=chained)
    zero_map = lambda n: (lambda i: (0,) * n)
    return pl.pallas_call(
        body,
        grid=(rows // T,),
        in_specs=specs,
        out_specs=[_row_spec(T, D_MODEL), pl.BlockSpec(cb_shape, zero_map(len(cb_shape))),
                   pl.BlockSpec(h_shape, zero_map(len(h_shape)))],
        out_shape=[jax.ShapeDtypeStruct((rows, D_MODEL), F32), jax.ShapeDtypeStruct(cb_shape, F32),
                   jax.ShapeDtypeStruct(h_shape, F32)],
        scratch_shapes=[pltpu.VMEM((pad_rows, W), F32), pltpu.VMEM((T, W), F32)]
                       + [pltpu.VMEM((LRU_TILES, T, LANES), F32)] * 3 + [pltpu.VMEM((SUBLANES, W), F32)],
        compiler_params=_params(),
        name="lru_chained" if chained else "lru_batched",
    )(*args)


def _ssd_layer(x, cbuf, s0, p, *, T, c, chained):
    rows = x.shape[0]
    consts = [p["norm_w"], p["w_z"], p["w_xbc"], p["w_dt"], p["conv_w"], p["conv_b"], p["dt_bias"], p["a_log"],
              p["d_exp"], p["snorm_w"], p["w_out"], _block_tri(T, c), _eye(), p["expand"]]
    args = [x] + consts
    specs = [_row_spec(T, D_MODEL)] + [_const_spec(a.shape) for a in consts]
    if chained:
        cb_shape = (CONV_W - 1, SSD_CONV_DIM)
        st_shape = (SSD_GROUPS, SSD_GW, SSD_STATE)
        pad_rows = SUBLANES + T
    else:
        args += [cbuf, s0]
        specs += [_const_spec(cbuf.shape), _const_spec(s0.shape)]
        cb_shape, st_shape = cbuf.shape, s0.shape
        pad_rows = SUBLANES + c
    scratch = [pltpu.VMEM((pad_rows, SSD_CONV_DIM), F32), pltpu.VMEM((T, SSD_CONV_DIM), F32),
               pltpu.VMEM((T, SSD_DI), F32), pltpu.VMEM((T, LANES), F32), pltpu.VMEM((T, LANES), F32),
               pltpu.VMEM((T, SSD_DI), F32), pltpu.VMEM((T, SSD_DI), F32), pltpu.VMEM((T, SSD_DI), F32),
               pltpu.VMEM((SUBLANES, SSD_DI), F32)]
    if chained:
        scratch.append(pltpu.VMEM((SSD_GROUPS, SSD_STATE, SSD_GW), F32))
    body = functools.partial(_ssd_body, T=T, c=c, chained=chained)
    zero_map = lambda n: (lambda i: (0,) * n)
    return pl.pallas_call(
        body,
        grid=(rows // T,),
        in_specs=specs,
        out_specs=[_row_spec(T, D_MODEL), pl.BlockSpec(cb_shape, zero_map(len(cb_shape))),
                   pl.BlockSpec(st_shape, zero_map(len(st_shape)))],
        out_shape=[jax.ShapeDtypeStruct((rows, D_MODEL), F32), jax.ShapeDtypeStruct(cb_shape, F32),
                   jax.ShapeDtypeStruct(st_shape, F32)],
        scratch_shapes=scratch,
        compiler_params=_params(),
        name="ssd_chained" if chained else "ssd_batched",
    )(*args)


def _row(v):
    return v.reshape(1, -1).astype(F32)


def _pad_lanes(a, width=LANES):
    return jnp.pad(a, [(0, 0)] * (a.ndim - 1) + [(0, width - a.shape[-1])])


def _band_slabs(w):
    eye = jnp.eye(LRU_BLOCKS, dtype=w.dtype)
    dense = (w[:, :, None, :] * eye[:, None, :, None]).reshape(LRU_WIDTH, LRU_WIDTH)
    slabs = []
    for j in range(LRU_TILES):
        st = min(max(j - 1, 0), LRU_TILES - LRU_BAND)
        slabs.append(dense[st * LANES:(st + LRU_BAND) * LANES, j * LANES:(j + 1) * LANES])
    return jnp.stack(slabs)


def _gla_params(norm_w, w_in, w_gate_up, b_gate, gnorm_w, w_out):
    n_main = 2 * GLA_QK + 2 * GLA_V
    return {
        "norm_w": _row(norm_w),
        "w_main": w_in[:, :n_main].astype(BF16),
        "w_glr": _pad_lanes(w_in[:, n_main:]).astype(BF16),
        "w_gu": jnp.pad(w_gate_up, ((0, LANES - GLA_RANK), (0, 0))).astype(BF16),
        "b_gate": _row(b_gate),
        "gnorm_w": _row(gnorm_w),
        "w_out": w_out.astype(BF16),
    }


def _lru_params(norm_w, w_in, conv_w, conv_b, w_a, b_a, w_x, b_x, lam, w_out):
    return {
        "norm_w": _row(norm_w),
        "w_in": w_in.astype(BF16),
        "conv_w": conv_w.astype(F32),
        "conv_b": _row(conv_b),
        "w_band": jnp.concatenate([_band_slabs(w_a), _band_slabs(w_x)], axis=-1).astype(BF16),
        "b_a": _row(b_a),
        "b_x": _row(b_x),
        "lam": _row(lam),
        "w_out": w_out.astype(BF16),
    }


def _ssd_params(norm_w, w_in, conv_w, conv_b, dt_bias, a_log, d_skip, snorm_w, w_out):
    heads = np.arange(SSD_DI) // SSD_HEADDIM
    expand = jnp.asarray(np.arange(LANES)[:, None] == heads[None, :], BF16)
    return {
        "norm_w": _row(norm_w),
        "w_z": w_in[:, :SSD_DI].astype(BF16),
        "w_xbc": w_in[:, SSD_DI:SSD_DI + SSD_CONV_DIM].astype(BF16),
        "w_dt": _pad_lanes(w_in[:, SSD_DI + SSD_CONV_DIM:]).astype(BF16),
        "conv_w": conv_w.astype(F32),
        "conv_b": _row(conv_b),
        "dt_bias": _pad_lanes(_row(dt_bias)),
        "a_log": _pad_lanes(_row(a_log)),
        "d_exp": _row(jnp.repeat(d_skip, SSD_HEADDIM)),
        "snorm_w": _row(snorm_w),
        "w_out": w_out.astype(BF16),
        "expand": expand,
    }


def _trunk(x, st_gla, st_lru_conv, st_lru_h, st_ssd_conv, st_ssd_h, layers, final_norm_w, *, T, c, chained):
    new = {"gla": [], "lru_conv": [], "lru_h": [], "ssd_conv": [], "ssd_h": []}
    n_layers = len(layers)
    for i, (kind, j, p) in enumerate(layers):
        if kind == "gla":
            fnw = final_norm_w if i == n_layers - 1 else None
            x, s = _gla_layer(x, None if chained else st_gla[j], p, T=T, c=c, chained=chained, final_norm_w=fnw)
            new["gla"].append(s)
        elif kind == "lru":
            x, cb, h = _lru_layer(x, None if chained else st_lru_conv[j], None if chained else st_lru_h[j], p,
                                  T=T, c=c, chained=chained)
            new["lru_conv"].append(cb)
            new["lru_h"].append(h)
        else:
            s0 = None if chained else st_ssd_h[j].reshape(-1, SSD_GROUPS, SSD_GW, SSD_STATE)
            x, cb, s = _ssd_layer(x, None if chained else st_ssd_conv[j], s0, p, T=T, c=c, chained=chained)
            new["ssd_conv"].append(cb)
            new["ssd_h"].append(s)
    return x, new


def kernel(x_prompt, x_sample, state_gla, state_lru_conv, state_lru_h, state_ssd_conv, state_ssd_h, norm_w, final_norm_w, gla_w_in, gla_w_gate_up, gla_b_gate, gla_norm_w, gla_w_out, lru_w_in, lru_conv_w, lru_conv_b, lru_w_a, lru_b_a, lru_w_x, lru_b_x, lru_lambda, lru_w_out, ssd_w_in, ssd_conv_w, ssd_conv_b, ssd_dt_bias, ssd_a_log, ssd_d, ssd_norm_w, ssd_w_out):
    depth = norm_w.shape[0]
    layers = []
    for i in range(depth):
        j = i // 3
        if i % 3 == 0:
            layers.append(("gla", j, _gla_params(norm_w[i], gla_w_in[j], gla_w_gate_up[j], gla_b_gate[j],
                                                 gla_norm_w[j], gla_w_out[j])))
        elif i % 3 == 1:
            layers.append(("lru", j, _lru_params(norm_w[i], lru_w_in[j], lru_conv_w[j], lru_conv_b[j], lru_w_a[j],
                                                 lru_b_a[j], lru_w_x[j], lru_b_x[j], lru_lambda[j], lru_w_out[j])))
        else:
            layers.append(("ssd", j, _ssd_params(norm_w[i], ssd_w_in[j], ssd_conv_w[j], ssd_conv_b[j],
                                                 ssd_dt_bias[j], ssd_a_log[j], ssd_d[j], ssd_norm_w[j],
                                                 ssd_w_out[j])))
    assert layers[-1][0] == "gla", "the final RMSNorm is fused into a GLA layer"
    fnw = _row(final_norm_w)

    bp, lp, _ = x_prompt.shape
    bs, ls, _ = x_sample.shape
    assert bp == 1 and lp % PROMPT_TILE == 0 and bs * ls == LANES and bs == SUBLANES

    yp, new_p = _trunk(x_prompt.reshape(lp, D_MODEL), None, None, None, None, None, layers, fnw,
                       T=PROMPT_TILE, c=min(CHUNK, lp), chained=True)
    ys, new_s = _trunk(x_sample.reshape(bs * ls, D_MODEL), state_gla, state_lru_conv, state_lru_h,
                       state_ssd_conv, state_ssd_h, layers, fnw, T=bs * ls, c=min(CHUNK, ls), chained=False)

    def stack(xs, shape):
        return jnp.stack(xs).reshape(shape)

    n_gla, n_lru, n_ssd = len(new_p["gla"]), len(new_p["lru_h"]), len(new_p["ssd_h"])
    return (
        yp.reshape(x_prompt.shape), ys.reshape(x_sample.shape),
        stack(new_p["gla"], (n_gla, 1, GLA_HEADS, GLA_DK, GLA_DV)),
        stack(new_p["lru_conv"], (n_lru, 1, CONV_W - 1, LRU_WIDTH)),
        stack(new_p["lru_h"], (n_lru, 1, LRU_WIDTH)),
        stack(new_p["ssd_conv"], (n_ssd, 1, CONV_W - 1, SSD_CONV_DIM)),
        stack(new_p["ssd_h"], (n_ssd, 1, SSD_HEADS, SSD_HEADDIM, SSD_STATE)),
        stack(new_s["gla"], (n_gla, bs, GLA_HEADS, GLA_DK, GLA_DV)),
        stack(new_s["lru_conv"], (n_lru, bs, CONV_W - 1, LRU_WIDTH)),
        stack(new_s["lru_h"], (n_lru, bs, LRU_WIDTH)),
        stack(new_s["ssd_conv"], (n_ssd, bs, CONV_W - 1, SSD_CONV_DIM)),
        stack(new_s["ssd_h"], (n_ssd, bs, SSD_HEADS, SSD_HEADDIM, SSD_STATE)),
    )
```

```python
import functools

import numpy as np
import jax
import jax.numpy as jnp
from jax import lax
from jax.experimental import pallas as pl
from jax.experimental.pallas import tpu as pltpu

F32 = jnp.float32
BF16 = jnp.bfloat16

D_MODEL = 1024
CHUNK = 64
EPS = 1e-6
CONV_W = 4

GLA_HEADS = 4
GLA_QK = D_MODEL // 2
GLA_V = D_MODEL
GLA_DK = GLA_QK // GLA_HEADS
GLA_DV = GLA_V // GLA_HEADS
GLA_RANK = 16
GLA_TAU = 16.0

LRU_WIDTH = 1408
LRU_BLOCKS = 16
LRU_BW = LRU_WIDTH // LRU_BLOCKS
LRU_C = 8.0

SSD_DI = 2 * D_MODEL
SSD_HEADDIM = 64
SSD_HEADS = SSD_DI // SSD_HEADDIM
SSD_STATE = 128
SSD_GROUPS = 4
SSD_HPG = SSD_HEADS // SSD_GROUPS
SSD_CONV_DIM = SSD_DI + 2 * SSD_GROUPS * SSD_STATE
SSD_GW = SSD_HPG * SSD_HEADDIM

LANES = 128
SUBLANES = 8
PROMPT_TILE = 256
VMEM_LIMIT_BYTES = 56 * 1024 * 1024
LRU_TILES = LRU_WIDTH // LANES
LRU_BAND = 3


def _bdot(a, b):
    return jnp.dot(a.astype(BF16), b.astype(BF16), preferred_element_type=F32)


def _bdot_nt(a, b):
    return lax.dot_general(a.astype(BF16), b.astype(BF16), (((1,), (1,)), ((), ())),
                           preferred_element_type=F32)


def _bdot_tn(a, b):
    return lax.dot_general(a.astype(BF16), b.astype(BF16), (((0,), (0,)), ((), ())),
                           preferred_element_type=F32)


def _split3(x):
    hi = x.astype(BF16)
    r1 = x - hi.astype(F32)
    mid = r1.astype(BF16)
    lo = (r1 - mid.astype(F32)).astype(BF16)
    return hi, mid, lo


def _sel_dot(sel, x):
    hi, mid, lo = _split3(x)
    d = lambda p: jnp.dot(sel, p, preferred_element_type=F32)
    return d(hi) + d(mid) + d(lo)


def _dot_sel(x, sel):
    hi, mid, lo = _split3(x)
    d = lambda p: jnp.dot(p, sel, preferred_element_type=F32)
    return d(hi) + d(mid) + d(lo)


def _sel_dot_nt(sel, x):
    hi, mid, lo = _split3(x)
    d = lambda p: lax.dot_general(sel, p, (((1,), (1,)), ((), ())), preferred_element_type=F32)
    return d(hi) + d(mid) + d(lo)


def _rms(x, w):
    return x * lax.rsqrt(jnp.mean(x * x, axis=-1, keepdims=True) + EPS) * w


def _softplus(x):
    return jnp.maximum(x, 0.0) + jnp.log1p(jnp.exp(-jnp.abs(x)))


def _silu(x):
    return x * jax.nn.sigmoid(x)


def _scan_pitch(seg):
    return seg if (seg // SUBLANES) % 2 == 1 else seg + SUBLANES


def _causal_conv(xb, xpad_s, xc_s, cw_ref, cb_ref, cbuf_in, cbuf_out, *, T, c, chained, act=None):
    nt = xb.shape[1] // LANES
    hdr = SUBLANES
    pad0 = hdr - (CONV_W - 1)
    blk = 2 * SUBLANES
    lanes = lambda j: slice(j * LANES, (j + 1) * LANES)
    if chained:
        seqs = [(0, 0, T)]

        @pl.when(pl.program_id(0) == 0)
        def _():
            xpad_s[:, 0:hdr, :] = jnp.zeros((nt, hdr, LANES), F32)
    else:
        seqs = [(b * (hdr + c), b * c, c) for b in range(T // c)]
        for b, (base, _, _) in enumerate(seqs):
            prev = cbuf_in[b]
            for j in range(nt):
                xpad_s[j, base + pad0:base + hdr, :] = prev[:, lanes(j)]
    for base, r0, n in seqs:
        for j in range(nt):
            xpad_s[j, base + hdr:base + hdr + n, :] = xb[r0:r0 + n, lanes(j)]
    for j in range(nt):
        w = [jnp.broadcast_to(cw_ref[k:k + 1, lanes(j)], (SUBLANES, LANES)) for k in range(CONV_W)]
        bias = jnp.broadcast_to(cb_ref[:, lanes(j)], (SUBLANES, LANES))
        for base, r0, n in seqs:
            for i in range(n // blk):
                p0 = base + pad0 + i * blk
                win = [xpad_s[j, pl.ds(p0 + m, SUBLANES, stride=2), :] for m in range(CONV_W + 1)]
                for par in range(2):
                    y = bias
                    for k in range(CONV_W):
                        y = y + w[k] * win[par + k]
                    xc_s[j, pl.ds(r0 + i * blk + par, SUBLANES, stride=2), :] = y if act is None else act(y)
    for b, (base, r0, n) in enumerate(seqs):
        tail = xb[r0 + n - (CONV_W - 1):r0 + n, :]
        if chained:
            cbuf_out[...] = tail
            for j in range(nt):
                xpad_s[j, pad0:hdr, :] = tail[:, lanes(j)]
        else:
            cbuf_out[b] = tail


def _gla_body(*refs, T, c, chained, final_norm):
    refs = list(refs)
    (x_ref, nw_ref, wm_ref, wglr_ref, wgu_ref, bg_ref, gnw_ref, wout_ref, tri_ref, eye_ref) = refs[:10]
    refs = refs[10:]
    s0_ref = None if chained else refs.pop(0)
    fnw_ref = refs.pop(0) if final_norm else None
    y_ref, sout_ref, proj_s, bcum_s, o_s = refs
    nc = T // c

    x = x_ref[...]
    hb = _rms(x, nw_ref[...]).astype(BF16)
    proj_s[...] = jnp.dot(hb, wm_ref[...], preferred_element_type=F32)
    glr = jnp.dot(hb, wglr_ref[...], preferred_element_type=F32)
    zg = _bdot(glr, wgu_ref[...]) + bg_ref[...]
    log_a = -_softplus(-zg) / GLA_TAU
    bcum_s[...] = _sel_dot(tri_ref[...], log_a)

    if chained:
        @pl.when(pl.program_id(0) == 0)
        def _():
            sout_ref[...] = jnp.zeros(sout_ref.shape, F32)

    tril = lax.broadcasted_iota(jnp.int32, (c, c), 0) >= lax.broadcasted_iota(jnp.int32, (c, c), 1)
    k_off, v_off, g_off = GLA_QK, 2 * GLA_QK, 2 * GLA_QK + GLA_V

    def chunk(n, carry):
        rows = pl.ds(pl.multiple_of(n * c, c), c)
        for hd in range(GLA_HEADS):
            kc = hd * GLA_DK
            vc = hd * GLA_DV
            bc = bcum_s[rows, kc:kc + GLA_DK]
            bl = bc[c - 1:c, :]
            q = proj_s[rows, kc:kc + GLA_DK]
            k = proj_s[rows, k_off + kc:k_off + kc + GLA_DK]
            v = proj_s[rows, v_off + vc:v_off + vc + GLA_DV]
            g = proj_s[rows, g_off + vc:g_off + vc + GLA_DV]
            qd = q * (GLA_DK ** -0.5) * jnp.exp(bc)
            kd = k * jnp.exp(-bc)
            ke = k * jnp.exp(bl - bc)
            att = jnp.where(tril, _bdot_nt(qd, kd), 0.0)
            s_prev = sout_ref[hd] if chained else s0_ref[n, hd]
            o = _bdot(att, v) + _bdot(qd, s_prev)
            ds = _bdot_tn(ke, v)
            bl_col = _sel_dot_nt(eye_ref[...], jnp.broadcast_to(bl, (SUBLANES, GLA_DK)))[:, 0:1]
            s_new = jnp.exp(bl_col) * s_prev + ds
            if chained:
                sout_ref[hd] = s_new
            else:
                sout_ref[n, hd] = s_new
            o = o * lax.rsqrt(jnp.mean(o * o, axis=-1, keepdims=True) + EPS) * gnw_ref[...]
            o_s[rows, vc:vc + GLA_DV] = o * _silu(g)
        return carry

    lax.fori_loop(0, nc, chunk, 0)
    out = x + jnp.dot(o_s[...].astype(BF16), wout_ref[...], preferred_element_type=F32)
    if final_norm:
        out = _rms(out, fnw_ref[...])
    y_ref[...] = out


def _lru_body(*refs, T, c, chained):
    refs = list(refs)
    (x_ref, nw_ref, win_ref, cw_ref, cb_ref, wband_ref, ba_ref, bx_ref, lam_ref, wout_ref) = refs[:10]
    refs = refs[10:]
    if chained:
        cbuf_in = h0_ref = None
    else:
        cbuf_in = refs.pop(0)
        h0_ref = refs.pop(0)
    y_ref, cbuf_out, h_out, xpad_s, xc_s, a_s, u_s, hs_s, hin_s = refs
    W = LRU_WIDTH
    seg = T // SUBLANES

    x = x_ref[...]
    hb = _rms(x, nw_ref[...]).astype(BF16)
    proj = jnp.dot(hb, win_ref[...], preferred_element_type=F32)
    gate = proj[:, W:]
    _causal_conv(proj[:, :W], xpad_s, xc_s, cw_ref, cb_ref, cbuf_in, cbuf_out, T=T, c=c, chained=chained)

    pitch = _scan_pitch(seg)
    sp = _softplus(-lam_ref[...])
    for j in range(LRU_TILES):
        st = min(max(j - 1, 0), LRU_TILES - LRU_BAND)
        lo = j * LANES
        xin = jnp.concatenate([xc_s[st + m] for m in range(LRU_BAND)], axis=1)
        pre = _bdot(xin, wband_ref[j])
        r = jax.nn.sigmoid(pre[:, :LANES] + ba_ref[:, lo:lo + LANES])
        i = jax.nn.sigmoid(pre[:, LANES:] + bx_ref[:, lo:lo + LANES])
        log_a = -LRU_C * r * sp[:, lo:lo + LANES]
        th = jnp.tanh(log_a)
        one_minus_a2 = -2.0 * th / (1.0 - th)
        a = jnp.exp(log_a)
        u = jnp.sqrt(one_minus_a2) * (i * xc_s[j])
        for s in range(SUBLANES):
            a_s[j, s * pitch:s * pitch + seg, :] = a[s * seg:(s + 1) * seg, :]
            u_s[j, s * pitch:s * pitch + seg, :] = u[s * seg:(s + 1) * seg, :]

    def seg_rows(t):
        return pl.ds(t, SUBLANES, stride=pitch)

    tiles = range(LRU_TILES)
    if chained:
        @pl.when(pl.program_id(0) == 0)
        def _():
            h_out[...] = jnp.zeros(h_out.shape, F32)

        def local_scan(t, carry):
            hs, ps = carry
            a = [a_s[j, seg_rows(t), :] for j in tiles]
            return (tuple(a[j] * hs[j] + u_s[j, seg_rows(t), :] for j in tiles),
                    tuple(ps[j] * a[j] for j in tiles))

        h_end, p_end = lax.fori_loop(
            0, seg, local_scan,
            (tuple(jnp.zeros((SUBLANES, LANES), F32) for _ in tiles),
             tuple(jnp.ones((SUBLANES, LANES), F32) for _ in tiles)))
        h_end = jnp.concatenate(h_end, axis=1)
        p_end = jnp.concatenate(p_end, axis=1)
        hc = h_out[...]
        for s in range(SUBLANES):
            hin_s[s:s + 1, :] = hc
            hc = h_end[s:s + 1, :] + p_end[s:s + 1, :] * hc
        h_out[...] = hc
        h_init = hin_s[...]
    else:
        h_init = h0_ref[...]

    def full_scan(t, hs):
        new = []
        for j in tiles:
            h = a_s[j, seg_rows(t), :] * hs[j] + u_s[j, seg_rows(t), :]
            hs_s[j, seg_rows(t), :] = h
            new.append(h)
        return tuple(new)

    h_fin = lax.fori_loop(0, seg, full_scan, tuple(h_init[:, j * LANES:(j + 1) * LANES] for j in tiles))
    if not chained:
        h_out[...] = jnp.concatenate(h_fin, axis=1)

    gated = []
    for j in tiles:
        hs = jnp.concatenate([hs_s[j, s * pitch:s * pitch + seg, :] for s in range(SUBLANES)], axis=0)
        gated.append((hs * _silu(gate[:, j * LANES:(j + 1) * LANES])).astype(BF16))
    y_ref[...] = x + jnp.dot(jnp.concatenate(gated, axis=1), wout_ref[...], preferred_element_type=F32)


def _ssd_body(*refs, T, c, chained):
    refs = list(refs)
    (x_ref, nw_ref, wz_ref, wxbc_ref, wdt_ref, cw_ref, cb_ref, dtb_ref, alog_ref, de_ref, snw_ref,
     wout_ref, tri_ref, eye_ref, exp_ref) = refs[:15]
    refs = refs[15:]
    if chained:
        cbuf_in = s0_ref = None
    else:
        cbuf_in = refs.pop(0)
        s0_ref = refs.pop(0)
    if chained:
        (y_ref, cbuf_out, sout_ref, xpad_s, xbc_s, z_s, cum_s, dt_s, ecum_s, xw_s, yacc_s, cdec_s, st_s) = refs
    else:
        (y_ref, cbuf_out, sout_ref, xpad_s, xbc_s, z_s, cum_s, dt_s, ecum_s, xw_s, yacc_s, cdec_s) = refs
        st_s = None
    nc = T // c
    b_off = SSD_DI
    c_off = SSD_DI + SSD_GROUPS * SSD_STATE

    x = x_ref[...]
    hb = _rms(x, nw_ref[...]).astype(BF16)
    z_s[...] = jnp.dot(hb, wz_ref[...], preferred_element_type=F32)
    xbc_pre = jnp.dot(hb, wxbc_ref[...], preferred_element_type=F32)
    _causal_conv(xbc_pre, xpad_s, xbc_s, cw_ref, cb_ref, cbuf_in, cbuf_out, T=T, c=c, chained=chained, act=_silu)

    dt = _softplus(jnp.dot(hb, wdt_ref[...], preferred_element_type=F32) + dtb_ref[...])
    a_neg = -jnp.exp(alog_ref[...])
    cum = _sel_dot(tri_ref[...], dt * a_neg)
    cum_s[...] = cum
    dt_s[...] = dt
    lasts = [cum[n * c + c - 1:n * c + c, :] for n in range(nc)]
    cl = jnp.concatenate([jnp.broadcast_to(l, (c, LANES)) for l in lasts], axis=0)
    ecum_s[...] = _dot_sel(jnp.exp(cum), exp_ref[...])
    wb_e = _dot_sel(jnp.exp(cl - cum) * dt, exp_ref[...])
    for j in range(SSD_DI // LANES):
        xw_s[:, j * LANES:(j + 1) * LANES] = xbc_s[j] * wb_e[:, j * LANES:(j + 1) * LANES]
    cl_rows = jnp.concatenate(lasts + [jnp.zeros((SUBLANES - nc, LANES), F32)] * (nc < SUBLANES), axis=0)
    cdec_s[...] = _dot_sel(jnp.exp(cl_rows), exp_ref[...])

    if chained:
        @pl.when(pl.program_id(0) == 0)
        def _():
            st_s[...] = jnp.zeros(st_s.shape, F32)

    tril = lax.broadcasted_iota(jnp.int32, (c, c), 0) >= lax.broadcasted_iota(jnp.int32, (c, c), 1)
    low_half = lax.broadcasted_iota(jnp.int32, (c, LANES), 1) < SSD_HEADDIM

    def chunk(n, carry):
        rows = pl.ds(pl.multiple_of(n * c, c), c)
        cum_c = cum_s[rows, :]
        cum_t = _sel_dot_nt(eye_ref[...], cum_c)
        dt_t = _sel_dot_nt(eye_ref[...], dt_s[rows, :])
        for g in range(SSD_GROUPS):
            gl = g * SSD_GW
            bm = xbc_s[b_off // LANES + g, rows, :]
            cm = xbc_s[c_off // LANES + g, rows, :]
            cb = _bdot_nt(cm, bm)
            if chained:
                st_prev = st_s[g]
            else:
                st_prev = s0_ref[n, g].T
            y_inter = _bdot(cm, st_prev) * ecum_s[rows, gl:gl + SSD_GW]
            for pr in range(SSD_HPG // 2):
                h0 = g * SSD_HPG + 2 * pr
                xl = h0 * SSD_HEADDIM
                xp = xbc_s[h0 // 2, rows, :]
                acc = y_inter[:, pr * LANES:(pr + 1) * LANES]
                for e in range(2):
                    h = h0 + e
                    seg = cum_c[:, h:h + 1] - cum_t[h:h + 1, :]
                    dec = jnp.exp(jnp.where(tril, seg, -jnp.inf))
                    wgt = cb * dec * dt_t[h:h + 1, :]
                    xm = jnp.where(low_half if e == 0 else jnp.logical_not(low_half), xp, 0.0)
                    acc = acc + _bdot(wgt, xm)
                yacc_s[rows, xl:xl + LANES] = acc
            ds = _bdot_tn(bm, xw_s[rows, gl:gl + SSD_GW])
            st_new = st_prev * cdec_s[pl.ds(n, 1), gl:gl + SSD_GW] + ds
            if chained:
                st_s[g] = st_new
            else:
                sout_ref[n, g] = st_new.T
        return carry

    lax.fori_loop(0, nc, chunk, 0)

    if chained:
        @pl.when(pl.program_id(0) == pl.num_programs(0) - 1)
        def _():
            for g in range(SSD_GROUPS):
                sout_ref[g] = st_s[g].T

    xs = jnp.concatenate([xbc_s[j] for j in range(SSD_DI // LANES)], axis=1)
    y = yacc_s[...] + de_ref[...] * xs
    yz = _rms(y * _silu(z_s[...]), snw_ref[...]).astype(BF16)
    y_ref[...] = x + jnp.dot(yz, wout_ref[...], preferred_element_type=F32)


def _const_spec(shape):
    nd = len(shape)
    return pl.BlockSpec(shape, lambda i, _nd=nd: (0,) * _nd, pipeline_mode=pl.Buffered(1))


def _row_spec(T, width):
    return pl.BlockSpec((T, width), lambda i: (i, 0))


def _params():
    return pltpu.CompilerParams(dimension_semantics=("arbitrary",), vmem_limit_bytes=VMEM_LIMIT_BYTES)


def _block_tri(T, c):
    r = np.arange(T)
    return jnp.asarray((r[:, None] // c == r[None, :] // c) & (r[:, None] >= r[None, :]), BF16)


def _eye():
    return jnp.asarray(np.eye(LANES), BF16)


def _gla_layer(x, s0, p, *, T, c, chained, final_norm_w=None):
    rows = x.shape[0]
    consts = [p["norm_w"], p["w_main"], p["w_glr"], p["w_gu"], p["b_gate"], p["gnorm_w"], p["w_out"],
              _block_tri(T, c), _eye()]
    args = [x] + consts
    specs = [_row_spec(T, D_MODEL)] + [_const_spec(a.shape) for a in consts]
    if not chained:
        args.append(s0)
        specs.append(_const_spec(s0.shape))
    if final_norm_w is not None:
        args.append(final_norm_w)
        specs.append(_const_spec(final_norm_w.shape))
    st_shape = (GLA_HEADS, GLA_DK, GLA_DV) if chained else s0.shape
    body = functools.partial(_gla_body, T=T, c=c, chained=chained, final_norm=final_norm_w is not None)
    return pl.pallas_call(
        body,
        grid=(rows // T,),
        in_specs=specs,
        out_specs=[_row_spec(T, D_MODEL), pl.BlockSpec(st_shape, lambda i, _n=len(st_shape): (0,) * _n)],
        out_shape=[jax.ShapeDtypeStruct((rows, D_MODEL), F32), jax.ShapeDtypeStruct(st_shape, F32)],
        scratch_shapes=[pltpu.VMEM((T, 2 * GLA_QK + 2 * GLA_V), F32), pltpu.VMEM((T, GLA_QK), F32),
                        pltpu.VMEM((T, GLA_V), F32)],
        compiler_params=_params(),
        name="gla_chained" if chained else "gla_batched",
    )(*args)


def _lru_layer(x, cbuf, h0, p, *, T, c, chained):
    rows = x.shape[0]
    W = LRU_WIDTH
    consts = [p["norm_w"], p["w_in"], p["conv_w"], p["conv_b"], p["w_band"], p["b_a"], p["b_x"], p["lam"], p["w_out"]]
    args = [x] + consts
    specs = [_row_spec(T, D_MODEL)] + [_const_spec(a.shape) for a in consts]
    if chained:
        cb_shape, h_shape = (CONV_W - 1, W), (1, W)
        pad_rows = SUBLANES + T
    else:
        args += [cbuf, h0]
        specs += [_const_spec(cbuf.shape), _const_spec(h0.shape)]
        cb_shape, h_shape = cbuf.shape, h0.shape
        pad_rows = (T // c) * (SUBLANES + c)
    scan_rows = SUBLANES * _scan_pitch(T // SUBLANES)
    body = functools.partial(_lru_body, T=T, c=c, chained=chained)
    zero_map = lambda n: (lambda i: (0,) * n)
    return pl.pallas_call(
        body,
        grid=(rows // T,),
        in_specs=specs,
        out_specs=[_row_spec(T, D_MODEL), pl.BlockSpec(cb_shape, zero_map(len(cb_shape))),
                   pl.BlockSpec(h_shape, zero_map(len(h_shape)))],
        out_shape=[jax.ShapeDtypeStruct((rows, D_MODEL), F32), jax.ShapeDtypeStruct(cb_shape, F32),
                   jax.ShapeDtypeStruct(h_shape, F32)],
        scratch_shapes=[pltpu.VMEM((LRU_TILES, pad_rows, LANES), F32), pltpu.VMEM((LRU_TILES, T, LANES), F32)]
                       + [pltpu.VMEM((LRU_TILES, scan_rows, LANES), F32)] * 3 + [pltpu.VMEM((SUBLANES, W), F32)],
        compiler_params=_params(),
        name="lru_chained" if chained else "lru_batched",
    )(*args)


def _ssd_layer(x, cbuf, s0, p, *, T, c, chained):
    rows = x.shape[0]
    consts = [p["norm_w"], p["w_z"], p["w_xbc"], p["w_dt"], p["conv_w"], p["conv_b"], p["dt_bias"], p["a_log"],
              p["d_exp"], p["snorm_w"], p["w_out"], _block_tri(T, c), _eye(), p["expand"]]
    args = [x] + consts
    specs = [_row_spec(T, D_MODEL)] + [_const_spec(a.shape) for a in consts]
    if chained:
        cb_shape = (CONV_W - 1, SSD_CONV_DIM)
        st_shape = (SSD_GROUPS, SSD_GW, SSD_STATE)
        pad_rows = SUBLANES + T
    else:
        args += [cbuf, s0]
        specs += [_const_spec(cbuf.shape), _const_spec(s0.shape)]
        cb_shape, st_shape = cbuf.shape, s0.shape
        pad_rows = (T // c) * (SUBLANES + c)
    conv_tiles = SSD_CONV_DIM // LANES
    scratch = [pltpu.VMEM((conv_tiles, pad_rows, LANES), F32), pltpu.VMEM((conv_tiles, T, LANES), F32),
               pltpu.VMEM((T, SSD_DI), F32), pltpu.VMEM((T, LANES), F32), pltpu.VMEM((T, LANES), F32),
               pltpu.VMEM((T, SSD_DI), F32), pltpu.VMEM((T, SSD_DI), F32), pltpu.VMEM((T, SSD_DI), F32),
               pltpu.VMEM((SUBLANES, SSD_DI), F32)]
    if chained:
        scratch.append(pltpu.VMEM((SSD_GROUPS, SSD_STATE, SSD_GW), F32))
    body = functools.partial(_ssd_body, T=T, c=c, chained=chained)
    zero_map = lambda n: (lambda i: (0,) * n)
    return pl.pallas_call(
        body,
        grid=(rows // T,),
        in_specs=specs,
        out_specs=[_row_spec(T, D_MODEL), pl.BlockSpec(cb_shape, zero_map(len(cb_shape))),
                   pl.BlockSpec(st_shape, zero_map(len(st_shape)))],
        out_shape=[jax.ShapeDtypeStruct((rows, D_MODEL), F32), jax.ShapeDtypeStruct(cb_shape, F32),
                   jax.ShapeDtypeStruct(st_shape, F32)],
        scratch_shapes=scratch,
        compiler_params=_params(),
        name="ssd_chained" if chained else "ssd_batched",
    )(*args)


def _row(v):
    return v.reshape(1, -1).astype(F32)


def _pad_lanes(a, width=LANES):
    return jnp.pad(a, [(0, 0)] * (a.ndim - 1) + [(0, width - a.shape[-1])])


def _band_slabs(w):
    eye = jnp.eye(LRU_BLOCKS, dtype=w.dtype)
    dense = (w[:, :, None, :] * eye[:, None, :, None]).reshape(LRU_WIDTH, LRU_WIDTH)
    slabs = []
    for j in range(LRU_TILES):
        st = min(max(j - 1, 0), LRU_TILES - LRU_BAND)
        slabs.append(dense[st * LANES:(st + LRU_BAND) * LANES, j * LANES:(j + 1) * LANES])
    return jnp.stack(slabs)


def _gla_params(norm_w, w_in, w_gate_up, b_gate, gnorm_w, w_out):
    n_main = 2 * GLA_QK + 2 * GLA_V
    return {
        "norm_w": _row(norm_w),
        "w_main": w_in[:, :n_main].astype(BF16),
        "w_glr": _pad_lanes(w_in[:, n_main:]).astype(BF16),
        "w_gu": jnp.pad(w_gate_up, ((0, LANES - GLA_RANK), (0, 0))).astype(BF16),
        "b_gate": _row(b_gate),
        "gnorm_w": _row(gnorm_w),
        "w_out": w_out.astype(BF16),
    }


def _lru_params(norm_w, w_in, conv_w, conv_b, w_a, b_a, w_x, b_x, lam, w_out):
    return {
        "norm_w": _row(norm_w),
        "w_in": w_in.astype(BF16),
        "conv_w": conv_w.astype(F32),
        "conv_b": _row(conv_b),
        "w_band": jnp.concatenate([_band_slabs(w_a), _band_slabs(w_x)], axis=-1).astype(BF16),
        "b_a": _row(b_a),
        "b_x": _row(b_x),
        "lam": _row(lam),
        "w_out": w_out.astype(BF16),
    }


def _ssd_params(norm_w, w_in, conv_w, conv_b, dt_bias, a_log, d_skip, snorm_w, w_out):
    heads = np.arange(SSD_DI) // SSD_HEADDIM
    expand = jnp.asarray(np.arange(LANES)[:, None] == heads[None, :], BF16)
    return {
        "norm_w": _row(norm_w),
        "w_z": w_in[:, :SSD_DI].astype(BF16),
        "w_xbc": w_in[:, SSD_DI:SSD_DI + SSD_CONV_DIM].astype(BF16),
        "w_dt": _pad_lanes(w_in[:, SSD_DI + SSD_CONV_DIM:]).astype(BF16),
        "conv_w": conv_w.astype(F32),
        "conv_b": _row(conv_b),
        "dt_bias": _pad_lanes(_row(dt_bias)),
        "a_log": _pad_lanes(_row(a_log)),
        "d_exp": _row(jnp.repeat(d_skip, SSD_HEADDIM)),
        "snorm_w": _row(snorm_w),
        "w_out": w_out.astype(BF16),
        "expand": expand,
    }


def _trunk(x, st_gla, st_lru_conv, st_lru_h, st_ssd_conv, st_ssd_h, layers, final_norm_w, *, T, c, chained):
    new = {"gla": [], "lru_conv": [], "lru_h": [], "ssd_conv": [], "ssd_h": []}
    n_layers = len(layers)
    for i, (kind, j, p) in enumerate(layers):
        if kind == "gla":
            fnw = final_norm_w if i == n_layers - 1 else None
            x, s = _gla_layer(x, None if chained else st_gla[j], p, T=T, c=c, chained=chained, final_norm_w=fnw)
            new["gla"].append(s)
        elif kind == "lru":
            x, cb, h = _lru_layer(x, None if chained else st_lru_conv[j], None if chained else st_lru_h[j], p,
                                  T=T, c=c, chained=chained)
            new["lru_conv"].append(cb)
            new["lru_h"].append(h)
        else:
            s0 = None if chained else st_ssd_h[j].reshape(-1, SSD_GROUPS, SSD_GW, SSD_STATE)
            x, cb, s = _ssd_layer(x, None if chained else st_ssd_conv[j], s0, p, T=T, c=c, chained=chained)
            new["ssd_conv"].append(cb)
            new["ssd_h"].append(s)
    return x, new


def kernel(x_prompt, x_sample, state_gla, state_lru_conv, state_lru_h, state_ssd_conv, state_ssd_h, norm_w, final_norm_w, gla_w_in, gla_w_gate_up, gla_b_gate, gla_norm_w, gla_w_out, lru_w_in, lru_conv_w, lru_conv_b, lru_w_a, lru_b_a, lru_w_x, lru_b_x, lru_lambda, lru_w_out, ssd_w_in, ssd_conv_w, ssd_conv_b, ssd_dt_bias, ssd_a_log, ssd_d, ssd_norm_w, ssd_w_out):
    depth = norm_w.shape[0]
    layers = []
    for i in range(depth):
        j = i // 3
        if i % 3 == 0:
            layers.append(("gla", j, _gla_params(norm_w[i], gla_w_in[j], gla_w_gate_up[j], gla_b_gate[j],
                                                 gla_norm_w[j], gla_w_out[j])))
        elif i % 3 == 1:
            layers.append(("lru", j, _lru_params(norm_w[i], lru_w_in[j], lru_conv_w[j], lru_conv_b[j], lru_w_a[j],
                                                 lru_b_a[j], lru_w_x[j], lru_b_x[j], lru_lambda[j], lru_w_out[j])))
        else:
            layers.append(("ssd", j, _ssd_params(norm_w[i], ssd_w_in[j], ssd_conv_w[j], ssd_conv_b[j],
                                                 ssd_dt_bias[j], ssd_a_log[j], ssd_d[j], ssd_norm_w[j],
                                                 ssd_w_out[j])))
    assert layers[-1][0] == "gla", "the final RMSNorm is fused into a GLA layer"
    fnw = _row(final_norm_w)

    bp, lp, _ = x_prompt.shape
    bs, ls, _ = x_sample.shape
    assert bp == 1 and lp % PROMPT_TILE == 0 and bs * ls == LANES and bs == SUBLANES

    yp, new_p = _trunk(x_prompt.reshape(lp, D_MODEL), None, None, None, None, None, layers, fnw,
                       T=PROMPT_TILE, c=min(CHUNK, lp), chained=True)
    ys, new_s = _trunk(x_sample.reshape(bs * ls, D_MODEL), state_gla, state_lru_conv, state_lru_h,
                       state_ssd_conv, state_ssd_h, layers, fnw, T=bs * ls, c=min(CHUNK, ls), chained=False)

    def stack(xs, shape):
        return jnp.stack(xs).reshape(shape)

    n_gla, n_lru, n_ssd = len(new_p["gla"]), len(new_p["lru_h"]), len(new_p["ssd_h"])
    return (
        yp.reshape(x_prompt.shape), ys.reshape(x_sample.shape),
        stack(new_p["gla"], (n_gla, 1, GLA_HEADS, GLA_DK, GLA_DV)),
        stack(new_p["lru_conv"], (n_lru, 1, CONV_W - 1, LRU_WIDTH)),
        stack(new_p["lru_h"], (n_lru, 1, LRU_WIDTH)),
        stack(new_p["ssd_conv"], (n_ssd, 1, CONV_W - 1, SSD_CONV_DIM)),
        stack(new_p["ssd_h"], (n_ssd, 1, SSD_HEADS, SSD_HEADDIM, SSD_STATE)),
        stack(new_s["gla"], (n_gla, bs, GLA_HEADS, GLA_DK, GLA_DV)),
        stack(new_s["lru_conv"], (n_lru, bs, CONV_W - 1, LRU_WIDTH)),
        stack(new_s["lru_h"], (n_lru, bs, LRU_WIDTH)),
        stack(new_s["ssd_conv"], (n_ssd, bs, CONV_W - 1, SSD_CONV_DIM)),
        stack(new_s["ssd_h"], (n_ssd, bs, SSD_HEADS, SSD_HEADDIM, SSD_STATE)),
    )
```

```python
import functools

import numpy as np
import jax
import jax.numpy as jnp
from jax import lax
from jax.experimental import pallas as pl
from jax.experimental.pallas import tpu as pltpu

F32 = jnp.float32
BF16 = jnp.bfloat16

D_MODEL = 1024
CHUNK = 64
EPS = 1e-6
CONV_W = 4

GLA_HEADS = 4
GLA_QK = D_MODEL // 2
GLA_V = D_MODEL
GLA_DK = GLA_QK // GLA_HEADS
GLA_DV = GLA_V // GLA_HEADS
GLA_RANK = 16
GLA_TAU = 16.0

LRU_WIDTH = 1408
LRU_BLOCKS = 16
LRU_BW = LRU_WIDTH // LRU_BLOCKS
LRU_C = 8.0

SSD_DI = 2 * D_MODEL
SSD_HEADDIM = 64
SSD_HEADS = SSD_DI // SSD_HEADDIM
SSD_STATE = 128
SSD_GROUPS = 4
SSD_HPG = SSD_HEADS // SSD_GROUPS
SSD_CONV_DIM = SSD_DI + 2 * SSD_GROUPS * SSD_STATE
SSD_GW = SSD_HPG * SSD_HEADDIM

LANES = 128
SUBLANES = 8
PROMPT_TILE = {"gla": 512, "lru": 512, "ssd": 256}
CUMSUM_ROWS = 256
VMEM_LIMIT_BYTES = 56 * 1024 * 1024
LRU_TILES = LRU_WIDTH // LANES
LRU_BAND = 3


def _bdot(a, b):
    return jnp.dot(a.astype(BF16), b.astype(BF16), preferred_element_type=F32)


def _bdot_nt(a, b):
    return lax.dot_general(a.astype(BF16), b.astype(BF16), (((1,), (1,)), ((), ())),
                           preferred_element_type=F32)


def _bdot_tn(a, b):
    return lax.dot_general(a.astype(BF16), b.astype(BF16), (((0,), (0,)), ((), ())),
                           preferred_element_type=F32)


def _split3(x):
    hi = x.astype(BF16)
    r1 = x - hi.astype(F32)
    mid = r1.astype(BF16)
    lo = (r1 - mid.astype(F32)).astype(BF16)
    return hi, mid, lo


def _sel_dot(sel, x):
    hi, mid, lo = _split3(x)
    d = lambda p: jnp.dot(sel, p, preferred_element_type=F32)
    return d(hi) + d(mid) + d(lo)


def _expand_heads(x, expand):
    hi = x.astype(BF16).astype(F32)
    r1 = x - hi
    mid = r1.astype(BF16).astype(F32)
    lane = lax.broadcasted_iota(jnp.int32, x.shape, 1)
    packed = jnp.where(lane < SSD_HEADS, hi, jnp.where(lane < 2 * SSD_HEADS, mid, r1 - mid))
    return jnp.dot(packed.astype(BF16), expand, preferred_element_type=F32)


def _sel_dot_nt(sel, x):
    hi, mid, lo = _split3(x)
    d = lambda p: lax.dot_general(sel, p, (((1,), (1,)), ((), ())), preferred_element_type=F32)
    return d(hi) + d(mid) + d(lo)


def _rms(x, w):
    return x * lax.rsqrt(jnp.mean(x * x, axis=-1, keepdims=True) + EPS) * w


def _softplus(x):
    return jnp.maximum(x, 0.0) + jnp.log1p(jnp.exp(-jnp.abs(x)))


def _silu(x):
    return x * jax.nn.sigmoid(x)


def _scan_pitch(seg):
    return seg if (seg // SUBLANES) % 2 == 1 else seg + SUBLANES


def _causal_conv(xb, xpad_s, xc_s, cw_ref, cb_ref, cbuf_in, cbuf_out, *, T, c, chained, act=None):
    nt = xb.shape[1] // LANES
    hdr = SUBLANES
    pad0 = hdr - (CONV_W - 1)
    blk = 2 * SUBLANES
    lanes = lambda j: slice(j * LANES, (j + 1) * LANES)
    if chained:
        seqs = [(0, 0, T)]

        @pl.when(pl.program_id(0) == 0)
        def _():
            xpad_s[:, 0:hdr, :] = jnp.zeros((nt, hdr, LANES), F32)
    else:
        seqs = [(b * (hdr + c), b * c, c) for b in range(T // c)]
        for b, (base, _, _) in enumerate(seqs):
            prev = cbuf_in[b]
            for j in range(nt):
                xpad_s[j, base + pad0:base + hdr, :] = prev[:, lanes(j)]
    for base, r0, n in seqs:
        for j in range(nt):
            xpad_s[j, base + hdr:base + hdr + n, :] = xb[r0:r0 + n, lanes(j)]
    for j in range(nt):
        w = [jnp.broadcast_to(cw_ref[k:k + 1, lanes(j)], (SUBLANES, LANES)) for k in range(CONV_W)]
        bias = jnp.broadcast_to(cb_ref[:, lanes(j)], (SUBLANES, LANES))
        for base, r0, n in seqs:
            for i in range(n // blk):
                p0 = base + pad0 + i * blk
                win = [xpad_s[j, pl.ds(p0 + m, SUBLANES, stride=2), :] for m in range(CONV_W + 1)]
                for par in range(2):
                    y = bias
                    for k in range(CONV_W):
                        y = y + w[k] * win[par + k]
                    xc_s[j, pl.ds(r0 + i * blk + par, SUBLANES, stride=2), :] = y if act is None else act(y)
    for b, (base, r0, n) in enumerate(seqs):
        tail = xb[r0 + n - (CONV_W - 1):r0 + n, :]
        if chained:
            cbuf_out[...] = tail
            for j in range(nt):
                xpad_s[j, pad0:hdr, :] = tail[:, lanes(j)]
        else:
            cbuf_out[b] = tail


def _gla_body(*refs, T, c, chained, final_norm):
    refs = list(refs)
    (x_ref, nw_ref, wm_ref, wglr_ref, wgu_ref, bg_ref, gnw_ref, wout_ref, tri_ref) = refs[:9]
    refs = refs[9:]
    s0_ref = None if chained else refs.pop(0)
    fnw_ref = refs.pop(0) if final_norm else None
    y_ref, sout_ref, proj_s, bcum_s, o_s = refs
    nc = T // c

    x = x_ref[...]
    hb = _rms(x, nw_ref[...]).astype(BF16)
    proj_s[...] = jnp.dot(hb, wm_ref[...], preferred_element_type=F32)
    glr = jnp.dot(hb, wglr_ref[...], preferred_element_type=F32)
    zg = _bdot(glr, wgu_ref[...]) + bg_ref[...]
    log_a = -_softplus(-zg) / GLA_TAU
    tb = tri_ref.shape[0]
    for b in range(T // tb):
        bcum_s[b * tb:(b + 1) * tb, :] = _sel_dot(tri_ref[...], log_a[b * tb:(b + 1) * tb, :])

    if chained:
        @pl.when(pl.program_id(0) == 0)
        def _():
            sout_ref[...] = jnp.zeros(sout_ref.shape, F32)

    tril = lax.broadcasted_iota(jnp.int32, (c, c), 0) >= lax.broadcasted_iota(jnp.int32, (c, c), 1)
    k_off, v_off, g_off = GLA_QK, 2 * GLA_QK, 2 * GLA_QK + GLA_V

    dec_cols = []
    for hd in range(GLA_HEADS):
        kc = hd * GLA_DK
        last_rows = [bcum_s[n * c + c - 1:n * c + c, kc:kc + GLA_DK] for n in range(nc)]
        blk = jnp.concatenate(last_rows + [jnp.zeros((GLA_DK - nc, GLA_DK), F32)], axis=0)
        dec_cols.append(jnp.exp(blk.T))

    def chunk(n, carry):
        rows = pl.ds(n * c, c)
        for hd in range(GLA_HEADS):
            kc = hd * GLA_DK
            vc = hd * GLA_DV
            bc = bcum_s[rows, kc:kc + GLA_DK]
            bl = bc[c - 1:c, :]
            q = proj_s[rows, kc:kc + GLA_DK]
            k = proj_s[rows, k_off + kc:k_off + kc + GLA_DK]
            v = proj_s[rows, v_off + vc:v_off + vc + GLA_DV]
            g = proj_s[rows, g_off + vc:g_off + vc + GLA_DV]
            qd = q * (GLA_DK ** -0.5) * jnp.exp(bc)
            kd = k * jnp.exp(-bc)
            ke = k * jnp.exp(bl - bc)
            att = jnp.where(tril, _bdot_nt(qd, kd), 0.0)
            s_prev = sout_ref[hd] if chained else s0_ref[n, hd]
            o = _bdot(att, v) + _bdot(qd, s_prev)
            ds = _bdot_tn(ke, v)
            s_new = dec_cols[hd][:, n:n + 1] * s_prev + ds
            if chained:
                sout_ref[hd] = s_new
            else:
                sout_ref[n, hd] = s_new
            o = o * lax.rsqrt(jnp.mean(o * o, axis=-1, keepdims=True) + EPS) * gnw_ref[...]
            o_s[rows, vc:vc + GLA_DV] = o * _silu(g)
        return carry

    for n in range(nc):
        chunk(n, 0)
    out = x + jnp.dot(o_s[...].astype(BF16), wout_ref[...], preferred_element_type=F32)
    if final_norm:
        out = _rms(out, fnw_ref[...])
    y_ref[...] = out


def _lru_body(*refs, T, c, chained):
    refs = list(refs)
    (x_ref, nw_ref, win_ref, cw_ref, cb_ref, wband_ref, ba_ref, bx_ref, lam_ref, wout_ref) = refs[:10]
    refs = refs[10:]
    if chained:
        cbuf_in = h0_ref = None
    else:
        cbuf_in = refs.pop(0)
        h0_ref = refs.pop(0)
    y_ref, cbuf_out, h_out, xpad_s, xc_s, a_s, u_s, hs_s, hin_s = refs
    W = LRU_WIDTH
    seg = T // SUBLANES

    x = x_ref[...]
    hb = _rms(x, nw_ref[...]).astype(BF16)
    proj = jnp.dot(hb, win_ref[...], preferred_element_type=F32)
    gate = proj[:, W:]
    _causal_conv(proj[:, :W], xpad_s, xc_s, cw_ref, cb_ref, cbuf_in, cbuf_out, T=T, c=c, chained=chained)

    pitch = _scan_pitch(seg)
    sp = _softplus(-lam_ref[...])
    for j in range(LRU_TILES):
        st = min(max(j - 1, 0), LRU_TILES - LRU_BAND)
        lo = j * LANES
        xin = jnp.concatenate([xc_s[st + m] for m in range(LRU_BAND)], axis=1)
        pre = _bdot(xin, wband_ref[j])
        r = jax.nn.sigmoid(pre[:, :LANES] + ba_ref[:, lo:lo + LANES])
        i = jax.nn.sigmoid(pre[:, LANES:] + bx_ref[:, lo:lo + LANES])
        log_a = -LRU_C * r * sp[:, lo:lo + LANES]
        th = jnp.tanh(log_a)
        one_minus_a2 = -2.0 * th / (1.0 - th)
        a = jnp.exp(log_a)
        u = jnp.sqrt(one_minus_a2) * (i * xc_s[j])
        for s in range(SUBLANES):
            a_s[j, s * pitch:s * pitch + seg, :] = a[s * seg:(s + 1) * seg, :]
            u_s[j, s * pitch:s * pitch + seg, :] = u[s * seg:(s + 1) * seg, :]

    def seg_rows(t):
        return pl.ds(t, SUBLANES, stride=pitch)

    tiles = range(LRU_TILES)
    if chained:
        @pl.when(pl.program_id(0) == 0)
        def _():
            h_out[...] = jnp.zeros(h_out.shape, F32)

        def local_scan(t, carry):
            hs, ps = carry
            a = [a_s[j, seg_rows(t), :] for j in tiles]
            return (tuple(a[j] * hs[j] + u_s[j, seg_rows(t), :] for j in tiles),
                    tuple(ps[j] * a[j] for j in tiles))

        h_end, p_end = lax.fori_loop(
            0, seg, local_scan,
            (tuple(jnp.zeros((SUBLANES, LANES), F32) for _ in tiles),
             tuple(jnp.ones((SUBLANES, LANES), F32) for _ in tiles)))
        h_end = jnp.concatenate(h_end, axis=1)
        p_end = jnp.concatenate(p_end, axis=1)
        hc = h_out[...]
        for s in range(SUBLANES):
            hin_s[s:s + 1, :] = hc
            hc = h_end[s:s + 1, :] + p_end[s:s + 1, :] * hc
        h_out[...] = hc
        h_init = hin_s[...]
    else:
        h_init = h0_ref[...]

    def full_scan(t, hs):
        new = []
        for j in tiles:
            h = a_s[j, seg_rows(t), :] * hs[j] + u_s[j, seg_rows(t), :]
            hs_s[j, seg_rows(t), :] = h
            new.append(h)
        return tuple(new)

    h_fin = lax.fori_loop(0, seg, full_scan, tuple(h_init[:, j * LANES:(j + 1) * LANES] for j in tiles))
    if not chained:
        h_out[...] = jnp.concatenate(h_fin, axis=1)

    gated = []
    for j in tiles:
        hs = jnp.concatenate([hs_s[j, s * pitch:s * pitch + seg, :] for s in range(SUBLANES)], axis=0)
        gated.append((hs * _silu(gate[:, j * LANES:(j + 1) * LANES])).astype(BF16))
    y_ref[...] = x + jnp.dot(jnp.concatenate(gated, axis=1), wout_ref[...], preferred_element_type=F32)


def _ssd_body(*refs, T, c, chained):
    refs = list(refs)
    (x_ref, nw_ref, wz_ref, wxbc_ref, wdt_ref, cw_ref, cb_ref, dtb_ref, alog_ref, de_ref, snw_ref,
     wout_ref, tri_ref, eye_ref, exp_ref) = refs[:15]
    refs = refs[15:]
    if chained:
        cbuf_in = s0_ref = None
    else:
        cbuf_in = refs.pop(0)
        s0_ref = refs.pop(0)
    if chained:
        (y_ref, cbuf_out, sout_ref, xpad_s, xbc_s, z_s, cum_s, dt_s, ecum_s, xw_s, yacc_s, cdec_s, st_s) = refs
    else:
        (y_ref, cbuf_out, sout_ref, xpad_s, xbc_s, z_s, cum_s, dt_s, ecum_s, xw_s, yacc_s, cdec_s) = refs
        st_s = None
    nc = T // c
    b_off = SSD_DI
    c_off = SSD_DI + SSD_GROUPS * SSD_STATE

    x = x_ref[...]
    hb = _rms(x, nw_ref[...]).astype(BF16)
    z_s[...] = jnp.dot(hb, wz_ref[...], preferred_element_type=F32)
    xbc_pre = jnp.dot(hb, wxbc_ref[...], preferred_element_type=F32)
    _causal_conv(xbc_pre, xpad_s, xbc_s, cw_ref, cb_ref, cbuf_in, cbuf_out, T=T, c=c, chained=chained, act=_silu)

    dt = _softplus(jnp.dot(hb, wdt_ref[...], preferred_element_type=F32) + dtb_ref[...])
    a_neg = -jnp.exp(alog_ref[...])
    cum = _sel_dot(tri_ref[...], dt * a_neg)
    cum_s[...] = cum
    dt_s[...] = dt
    lasts = [cum[n * c + c - 1:n * c + c, :] for n in range(nc)]
    cl = jnp.concatenate([jnp.broadcast_to(l, (c, LANES)) for l in lasts], axis=0)
    ecum_s[...] = _expand_heads(jnp.exp(cum), exp_ref[...])
    wb_e = _expand_heads(jnp.exp(cl - cum) * dt, exp_ref[...])
    for j in range(SSD_DI // LANES):
        xw_s[:, j * LANES:(j + 1) * LANES] = xbc_s[j] * wb_e[:, j * LANES:(j + 1) * LANES]
    cl_rows = jnp.concatenate(lasts + [jnp.zeros((SUBLANES - nc, LANES), F32)] * (nc < SUBLANES), axis=0)
    cdec_s[...] = _expand_heads(jnp.exp(cl_rows), exp_ref[...])

    if chained:
        @pl.when(pl.program_id(0) == 0)
        def _():
            st_s[...] = jnp.zeros(st_s.shape, F32)

    tril = lax.broadcasted_iota(jnp.int32, (c, c), 0) >= lax.broadcasted_iota(jnp.int32, (c, c), 1)
    low_half = lax.broadcasted_iota(jnp.int32, (c, LANES), 1) < SSD_HEADDIM

    def chunk(n, carry):
        rows = pl.ds(n * c, c)
        cum_c = cum_s[rows, :]
        cum_t = _sel_dot_nt(eye_ref[...], cum_c)
        dt_t = _sel_dot_nt(eye_ref[...], dt_s[rows, :])
        for g in range(SSD_GROUPS):
            gl = g * SSD_GW
            bm = xbc_s[b_off // LANES + g, rows, :]
            cm = xbc_s[c_off // LANES + g, rows, :]
            cb = _bdot_nt(cm, bm)
            if chained:
                st_prev = st_s[g]
            else:
                st_prev = s0_ref[n, g].T
            y_inter = _bdot(cm, st_prev) * ecum_s[rows, gl:gl + SSD_GW]
            for pr in range(SSD_HPG // 2):
                h0 = g * SSD_HPG + 2 * pr
                xl = h0 * SSD_HEADDIM
                xp = xbc_s[h0 // 2, rows, :]
                acc = y_inter[:, pr * LANES:(pr + 1) * LANES]
                for e in range(2):
                    h = h0 + e
                    seg = cum_c[:, h:h + 1] - cum_t[h:h + 1, :]
                    dec = jnp.exp(jnp.where(tril, seg, -jnp.inf))
                    wgt = cb * dec * dt_t[h:h + 1, :]
                    xm = jnp.where(low_half if e == 0 else jnp.logical_not(low_half), xp, 0.0)
                    acc = acc + _bdot(wgt, xm)
                yacc_s[rows, xl:xl + LANES] = acc
            ds = _bdot_tn(bm, xw_s[rows, gl:gl + SSD_GW])
            st_new = st_prev * cdec_s[pl.ds(n, 1), gl:gl + SSD_GW] + ds
            if chained:
                st_s[g] = st_new
            else:
                sout_ref[n, g] = st_new.T
        return carry

    for n in range(nc):
        chunk(n, 0)

    if chained:
        @pl.when(pl.program_id(0) == pl.num_programs(0) - 1)
        def _():
            for g in range(SSD_GROUPS):
                sout_ref[g] = st_s[g].T

    xs = jnp.concatenate([xbc_s[j] for j in range(SSD_DI // LANES)], axis=1)
    y = yacc_s[...] + de_ref[...] * xs
    yz = _rms(y * _silu(z_s[...]), snw_ref[...]).astype(BF16)
    y_ref[...] = x + jnp.dot(yz, wout_ref[...], preferred_element_type=F32)


def _const_spec(shape):
    nd = len(shape)
    return pl.BlockSpec(shape, lambda i, _nd=nd: (0,) * _nd, pipeline_mode=pl.Buffered(1))


def _row_spec(T, width):
    return pl.BlockSpec((T, width), lambda i: (i, 0))


def _params():
    return pltpu.CompilerParams(dimension_semantics=("arbitrary",), vmem_limit_bytes=VMEM_LIMIT_BYTES)


def _block_tri(T, c):
    r = np.arange(T)
    return jnp.asarray((r[:, None] // c == r[None, :] // c) & (r[:, None] >= r[None, :]), BF16)


def _eye():
    return jnp.asarray(np.eye(LANES), BF16)


def _gla_layer(x, s0, p, *, T, c, chained, final_norm_w=None):
    rows = x.shape[0]
    consts = [p["norm_w"], p["w_main"], p["w_glr"], p["w_gu"], p["b_gate"], p["gnorm_w"], p["w_out"],
              _block_tri(min(T, CUMSUM_ROWS), c)]
    args = [x] + consts
    specs = [_row_spec(T, D_MODEL)] + [_const_spec(a.shape) for a in consts]
    if not chained:
        args.append(s0)
        specs.append(_const_spec(s0.shape))
    if final_norm_w is not None:
        args.append(final_norm_w)
        specs.append(_const_spec(final_norm_w.shape))
    st_shape = (GLA_HEADS, GLA_DK, GLA_DV) if chained else s0.shape
    body = functools.partial(_gla_body, T=T, c=c, chained=chained, final_norm=final_norm_w is not None)
    return pl.pallas_call(
        body,
        grid=(rows // T,),
        in_specs=specs,
        out_specs=[_row_spec(T, D_MODEL), pl.BlockSpec(st_shape, lambda i, _n=len(st_shape): (0,) * _n)],
        out_shape=[jax.ShapeDtypeStruct((rows, D_MODEL), F32), jax.ShapeDtypeStruct(st_shape, F32)],
        scratch_shapes=[pltpu.VMEM((T, 2 * GLA_QK + 2 * GLA_V), F32), pltpu.VMEM((T, GLA_QK), F32),
                        pltpu.VMEM((T, GLA_V), F32)],
        compiler_params=_params(),
        name="gla_chained" if chained else "gla_batched",
    )(*args)


def _lru_layer(x, cbuf, h0, p, *, T, c, chained):
    rows = x.shape[0]
    W = LRU_WIDTH
    consts = [p["norm_w"], p["w_in"], p["conv_w"], p["conv_b"], p["w_band"], p["b_a"], p["b_x"], p["lam"], p["w_out"]]
    args = [x] + consts
    specs = [_row_spec(T, D_MODEL)] + [_const_spec(a.shape) for a in consts]
    if chained:
        cb_shape, h_shape = (CONV_W - 1, W), (1, W)
        pad_rows = SUBLANES + T
    else:
        args += [cbuf, h0]
        specs += [_const_spec(cbuf.shape), _const_spec(h0.shape)]
        cb_shape, h_shape = cbuf.shape, h0.shape
        pad_rows = (T // c) * (SUBLANES + c)
    scan_rows = SUBLANES * _scan_pitch(T // SUBLANES)
    body = functools.partial(_lru_body, T=T, c=c, chained=chained)
    zero_map = lambda n: (lambda i: (0,) * n)
    return pl.pallas_call(
        body,
        grid=(rows // T,),
        in_specs=specs,
        out_specs=[_row_spec(T, D_MODEL), pl.BlockSpec(cb_shape, zero_map(len(cb_shape))),
                   pl.BlockSpec(h_shape, zero_map(len(h_shape)))],
        out_shape=[jax.ShapeDtypeStruct((rows, D_MODEL), F32), jax.ShapeDtypeStruct(cb_shape, F32),
                   jax.ShapeDtypeStruct(h_shape, F32)],
        scratch_shapes=[pltpu.VMEM((LRU_TILES, pad_rows, LANES), F32), pltpu.VMEM((LRU_TILES, T, LANES), F32)]
                       + [pltpu.VMEM((LRU_TILES, scan_rows, LANES), F32)] * 3 + [pltpu.VMEM((SUBLANES, W), F32)],
        compiler_params=_params(),
        name="lru_chained" if chained else "lru_batched",
    )(*args)


def _ssd_layer(x, cbuf, s0, p, *, T, c, chained):
    rows = x.shape[0]
    consts = [p["norm_w"], p["w_z"], p["w_xbc"], p["w_dt"], p["conv_w"], p["conv_b"], p["dt_bias"], p["a_log"],
              p["d_exp"], p["snorm_w"], p["w_out"], _block_tri(T, c), _eye(), p["expand"]]
    args = [x] + consts
    specs = [_row_spec(T, D_MODEL)] + [_const_spec(a.shape) for a in consts]
    if chained:
        cb_shape = (CONV_W - 1, SSD_CONV_DIM)
        st_shape = (SSD_GROUPS, SSD_GW, SSD_STATE)
        pad_rows = SUBLANES + T
    else:
        args += [cbuf, s0]
        specs += [_const_spec(cbuf.shape), _const_spec(s0.shape)]
        cb_shape, st_shape = cbuf.shape, s0.shape
        pad_rows = (T // c) * (SUBLANES + c)
    conv_tiles = SSD_CONV_DIM // LANES
    scratch = [pltpu.VMEM((conv_tiles, pad_rows, LANES), F32), pltpu.VMEM((conv_tiles, T, LANES), F32),
               pltpu.VMEM((T, SSD_DI), F32), pltpu.VMEM((T, LANES), F32), pltpu.VMEM((T, LANES), F32),
               pltpu.VMEM((T, SSD_DI), F32), pltpu.VMEM((T, SSD_DI), F32), pltpu.VMEM((T, SSD_DI), F32),
               pltpu.VMEM((SUBLANES, SSD_DI), F32)]
    if chained:
        scratch.append(pltpu.VMEM((SSD_GROUPS, SSD_STATE, SSD_GW), F32))
    body = functools.partial(_ssd_body, T=T, c=c, chained=chained)
    zero_map = lambda n: (lambda i: (0,) * n)
    return pl.pallas_call(
        body,
        grid=(rows // T,),
        in_specs=specs,
        out_specs=[_row_spec(T, D_MODEL), pl.BlockSpec(cb_shape, zero_map(len(cb_shape))),
                   pl.BlockSpec(st_shape, zero_map(len(st_shape)))],
        out_shape=[jax.ShapeDtypeStruct((rows, D_MODEL), F32), jax.ShapeDtypeStruct(cb_shape, F32),
                   jax.ShapeDtypeStruct(st_shape, F32)],
        scratch_shapes=scratch,
        compiler_params=_params(),
        name="ssd_chained" if chained else "ssd_batched",
    )(*args)


def _row(v):
    return v.reshape(1, -1).astype(F32)


def _pad_lanes(a, width=LANES):
    return jnp.pad(a, [(0, 0)] * (a.ndim - 1) + [(0, width - a.shape[-1])])


def _band_slabs(w):
    eye = jnp.eye(LRU_BLOCKS, dtype=w.dtype)
    dense = (w[:, :, None, :] * eye[:, None, :, None]).reshape(LRU_WIDTH, LRU_WIDTH)
    slabs = []
    for j in range(LRU_TILES):
        st = min(max(j - 1, 0), LRU_TILES - LRU_BAND)
        slabs.append(dense[st * LANES:(st + LRU_BAND) * LANES, j * LANES:(j + 1) * LANES])
    return jnp.stack(slabs)


def _gla_params(norm_w, w_in, w_gate_up, b_gate, gnorm_w, w_out):
    n_main = 2 * GLA_QK + 2 * GLA_V
    return {
        "norm_w": _row(norm_w),
        "w_main": w_in[:, :n_main].astype(BF16),
        "w_glr": _pad_lanes(w_in[:, n_main:]).astype(BF16),
        "w_gu": jnp.pad(w_gate_up, ((0, LANES - GLA_RANK), (0, 0))).astype(BF16),
        "b_gate": _row(b_gate),
        "gnorm_w": _row(gnorm_w),
        "w_out": w_out.astype(BF16),
    }


def _lru_params(norm_w, w_in, conv_w, conv_b, w_a, b_a, w_x, b_x, lam, w_out):
    return {
        "norm_w": _row(norm_w),
        "w_in": w_in.astype(BF16),
        "conv_w": conv_w.astype(F32),
        "conv_b": _row(conv_b),
        "w_band": jnp.concatenate([_band_slabs(w_a), _band_slabs(w_x)], axis=-1).astype(BF16),
        "b_a": _row(b_a),
        "b_x": _row(b_x),
        "lam": _row(lam),
        "w_out": w_out.astype(BF16),
    }


def _ssd_params(norm_w, w_in, conv_w, conv_b, dt_bias, a_log, d_skip, snorm_w, w_out):
    heads = np.arange(SSD_DI) // SSD_HEADDIM
    lane = np.arange(LANES)
    expand = jnp.asarray((lane[:, None] % SSD_HEADS == heads[None, :]) & (lane[:, None] < 3 * SSD_HEADS), BF16)
    reps = LANES // SSD_HEADS
    return {
        "norm_w": _row(norm_w),
        "w_z": w_in[:, :SSD_DI].astype(BF16),
        "w_xbc": w_in[:, SSD_DI:SSD_DI + SSD_CONV_DIM].astype(BF16),
        "w_dt": jnp.tile(w_in[:, SSD_DI + SSD_CONV_DIM:], (1, reps)).astype(BF16),
        "conv_w": conv_w.astype(F32),
        "conv_b": _row(conv_b),
        "dt_bias": jnp.tile(_row(dt_bias), (1, reps)),
        "a_log": jnp.tile(_row(a_log), (1, reps)),
        "d_exp": _row(jnp.repeat(d_skip, SSD_HEADDIM)),
        "snorm_w": _row(snorm_w),
        "w_out": w_out.astype(BF16),
        "expand": expand,
    }


def _trunk(x, st_gla, st_lru_conv, st_lru_h, st_ssd_conv, st_ssd_h, layers, final_norm_w, *, tiles, c, chained):
    new = {"gla": [], "lru_conv": [], "lru_h": [], "ssd_conv": [], "ssd_h": []}
    n_layers = len(layers)
    for i, (kind, j, p) in enumerate(layers):
        T = tiles[kind]
        if kind == "gla":
            fnw = final_norm_w if i == n_layers - 1 else None
            x, s = _gla_layer(x, None if chained else st_gla[j], p, T=T, c=c, chained=chained, final_norm_w=fnw)
            new["gla"].append(s)
        elif kind == "lru":
            x, cb, h = _lru_layer(x, None if chained else st_lru_conv[j], None if chained else st_lru_h[j], p,
                                  T=T, c=c, chained=chained)
            new["lru_conv"].append(cb)
            new["lru_h"].append(h)
        else:
            s0 = None if chained else st_ssd_h[j].reshape(-1, SSD_GROUPS, SSD_GW, SSD_STATE)
            x, cb, s = _ssd_layer(x, None if chained else st_ssd_conv[j], s0, p, T=T, c=c, chained=chained)
            new["ssd_conv"].append(cb)
            new["ssd_h"].append(s)
    return x, new


def kernel(x_prompt, x_sample, state_gla, state_lru_conv, state_lru_h, state_ssd_conv, state_ssd_h, norm_w, final_norm_w, gla_w_in, gla_w_gate_up, gla_b_gate, gla_norm_w, gla_w_out, lru_w_in, lru_conv_w, lru_conv_b, lru_w_a, lru_b_a, lru_w_x, lru_b_x, lru_lambda, lru_w_out, ssd_w_in, ssd_conv_w, ssd_conv_b, ssd_dt_bias, ssd_a_log, ssd_d, ssd_norm_w, ssd_w_out):
    depth = norm_w.shape[0]
    layers = []
    for i in range(depth):
        j = i // 3
        if i % 3 == 0:
            layers.append(("gla", j, _gla_params(norm_w[i], gla_w_in[j], gla_w_gate_up[j], gla_b_gate[j],
                                                 gla_norm_w[j], gla_w_out[j])))
        elif i % 3 == 1:
            layers.append(("lru", j, _lru_params(norm_w[i], lru_w_in[j], lru_conv_w[j], lru_conv_b[j], lru_w_a[j],
                                                 lru_b_a[j], lru_w_x[j], lru_b_x[j], lru_lambda[j], lru_w_out[j])))
        else:
            layers.append(("ssd", j, _ssd_params(norm_w[i], ssd_w_in[j], ssd_conv_w[j], ssd_conv_b[j],
                                                 ssd_dt_bias[j], ssd_a_log[j], ssd_d[j], ssd_norm_w[j],
                                                 ssd_w_out[j])))
    assert layers[-1][0] == "gla", "the final RMSNorm is fused into a GLA layer"
    fnw = _row(final_norm_w)

    bp, lp, _ = x_prompt.shape
    bs, ls, _ = x_sample.shape
    p_tiles = {k: min(t, lp) for k, t in PROMPT_TILE.items()}
    assert bp == 1 and all(lp % t == 0 for t in p_tiles.values()) and bs * ls == LANES and bs == SUBLANES

    yp, new_p = _trunk(x_prompt.reshape(lp, D_MODEL), None, None, None, None, None, layers, fnw,
                       tiles=p_tiles, c=min(CHUNK, lp), chained=True)
    ys, new_s = _trunk(x_sample.reshape(bs * ls, D_MODEL), state_gla, state_lru_conv, state_lru_h,
                       state_ssd_conv, state_ssd_h, layers, fnw, tiles=dict.fromkeys(PROMPT_TILE, bs * ls),
                       c=min(CHUNK, ls), chained=False)

    def stack(xs, shape):
        return jnp.stack(xs).reshape(shape)

    n_gla, n_lru, n_ssd = len(new_p["gla"]), len(new_p["lru_h"]), len(new_p["ssd_h"])
    return (
        yp.reshape(x_prompt.shape), ys.reshape(x_sample.shape),
        stack(new_p["gla"], (n_gla, 1, GLA_HEADS, GLA_DK, GLA_DV)),
        stack(new_p["lru_conv"], (n_lru, 1, CONV_W - 1, LRU_WIDTH)),
        stack(new_p["lru_h"], (n_lru, 1, LRU_WIDTH)),
        stack(new_p["ssd_conv"], (n_ssd, 1, CONV_W - 1, SSD_CONV_DIM)),
        stack(new_p["ssd_h"], (n_ssd, 1, SSD_HEADS, SSD_HEADDIM, SSD_STATE)),
        stack(new_s["gla"], (n_gla, bs, GLA_HEADS, GLA_DK, GLA_DV)),
        stack(new_s["lru_conv"], (n_lru, bs, CONV_W - 1, LRU_WIDTH)),
        stack(new_s["lru_h"], (n_lru, bs, LRU_WIDTH)),
        stack(new_s["ssd_conv"], (n_ssd, bs, CONV_W - 1, SSD_CONV_DIM)),
        stack(new_s["ssd_h"], (n_ssd, bs, SSD_HEADS, SSD_HEADDIM, SSD_STATE)),
    )
```

```python
import functools

import numpy as np
import jax
import jax.numpy as jnp
from jax import lax
from jax.experimental import pallas as pl
from jax.experimental.pallas import tpu as pltpu

F32 = jnp.float32
BF16 = jnp.bfloat16

D_MODEL = 1024
CHUNK = 64
EPS = 1e-6
CONV_W = 4

GLA_HEADS = 4
GLA_QK = D_MODEL // 2
GLA_V = D_MODEL
GLA_DK = GLA_QK // GLA_HEADS
GLA_DV = GLA_V // GLA_HEADS
GLA_RANK = 16
GLA_TAU = 16.0

LRU_WIDTH = 1408
LRU_BLOCKS = 16
LRU_BW = LRU_WIDTH // LRU_BLOCKS
LRU_C = 8.0

SSD_DI = 2 * D_MODEL
SSD_HEADDIM = 64
SSD_HEADS = SSD_DI // SSD_HEADDIM
SSD_STATE = 128
SSD_GROUPS = 4
SSD_HPG = SSD_HEADS // SSD_GROUPS
SSD_CONV_DIM = SSD_DI + 2 * SSD_GROUPS * SSD_STATE
SSD_GW = SSD_HPG * SSD_HEADDIM

LANES = 128
SUBLANES = 8
PROMPT_TILE = {"gla": 512, "lru": 512, "ssd": 256}
PROJ_SLICE = 256
CUMSUM_ROWS = 256
VMEM_LIMIT_BYTES = 56 * 1024 * 1024
LRU_TILES = LRU_WIDTH // LANES
LRU_BAND = 3


def _bdot(a, b):
    return jnp.dot(a.astype(BF16), b.astype(BF16), preferred_element_type=F32)


def _bdot_nt(a, b):
    return lax.dot_general(a.astype(BF16), b.astype(BF16), (((1,), (1,)), ((), ())),
                           preferred_element_type=F32)


def _bdot_tn(a, b):
    return lax.dot_general(a.astype(BF16), b.astype(BF16), (((0,), (0,)), ((), ())),
                           preferred_element_type=F32)


def _split3(x):
    hi = x.astype(BF16)
    r1 = x - hi.astype(F32)
    mid = r1.astype(BF16)
    lo = (r1 - mid.astype(F32)).astype(BF16)
    return hi, mid, lo


def _sel_dot(sel, x):
    hi, mid, lo = _split3(x)
    d = lambda p: jnp.dot(sel, p, preferred_element_type=F32)
    return d(hi) + d(mid) + d(lo)


def _expand_heads(x, expand):
    hi = x.astype(BF16).astype(F32)
    r1 = x - hi
    mid = r1.astype(BF16).astype(F32)
    lane = lax.broadcasted_iota(jnp.int32, x.shape, 1)
    packed = jnp.where(lane < SSD_HEADS, hi, jnp.where(lane < 2 * SSD_HEADS, mid, r1 - mid))
    return jnp.dot(packed.astype(BF16), expand, preferred_element_type=F32)


def _sel_dot_nt(sel, x):
    hi, mid, lo = _split3(x)
    d = lambda p: lax.dot_general(sel, p, (((1,), (1,)), ((), ())), preferred_element_type=F32)
    return d(hi) + d(mid) + d(lo)


def _rms(x, w):
    return x * lax.rsqrt(jnp.mean(x * x, axis=-1, keepdims=True) + EPS) * w


def _softplus(x):
    return jnp.maximum(x, 0.0) + jnp.log1p(jnp.exp(-jnp.abs(x)))


def _silu(x):
    return x * jax.nn.sigmoid(x)


class _Interleave:
    def __init__(self, pieces, slots):
        self._pieces, self._slots, self._calls, self._total = list(pieces), slots, 0, None

    def take(self, k):
        for _ in range(min(k, len(self._pieces))):
            self._pieces.pop(0)()

    def step(self):
        if self._total is None:
            self._total = len(self._pieces)
        self._calls += 1
        left_after = self._total - self._calls * self._total // self._slots
        self.take(len(self._pieces) - max(left_after, 0))

    def flush(self):
        self.take(len(self._pieces))


def _proj_slice(dst_ref, hb, w_ref, lo):
    hi = min(lo + PROJ_SLICE, w_ref.shape[1])
    dst_ref[:, lo:hi] = jnp.dot(hb, w_ref[:, lo:hi], preferred_element_type=F32)


def _scan_pitch(seg):
    return seg if (seg // SUBLANES) % 2 == 1 else seg + SUBLANES


def _causal_conv(xb, xpad_s, xc_s, cw_ref, cb_ref, cbuf_in, cbuf_out, *, T, c, chained, act=None):
    nt = xb.shape[1] // LANES
    hdr = SUBLANES
    pad0 = hdr - (CONV_W - 1)
    blk = 2 * SUBLANES
    lanes = lambda j: slice(j * LANES, (j + 1) * LANES)
    if chained:
        seqs = [(0, 0, T)]

        @pl.when(pl.program_id(0) == 0)
        def _():
            xpad_s[:, 0:hdr, :] = jnp.zeros((nt, hdr, LANES), F32)
    else:
        seqs = [(b * (hdr + c), b * c, c) for b in range(T // c)]
        for b, (base, _, _) in enumerate(seqs):
            prev = cbuf_in[b]
            for j in range(nt):
                xpad_s[j, base + pad0:base + hdr, :] = prev[:, lanes(j)]
    for base, r0, n in seqs:
        for j in range(nt):
            xpad_s[j, base + hdr:base + hdr + n, :] = xb[r0:r0 + n, lanes(j)]
    for j in range(nt):
        w = [jnp.broadcast_to(cw_ref[k:k + 1, lanes(j)], (SUBLANES, LANES)) for k in range(CONV_W)]
        bias = jnp.broadcast_to(cb_ref[:, lanes(j)], (SUBLANES, LANES))
        for base, r0, n in seqs:
            for i in range(n // blk):
                p0 = base + pad0 + i * blk
                win = [xpad_s[j, pl.ds(p0 + m, SUBLANES, stride=2), :] for m in range(CONV_W + 1)]
                for par in range(2):
                    y = bias
                    for k in range(CONV_W):
                        y = y + w[k] * win[par + k]
                    xc_s[j, pl.ds(r0 + i * blk + par, SUBLANES, stride=2), :] = y if act is None else act(y)
    for b, (base, r0, n) in enumerate(seqs):
        tail = xb[r0 + n - (CONV_W - 1):r0 + n, :]
        if chained:
            cbuf_out[...] = tail
            for j in range(nt):
                xpad_s[j, pad0:hdr, :] = tail[:, lanes(j)]
        else:
            cbuf_out[b] = tail


def _gla_body(*refs, T, c, chained, final_norm):
    refs = list(refs)
    (x_ref, nw_ref, wm_ref, wglr_ref, wgu_ref, bg_ref, gnw_ref, wout_ref, tri_ref) = refs[:9]
    refs = refs[9:]
    s0_ref = None if chained else refs.pop(0)
    fnw_ref = refs.pop(0) if final_norm else None
    if chained:
        y_ref, sout_ref, proj_s, bcum_s, o_s, glr_s, xs_s, pin_s = refs
    else:
        y_ref, sout_ref, proj_s, bcum_s, o_s = refs
    nc = T // c

    if chained:
        @pl.when(pl.program_id(0) == 0)
        def _():
            sout_ref[...] = jnp.zeros(sout_ref.shape, F32)
            proj_s[...] = jnp.zeros(proj_s.shape, F32)
            glr_s[...] = jnp.zeros(glr_s.shape, F32)
            xs_s[...] = jnp.zeros(xs_s.shape, F32)

    x_in = x_ref[...]
    hb = _rms(x_in, nw_ref[...]).astype(BF16)
    glr_in = jnp.dot(hb, wglr_ref[...], preferred_element_type=F32)
    if chained:
        x = xs_s[...]
        glr = glr_s[...]
        front = _Interleave([functools.partial(_proj_slice, pin_s, hb, wm_ref, lo)
                             for lo in range(0, wm_ref.shape[1], PROJ_SLICE)], nc * GLA_HEADS)
    else:
        x, glr = x_in, glr_in
        proj_s[...] = jnp.dot(hb, wm_ref[...], preferred_element_type=F32)
        front = _Interleave([], nc * GLA_HEADS)

    zg = _bdot(glr, wgu_ref[...]) + bg_ref[...]
    front.take(2)
    log_a = -_softplus(-zg) / GLA_TAU
    tb = tri_ref.shape[0]
    for b in range(T // tb):
        bcum_s[b * tb:(b + 1) * tb, :] = _sel_dot(tri_ref[...], log_a[b * tb:(b + 1) * tb, :])

    tril = lax.broadcasted_iota(jnp.int32, (c, c), 0) >= lax.broadcasted_iota(jnp.int32, (c, c), 1)
    k_off, v_off, g_off = GLA_QK, 2 * GLA_QK, 2 * GLA_QK + GLA_V

    dec_cols = []
    for hd in range(GLA_HEADS):
        kc = hd * GLA_DK
        last_rows = [bcum_s[n * c + c - 1:n * c + c, kc:kc + GLA_DK] for n in range(nc)]
        blk = jnp.concatenate(last_rows + [jnp.zeros((GLA_DK - nc, GLA_DK), F32)], axis=0)
        dec_cols.append(jnp.exp(blk.T))

    def chunk(n, carry):
        rows = pl.ds(n * c, c)
        for hd in range(GLA_HEADS):
            kc = hd * GLA_DK
            vc = hd * GLA_DV
            bc = bcum_s[rows, kc:kc + GLA_DK]
            bl = bc[c - 1:c, :]
            q = proj_s[rows, kc:kc + GLA_DK]
            k = proj_s[rows, k_off + kc:k_off + kc + GLA_DK]
            v = proj_s[rows, v_off + vc:v_off + vc + GLA_DV]
            g = proj_s[rows, g_off + vc:g_off + vc + GLA_DV]
            qd = q * (GLA_DK ** -0.5) * jnp.exp(bc)
            kd = k * jnp.exp(-bc)
            ke = k * jnp.exp(bl - bc)
            att = jnp.where(tril, _bdot_nt(qd, kd), 0.0)
            s_prev = sout_ref[hd] if chained else s0_ref[n, hd]
            o = _bdot(att, v) + _bdot(qd, s_prev)
            ds = _bdot_tn(ke, v)
            s_new = dec_cols[hd][:, n:n + 1] * s_prev + ds
            if chained:
                sout_ref[hd] = s_new
            else:
                sout_ref[n, hd] = s_new
            o = o * lax.rsqrt(jnp.mean(o * o, axis=-1, keepdims=True) + EPS) * gnw_ref[...]
            o_s[rows, vc:vc + GLA_DV] = o * _silu(g)
            front.step()
        return carry

    for n in range(nc):
        chunk(n, 0)
    front.flush()
    out = x + jnp.dot(o_s[...].astype(BF16), wout_ref[...], preferred_element_type=F32)
    if final_norm:
        out = _rms(out, fnw_ref[...])
    y_ref[...] = out
    if chained:
        proj_s[...] = pin_s[...]
        glr_s[...] = glr_in
        xs_s[...] = x_in


def _lru_body(*refs, T, c, chained):
    refs = list(refs)
    (x_ref, nw_ref, win_ref, cw_ref, cb_ref, wband_ref, ba_ref, bx_ref, lam_ref, wout_ref) = refs[:10]
    refs = refs[10:]
    if chained:
        cbuf_in = h0_ref = None
    else:
        cbuf_in = refs.pop(0)
        h0_ref = refs.pop(0)
    y_ref, cbuf_out, h_out, xpad_s, xc_s, a_s, u_s, hs_s, hin_s = refs
    W = LRU_WIDTH
    seg = T // SUBLANES

    x = x_ref[...]
    hb = _rms(x, nw_ref[...]).astype(BF16)
    proj = jnp.dot(hb, win_ref[...], preferred_element_type=F32)
    gate = proj[:, W:]
    _causal_conv(proj[:, :W], xpad_s, xc_s, cw_ref, cb_ref, cbuf_in, cbuf_out, T=T, c=c, chained=chained)

    pitch = _scan_pitch(seg)
    sp = _softplus(-lam_ref[...])
    for j in range(LRU_TILES):
        st = min(max(j - 1, 0), LRU_TILES - LRU_BAND)
        lo = j * LANES
        xin = jnp.concatenate([xc_s[st + m] for m in range(LRU_BAND)], axis=1)
        pre = _bdot(xin, wband_ref[j])
        r = jax.nn.sigmoid(pre[:, :LANES] + ba_ref[:, lo:lo + LANES])
        i = jax.nn.sigmoid(pre[:, LANES:] + bx_ref[:, lo:lo + LANES])
        log_a = -LRU_C * r * sp[:, lo:lo + LANES]
        th = jnp.tanh(log_a)
        one_minus_a2 = -2.0 * th / (1.0 - th)
        a = jnp.exp(log_a)
        u = jnp.sqrt(one_minus_a2) * (i * xc_s[j])
        for s in range(SUBLANES):
            a_s[j, s * pitch:s * pitch + seg, :] = a[s * seg:(s + 1) * seg, :]
            u_s[j, s * pitch:s * pitch + seg, :] = u[s * seg:(s + 1) * seg, :]

    def seg_rows(t):
        return pl.ds(t, SUBLANES, stride=pitch)

    tiles = range(LRU_TILES)
    if chained:
        @pl.when(pl.program_id(0) == 0)
        def _():
            h_out[...] = jnp.zeros(h_out.shape, F32)

        def local_scan(t, carry):
            hs, ps = carry
            a = [a_s[j, seg_rows(t), :] for j in tiles]
            return (tuple(a[j] * hs[j] + u_s[j, seg_rows(t), :] for j in tiles),
                    tuple(ps[j] * a[j] for j in tiles))

        h_end, p_end = lax.fori_loop(
            0, seg, local_scan,
            (tuple(jnp.zeros((SUBLANES, LANES), F32) for _ in tiles),
             tuple(jnp.ones((SUBLANES, LANES), F32) for _ in tiles)))
        h_end = jnp.concatenate(h_end, axis=1)
        p_end = jnp.concatenate(p_end, axis=1)
        hc = h_out[...]
        for s in range(SUBLANES):
            hin_s[s:s + 1, :] = hc
            hc = h_end[s:s + 1, :] + p_end[s:s + 1, :] * hc
        h_out[...] = hc
        h_init = hin_s[...]
    else:
        h_init = h0_ref[...]

    def full_scan(t, hs):
        new = []
        for j in tiles:
            h = a_s[j, seg_rows(t), :] * hs[j] + u_s[j, seg_rows(t), :]
            hs_s[j, seg_rows(t), :] = h
            new.append(h)
        return tuple(new)

    h_fin = lax.fori_loop(0, seg, full_scan, tuple(h_init[:, j * LANES:(j + 1) * LANES] for j in tiles))
    if not chained:
        h_out[...] = jnp.concatenate(h_fin, axis=1)

    gated = []
    for j in tiles:
        hs = jnp.concatenate([hs_s[j, s * pitch:s * pitch + seg, :] for s in range(SUBLANES)], axis=0)
        gated.append((hs * _silu(gate[:, j * LANES:(j + 1) * LANES])).astype(BF16))
    y_ref[...] = x + jnp.dot(jnp.concatenate(gated, axis=1), wout_ref[...], preferred_element_type=F32)


def _ssd_body(*refs, T, c, chained):
    refs = list(refs)
    (x_ref, nw_ref, wz_ref, wxbc_ref, wdt_ref, cw_ref, cb_ref, dtb_ref, alog_ref, de_ref, snw_ref,
     wout_ref, tri_ref, eye_ref, exp_ref) = refs[:15]
    refs = refs[15:]
    if chained:
        cbuf_in = s0_ref = None
    else:
        cbuf_in = refs.pop(0)
        s0_ref = refs.pop(0)
    if chained:
        (y_ref, cbuf_out, sout_ref, xpad_s, xbc_s, z_s, cum_s, dt_s, ecum_s, xw_s, yacc_s, cdec_s, st_s) = refs
    else:
        (y_ref, cbuf_out, sout_ref, xpad_s, xbc_s, z_s, cum_s, dt_s, ecum_s, xw_s, yacc_s, cdec_s) = refs
        st_s = None
    nc = T // c
    b_off = SSD_DI
    c_off = SSD_DI + SSD_GROUPS * SSD_STATE

    x = x_ref[...]
    hb = _rms(x, nw_ref[...]).astype(BF16)
    z_s[...] = jnp.dot(hb, wz_ref[...], preferred_element_type=F32)
    xbc_pre = jnp.dot(hb, wxbc_ref[...], preferred_element_type=F32)
    _causal_conv(xbc_pre, xpad_s, xbc_s, cw_ref, cb_ref, cbuf_in, cbuf_out, T=T, c=c, chained=chained, act=_silu)

    dt = _softplus(jnp.dot(hb, wdt_ref[...], preferred_element_type=F32) + dtb_ref[...])
    a_neg = -jnp.exp(alog_ref[...])
    cum = _sel_dot(tri_ref[...], dt * a_neg)
    cum_s[...] = cum
    dt_s[...] = dt
    lasts = [cum[n * c + c - 1:n * c + c, :] for n in range(nc)]
    cl = jnp.concatenate([jnp.broadcast_to(l, (c, LANES)) for l in lasts], axis=0)
    ecum_s[...] = _expand_heads(jnp.exp(cum), exp_ref[...])
    wb_e = _expand_heads(jnp.exp(cl - cum) * dt, exp_ref[...])
    for j in range(SSD_DI // LANES):
        xw_s[:, j * LANES:(j + 1) * LANES] = xbc_s[j] * wb_e[:, j * LANES:(j + 1) * LANES]
    cl_rows = jnp.concatenate(lasts + [jnp.zeros((SUBLANES - nc, LANES), F32)] * (nc < SUBLANES), axis=0)
    cdec_s[...] = _expand_heads(jnp.exp(cl_rows), exp_ref[...])

    if chained:
        @pl.when(pl.program_id(0) == 0)
        def _():
            st_s[...] = jnp.zeros(st_s.shape, F32)

    tril = lax.broadcasted_iota(jnp.int32, (c, c), 0) >= lax.broadcasted_iota(jnp.int32, (c, c), 1)
    low_half = lax.broadcasted_iota(jnp.int32, (c, LANES), 1) < SSD_HEADDIM

    def chunk(n, carry):
        rows = pl.ds(n * c, c)
        cum_c = cum_s[rows, :]
        cum_t = _sel_dot_nt(eye_ref[...], cum_c)
        dt_t = _sel_dot_nt(eye_ref[...], dt_s[rows, :])
        for g in range(SSD_GROUPS):
            gl = g * SSD_GW
            bm = xbc_s[b_off // LANES + g, rows, :]
            cm = xbc_s[c_off // LANES + g, rows, :]
            cb = _bdot_nt(cm, bm)
            if chained:
                st_prev = st_s[g]
            else:
                st_prev = s0_ref[n, g].T
            y_inter = _bdot(cm, st_prev) * ecum_s[rows, gl:gl + SSD_GW]
            for pr in range(SSD_HPG // 2):
                h0 = g * SSD_HPG + 2 * pr
                xl = h0 * SSD_HEADDIM
                xp = xbc_s[h0 // 2, rows, :]
                acc = y_inter[:, pr * LANES:(pr + 1) * LANES]
                for e in range(2):
                    h = h0 + e
                    seg = cum_c[:, h:h + 1] - cum_t[h:h + 1, :]
                    dec = jnp.exp(jnp.where(tril, seg, -jnp.inf))
                    wgt = cb * dec * dt_t[h:h + 1, :]
                    xm = jnp.where(low_half if e == 0 else jnp.logical_not(low_half), xp, 0.0)
                    acc = acc + _bdot(wgt, xm)
                yacc_s[rows, xl:xl + LANES] = acc
            ds = _bdot_tn(bm, xw_s[rows, gl:gl + SSD_GW])
            st_new = st_prev * cdec_s[pl.ds(n, 1), gl:gl + SSD_GW] + ds
            if chained:
                st_s[g] = st_new
            else:
                sout_ref[n, g] = st_new.T
        return carry

    for n in range(nc):
        chunk(n, 0)

    if chained:
        @pl.when(pl.program_id(0) == pl.num_programs(0) - 1)
        def _():
            for g in range(SSD_GROUPS):
                sout_ref[g] = st_s[g].T

    xs = jnp.concatenate([xbc_s[j] for j in range(SSD_DI // LANES)], axis=1)
    y = yacc_s[...] + de_ref[...] * xs
    yz = _rms(y * _silu(z_s[...]), snw_ref[...]).astype(BF16)
    y_ref[...] = x + jnp.dot(yz, wout_ref[...], preferred_element_type=F32)


def _const_spec(shape):
    nd = len(shape)
    return pl.BlockSpec(shape, lambda i, _nd=nd: (0,) * _nd, pipeline_mode=pl.Buffered(1))


def _row_specs(rows, T, skewed):
    n = rows // T
    if not skewed:
        spec = pl.BlockSpec((T, D_MODEL), lambda i: (i, 0))
        return (n,), spec, spec
    return ((n + 1,),
            pl.BlockSpec((T, D_MODEL), lambda i: (jnp.minimum(i, n - 1), 0)),
            pl.BlockSpec((T, D_MODEL), lambda i: (jnp.maximum(i - 1, 0), 0)))


def _params():
    return pltpu.CompilerParams(dimension_semantics=("arbitrary",), vmem_limit_bytes=VMEM_LIMIT_BYTES)


def _block_tri(T, c):
    r = np.arange(T)
    return jnp.asarray((r[:, None] // c == r[None, :] // c) & (r[:, None] >= r[None, :]), BF16)


def _eye():
    return jnp.asarray(np.eye(LANES), BF16)


def _gla_layer(x, s0, p, *, T, c, chained, final_norm_w=None):
    rows = x.shape[0]
    consts = [p["norm_w"], p["w_main"], p["w_glr"], p["w_gu"], p["b_gate"], p["gnorm_w"], p["w_out"],
              _block_tri(min(T, CUMSUM_ROWS), c)]
    args = [x] + consts
    grid, x_spec, y_spec = _row_specs(rows, T, chained)
    specs = [x_spec] + [_const_spec(a.shape) for a in consts]
    if not chained:
        args.append(s0)
        specs.append(_const_spec(s0.shape))
    if final_norm_w is not None:
        args.append(final_norm_w)
        specs.append(_const_spec(final_norm_w.shape))
    st_shape = (GLA_HEADS, GLA_DK, GLA_DV) if chained else s0.shape
    scratch = [pltpu.VMEM((T, 2 * GLA_QK + 2 * GLA_V), F32), pltpu.VMEM((T, GLA_QK), F32),
               pltpu.VMEM((T, GLA_V), F32)]
    if chained:
        scratch += [pltpu.VMEM((T, LANES), F32), pltpu.VMEM((T, D_MODEL), F32),
                    pltpu.VMEM((T, 2 * GLA_QK + 2 * GLA_V), F32)]
    body = functools.partial(_gla_body, T=T, c=c, chained=chained, final_norm=final_norm_w is not None)
    return pl.pallas_call(
        body,
        grid=grid,
        in_specs=specs,
        out_specs=[y_spec, pl.BlockSpec(st_shape, lambda i, _n=len(st_shape): (0,) * _n)],
        out_shape=[jax.ShapeDtypeStruct((rows, D_MODEL), F32), jax.ShapeDtypeStruct(st_shape, F32)],
        scratch_shapes=scratch,
        compiler_params=_params(),
        name="gla_chained" if chained else "gla_batched",
    )(*args)


def _lru_layer(x, cbuf, h0, p, *, T, c, chained):
    rows = x.shape[0]
    W = LRU_WIDTH
    consts = [p["norm_w"], p["w_in"], p["conv_w"], p["conv_b"], p["w_band"], p["b_a"], p["b_x"], p["lam"], p["w_out"]]
    args = [x] + consts
    grid, x_spec, y_spec = _row_specs(rows, T, False)
    specs = [x_spec] + [_const_spec(a.shape) for a in consts]
    if chained:
        cb_shape, h_shape = (CONV_W - 1, W), (1, W)
        pad_rows = SUBLANES + T
    else:
        args += [cbuf, h0]
        specs += [_const_spec(cbuf.shape), _const_spec(h0.shape)]
        cb_shape, h_shape = cbuf.shape, h0.shape
        pad_rows = (T // c) * (SUBLANES + c)
    scan_rows = SUBLANES * _scan_pitch(T // SUBLANES)
    body = functools.partial(_lru_body, T=T, c=c, chained=chained)
    zero_map = lambda n: (lambda i: (0,) * n)
    return pl.pallas_call(
        body,
        grid=grid,
        in_specs=specs,
        out_specs=[y_spec, pl.BlockSpec(cb_shape, zero_map(len(cb_shape))),
                   pl.BlockSpec(h_shape, zero_map(len(h_shape)))],
        out_shape=[jax.ShapeDtypeStruct((rows, D_MODEL), F32), jax.ShapeDtypeStruct(cb_shape, F32),
                   jax.ShapeDtypeStruct(h_shape, F32)],
        scratch_shapes=[pltpu.VMEM((LRU_TILES, pad_rows, LANES), F32), pltpu.VMEM((LRU_TILES, T, LANES), F32)]
                       + [pltpu.VMEM((LRU_TILES, scan_rows, LANES), F32)] * 3 + [pltpu.VMEM((SUBLANES, W), F32)],
        compiler_params=_params(),
        name="lru_chained" if chained else "lru_batched",
    )(*args)


def _ssd_layer(x, cbuf, s0, p, *, T, c, chained):
    rows = x.shape[0]
    consts = [p["norm_w"], p["w_z"], p["w_xbc"], p["w_dt"], p["conv_w"], p["conv_b"], p["dt_bias"], p["a_log"],
              p["d_exp"], p["snorm_w"], p["w_out"], _block_tri(T, c), _eye(), p["expand"]]
    args = [x] + consts
    grid, x_spec, y_spec = _row_specs(rows, T, False)
    specs = [x_spec] + [_const_spec(a.shape) for a in consts]
    if chained:
        cb_shape = (CONV_W - 1, SSD_CONV_DIM)
        st_shape = (SSD_GROUPS, SSD_GW, SSD_STATE)
        pad_rows = SUBLANES + T
    else:
        args += [cbuf, s0]
        specs += [_const_spec(cbuf.shape), _const_spec(s0.shape)]
        cb_shape, st_shape = cbuf.shape, s0.shape
        pad_rows = (T // c) * (SUBLANES + c)
    conv_tiles = SSD_CONV_DIM // LANES
    scratch = [pltpu.VMEM((conv_tiles, pad_rows, LANES), F32), pltpu.VMEM((conv_tiles, T, LANES), F32),
               pltpu.VMEM((T, SSD_DI), F32), pltpu.VMEM((T, LANES), F32), pltpu.VMEM((T, LANES), F32),
               pltpu.VMEM((T, SSD_DI), F32), pltpu.VMEM((T, SSD_DI), F32), pltpu.VMEM((T, SSD_DI), F32),
               pltpu.VMEM((SUBLANES, SSD_DI), F32)]
    if chained:
        scratch.append(pltpu.VMEM((SSD_GROUPS, SSD_STATE, SSD_GW), F32))
    body = functools.partial(_ssd_body, T=T, c=c, chained=chained)
    zero_map = lambda n: (lambda i: (0,) * n)
    return pl.pallas_call(
        body,
        grid=grid,
        in_specs=specs,
        out_specs=[y_spec, pl.BlockSpec(cb_shape, zero_map(len(cb_shape))),
                   pl.BlockSpec(st_shape, zero_map(len(st_shape)))],
        out_shape=[jax.ShapeDtypeStruct((rows, D_MODEL), F32), jax.ShapeDtypeStruct(cb_shape, F32),
                   jax.ShapeDtypeStruct(st_shape, F32)],
        scratch_shapes=scratch,
        compiler_params=_params(),
        name="ssd_chained" if chained else "ssd_batched",
    )(*args)


def _row(v):
    return v.reshape(1, -1).astype(F32)


def _pad_lanes(a, width=LANES):
    return jnp.pad(a, [(0, 0)] * (a.ndim - 1) + [(0, width - a.shape[-1])])


def _band_slabs(w):
    eye = jnp.eye(LRU_BLOCKS, dtype=w.dtype)
    dense = (w[:, :, None, :] * eye[:, None, :, None]).reshape(LRU_WIDTH, LRU_WIDTH)
    slabs = []
    for j in range(LRU_TILES):
        st = min(max(j - 1, 0), LRU_TILES - LRU_BAND)
        slabs.append(dense[st * LANES:(st + LRU_BAND) * LANES, j * LANES:(j + 1) * LANES])
    return jnp.stack(slabs)


def _gla_params(norm_w, w_in, w_gate_up, b_gate, gnorm_w, w_out):
    n_main = 2 * GLA_QK + 2 * GLA_V
    return {
        "norm_w": _row(norm_w),
        "w_main": w_in[:, :n_main].astype(BF16),
        "w_glr": _pad_lanes(w_in[:, n_main:]).astype(BF16),
        "w_gu": jnp.pad(w_gate_up, ((0, LANES - GLA_RANK), (0, 0))).astype(BF16),
        "b_gate": _row(b_gate),
        "gnorm_w": _row(gnorm_w),
        "w_out": w_out.astype(BF16),
    }


def _lru_params(norm_w, w_in, conv_w, conv_b, w_a, b_a, w_x, b_x, lam, w_out):
    return {
        "norm_w": _row(norm_w),
        "w_in": w_in.astype(BF16),
        "conv_w": conv_w.astype(F32),
        "conv_b": _row(conv_b),
        "w_band": jnp.concatenate([_band_slabs(w_a), _band_slabs(w_x)], axis=-1).astype(BF16),
        "b_a": _row(b_a),
        "b_x": _row(b_x),
        "lam": _row(lam),
        "w_out": w_out.astype(BF16),
    }


def _ssd_params(norm_w, w_in, conv_w, conv_b, dt_bias, a_log, d_skip, snorm_w, w_out):
    heads = np.arange(SSD_DI) // SSD_HEADDIM
    lane = np.arange(LANES)
    expand = jnp.asarray((lane[:, None] % SSD_HEADS == heads[None, :]) & (lane[:, None] < 3 * SSD_HEADS), BF16)
    reps = LANES // SSD_HEADS
    return {
        "norm_w": _row(norm_w),
        "w_z": w_in[:, :SSD_DI].astype(BF16),
        "w_xbc": w_in[:, SSD_DI:SSD_DI + SSD_CONV_DIM].astype(BF16),
        "w_dt": jnp.tile(w_in[:, SSD_DI + SSD_CONV_DIM:], (1, reps)).astype(BF16),
        "conv_w": conv_w.astype(F32),
        "conv_b": _row(conv_b),
        "dt_bias": jnp.tile(_row(dt_bias), (1, reps)),
        "a_log": jnp.tile(_row(a_log), (1, reps)),
        "d_exp": _row(jnp.repeat(d_skip, SSD_HEADDIM)),
        "snorm_w": _row(snorm_w),
        "w_out": w_out.astype(BF16),
        "expand": expand,
    }


def _trunk(x, st_gla, st_lru_conv, st_lru_h, st_ssd_conv, st_ssd_h, layers, final_norm_w, *, tiles, c, chained):
    new = {"gla": [], "lru_conv": [], "lru_h": [], "ssd_conv": [], "ssd_h": []}
    n_layers = len(layers)
    for i, (kind, j, p) in enumerate(layers):
        T = tiles[kind]
        if kind == "gla":
            fnw = final_norm_w if i == n_layers - 1 else None
            x, s = _gla_layer(x, None if chained else st_gla[j], p, T=T, c=c, chained=chained, final_norm_w=fnw)
            new["gla"].append(s)
        elif kind == "lru":
            x, cb, h = _lru_layer(x, None if chained else st_lru_conv[j], None if chained else st_lru_h[j], p,
                                  T=T, c=c, chained=chained)
            new["lru_conv"].append(cb)
            new["lru_h"].append(h)
        else:
            s0 = None if chained else st_ssd_h[j].reshape(-1, SSD_GROUPS, SSD_GW, SSD_STATE)
            x, cb, s = _ssd_layer(x, None if chained else st_ssd_conv[j], s0, p, T=T, c=c, chained=chained)
            new["ssd_conv"].append(cb)
            new["ssd_h"].append(s)
    return x, new


def kernel(x_prompt, x_sample, state_gla, state_lru_conv, state_lru_h, state_ssd_conv, state_ssd_h, norm_w, final_norm_w, gla_w_in, gla_w_gate_up, gla_b_gate, gla_norm_w, gla_w_out, lru_w_in, lru_conv_w, lru_conv_b, lru_w_a, lru_b_a, lru_w_x, lru_b_x, lru_lambda, lru_w_out, ssd_w_in, ssd_conv_w, ssd_conv_b, ssd_dt_bias, ssd_a_log, ssd_d, ssd_norm_w, ssd_w_out):
    depth = norm_w.shape[0]
    layers = []
    for i in range(depth):
        j = i // 3
        if i % 3 == 0:
            layers.append(("gla", j, _gla_params(norm_w[i], gla_w_in[j], gla_w_gate_up[j], gla_b_gate[j],
                                                 gla_norm_w[j], gla_w_out[j])))
        elif i % 3 == 1:
            layers.append(("lru", j, _lru_params(norm_w[i], lru_w_in[j], lru_conv_w[j], lru_conv_b[j], lru_w_a[j],
                                                 lru_b_a[j], lru_w_x[j], lru_b_x[j], lru_lambda[j], lru_w_out[j])))
        else:
            layers.append(("ssd", j, _ssd_params(norm_w[i], ssd_w_in[j], ssd_conv_w[j], ssd_conv_b[j],
                                                 ssd_dt_bias[j], ssd_a_log[j], ssd_d[j], ssd_norm_w[j],
                                                 ssd_w_out[j])))
    assert layers[-1][0] == "gla", "the final RMSNorm is fused into a GLA layer"
    fnw = _row(final_norm_w)

    bp, lp, _ = x_prompt.shape
    bs, ls, _ = x_sample.shape
    p_tiles = {k: min(t, lp) for k, t in PROMPT_TILE.items()}
    assert bp == 1 and all(lp % t == 0 for t in p_tiles.values()) and bs * ls == LANES and bs == SUBLANES

    yp, new_p = _trunk(x_prompt.reshape(lp, D_MODEL), None, None, None, None, None, layers, fnw,
                       tiles=p_tiles, c=min(CHUNK, lp), chained=True)
    ys, new_s = _trunk(x_sample.reshape(bs * ls, D_MODEL), state_gla, state_lru_conv, state_lru_h,
                       state_ssd_conv, state_ssd_h, layers, fnw, tiles=dict.fromkeys(PROMPT_TILE, bs * ls),
                       c=min(CHUNK, ls), chained=False)

    def stack(xs, shape):
        return jnp.stack(xs).reshape(shape)

    n_gla, n_lru, n_ssd = len(new_p["gla"]), len(new_p["lru_h"]), len(new_p["ssd_h"])
    return (
        yp.reshape(x_prompt.shape), ys.reshape(x_sample.shape),
        stack(new_p["gla"], (n_gla, 1, GLA_HEADS, GLA_DK, GLA_DV)),
        stack(new_p["lru_conv"], (n_lru, 1, CONV_W - 1, LRU_WIDTH)),
        stack(new_p["lru_h"], (n_lru, 1, LRU_WIDTH)),
        stack(new_p["ssd_conv"], (n_ssd, 1, CONV_W - 1, SSD_CONV_DIM)),
        stack(new_p["ssd_h"], (n_ssd, 1, SSD_HEADS, SSD_HEADDIM, SSD_STATE)),
        stack(new_s["gla"], (n_gla, bs, GLA_HEADS, GLA_DK, GLA_DV)),
        stack(new_s["lru_conv"], (n_lru, bs, CONV_W - 1, LRU_WIDTH)),
        stack(new_s["lru_h"], (n_lru, bs, LRU_WIDTH)),
        stack(new_s["ssd_conv"], (n_ssd, bs, CONV_W - 1, SSD_CONV_DIM)),
        stack(new_s["ssd_h"], (n_ssd, bs, SSD_HEADS, SSD_HEADDIM, SSD_STATE)),
    )
```

```python
import functools

import numpy as np
import jax
import jax.numpy as jnp
from jax import lax
from jax.experimental import pallas as pl
from jax.experimental.pallas import tpu as pltpu

F32 = jnp.float32
BF16 = jnp.bfloat16

D_MODEL = 1024
CHUNK = 64
EPS = 1e-6
CONV_W = 4

GLA_HEADS = 4
GLA_QK = D_MODEL // 2
GLA_V = D_MODEL
GLA_DK = GLA_QK // GLA_HEADS
GLA_DV = GLA_V // GLA_HEADS
GLA_RANK = 16
GLA_TAU = 16.0

LRU_WIDTH = 1408
LRU_BLOCKS = 16
LRU_BW = LRU_WIDTH // LRU_BLOCKS
LRU_C = 8.0

SSD_DI = 2 * D_MODEL
SSD_HEADDIM = 64
SSD_HEADS = SSD_DI // SSD_HEADDIM
SSD_STATE = 128
SSD_GROUPS = 4
SSD_HPG = SSD_HEADS // SSD_GROUPS
SSD_CONV_DIM = SSD_DI + 2 * SSD_GROUPS * SSD_STATE
SSD_GW = SSD_HPG * SSD_HEADDIM

LANES = 128
SUBLANES = 8
PROMPT_TILE = {"gla": 512, "lru": 512, "ssd": 256}
PROJ_SLICE = 256
CUMSUM_ROWS = 256
VMEM_LIMIT_BYTES = 56 * 1024 * 1024
LRU_TILES = LRU_WIDTH // LANES
LRU_BAND = 3


def _bdot(a, b):
    return jnp.dot(a.astype(BF16), b.astype(BF16), preferred_element_type=F32)


def _bdot_nt(a, b):
    return lax.dot_general(a.astype(BF16), b.astype(BF16), (((1,), (1,)), ((), ())),
                           preferred_element_type=F32)


def _bdot_tn(a, b):
    return lax.dot_general(a.astype(BF16), b.astype(BF16), (((0,), (0,)), ((), ())),
                           preferred_element_type=F32)


def _split3(x):
    hi = x.astype(BF16)
    r1 = x - hi.astype(F32)
    mid = r1.astype(BF16)
    lo = (r1 - mid.astype(F32)).astype(BF16)
    return hi, mid, lo


def _sel_dot(sel, x):
    hi, mid, lo = _split3(x)
    d = lambda p: jnp.dot(sel, p, preferred_element_type=F32)
    return d(hi) + d(mid) + d(lo)


def _expand_heads(x, expand):
    hi = x.astype(BF16).astype(F32)
    r1 = x - hi
    mid = r1.astype(BF16).astype(F32)
    lane = lax.broadcasted_iota(jnp.int32, x.shape, 1)
    packed = jnp.where(lane < SSD_HEADS, hi, jnp.where(lane < 2 * SSD_HEADS, mid, r1 - mid))
    return jnp.dot(packed.astype(BF16), expand, preferred_element_type=F32)


def _sel_dot_nt(sel, x):
    hi, mid, lo = _split3(x)
    d = lambda p: lax.dot_general(sel, p, (((1,), (1,)), ((), ())), preferred_element_type=F32)
    return d(hi) + d(mid) + d(lo)


def _rms(x, w):
    return x * lax.rsqrt(jnp.mean(x * x, axis=-1, keepdims=True) + EPS) * w


def _softplus(x):
    return jnp.maximum(x, 0.0) + jnp.log1p(jnp.exp(-jnp.abs(x)))


def _silu(x):
    return x * jax.nn.sigmoid(x)


class _Interleave:
    def __init__(self, pieces, slots):
        self._pieces, self._slots, self._calls, self._total = list(pieces), slots, 0, None

    def take(self, k):
        for _ in range(min(k, len(self._pieces))):
            self._pieces.pop(0)()

    def step(self):
        if self._total is None:
            self._total = len(self._pieces)
        self._calls += 1
        left_after = self._total - self._calls * self._total // self._slots
        self.take(len(self._pieces) - max(left_after, 0))

    def flush(self):
        self.take(len(self._pieces))


def _proj_slice(dst_ref, hb, w_ref, lo, off=0):
    hi = min(lo + PROJ_SLICE, w_ref.shape[1])
    dst_ref[:, off + lo:off + hi] = jnp.dot(hb, w_ref[:, lo:hi], preferred_element_type=F32)


def _scan_pitch(seg):
    return seg if (seg // SUBLANES) % 2 == 1 else seg + SUBLANES


_HDR = SUBLANES
_PAD0 = _HDR - (CONV_W - 1)


def _conv_seqs(T, c, chained):
    return [(0, 0, T)] if chained else [(b * (_HDR + c), b * c, c) for b in range(T // c)]


def _conv_fill(xpad_s, xb, cbuf_in, *, T, c, chained):
    nt = xpad_s.shape[0]
    for b, (base, r0, n) in enumerate(_conv_seqs(T, c, chained)):
        prev = None if chained else cbuf_in[b]
        for j in range(nt):
            lanes = slice(j * LANES, (j + 1) * LANES)
            if prev is not None:
                xpad_s[j, base + _PAD0:base + _HDR, :] = prev[:, lanes]
            xpad_s[j, base + _HDR:base + _HDR + n, :] = xb[r0:r0 + n, lanes]


def _conv_apply(xpad_s, xc_s, cw_ref, cb_ref, cbuf_out, *, T, c, chained, act=None, after_tile=None):
    nt = xpad_s.shape[0]
    blk = 2 * SUBLANES
    seqs = _conv_seqs(T, c, chained)
    for j in range(nt):
        lanes = slice(j * LANES, (j + 1) * LANES)
        w = [jnp.broadcast_to(cw_ref[k:k + 1, lanes], (SUBLANES, LANES)) for k in range(CONV_W)]
        bias = jnp.broadcast_to(cb_ref[:, lanes], (SUBLANES, LANES))
        for b, (base, r0, n) in enumerate(seqs):
            for i in range(n // blk):
                p0 = base + _PAD0 + i * blk
                win = [xpad_s[j, pl.ds(p0 + m, SUBLANES, stride=2), :] for m in range(CONV_W + 1)]
                for par in range(2):
                    y = bias
                    for k in range(CONV_W):
                        y = y + w[k] * win[par + k]
                    xc_s[j, pl.ds(r0 + i * blk + par, SUBLANES, stride=2), :] = y if act is None else act(y)
            tail = xpad_s[j, base + _HDR + n - (CONV_W - 1):base + _HDR + n, :]
            if chained:
                cbuf_out[:, lanes] = tail
                xpad_s[j, _PAD0:_HDR, :] = tail
            else:
                cbuf_out[b, :, lanes] = tail
        if after_tile is not None:
            after_tile()


def _gla_body(*refs, T, c, chained, final_norm):
    refs = list(refs)
    (x_ref, nw_ref, wm_ref, wglr_ref, wgu_ref, bg_ref, gnw_ref, wout_ref, tri_ref) = refs[:9]
    refs = refs[9:]
    s0_ref = None if chained else refs.pop(0)
    fnw_ref = refs.pop(0) if final_norm else None
    if chained:
        y_ref, sout_ref, proj_s, bcum_s, o_s, glr_s, xs_s, pin_s = refs
    else:
        y_ref, sout_ref, proj_s, bcum_s, o_s = refs
    nc = T // c

    if chained:
        @pl.when(pl.program_id(0) <= 1)
        def _():
            sout_ref[...] = jnp.zeros(sout_ref.shape, F32)

        @pl.when(pl.program_id(0) == 0)
        def _():
            proj_s[...] = jnp.zeros(proj_s.shape, F32)
            glr_s[...] = jnp.zeros(glr_s.shape, F32)
            xs_s[...] = jnp.zeros(xs_s.shape, F32)

    x_in = x_ref[...]
    hb = _rms(x_in, nw_ref[...]).astype(BF16)
    glr_in = jnp.dot(hb, wglr_ref[...], preferred_element_type=F32)
    if chained:
        x = xs_s[...]
        glr = glr_s[...]
        front = _Interleave([functools.partial(_proj_slice, pin_s, hb, wm_ref, lo)
                             for lo in range(0, wm_ref.shape[1], PROJ_SLICE)], nc * GLA_HEADS)
    else:
        x, glr = x_in, glr_in
        proj_s[...] = jnp.dot(hb, wm_ref[...], preferred_element_type=F32)
        front = _Interleave([], nc * GLA_HEADS)

    zg = _bdot(glr, wgu_ref[...]) + bg_ref[...]
    front.take(2)
    log_a = -_softplus(-zg) / GLA_TAU
    tb = tri_ref.shape[0]
    for b in range(T // tb):
        bcum_s[b * tb:(b + 1) * tb, :] = _sel_dot(tri_ref[...], log_a[b * tb:(b + 1) * tb, :])

    tril = lax.broadcasted_iota(jnp.int32, (c, c), 0) >= lax.broadcasted_iota(jnp.int32, (c, c), 1)
    k_off, v_off, g_off = GLA_QK, 2 * GLA_QK, 2 * GLA_QK + GLA_V

    dec_cols = []
    for hd in range(GLA_HEADS):
        kc = hd * GLA_DK
        last_rows = [bcum_s[n * c + c - 1:n * c + c, kc:kc + GLA_DK] for n in range(nc)]
        blk = jnp.concatenate(last_rows + [jnp.zeros((GLA_DK - nc, GLA_DK), F32)], axis=0)
        dec_cols.append(jnp.exp(blk.T))

    def chunk(n, carry):
        rows = pl.ds(n * c, c)
        for hd in range(GLA_HEADS):
            kc = hd * GLA_DK
            vc = hd * GLA_DV
            bc = bcum_s[rows, kc:kc + GLA_DK]
            bl = bc[c - 1:c, :]
            q = proj_s[rows, kc:kc + GLA_DK]
            k = proj_s[rows, k_off + kc:k_off + kc + GLA_DK]
            v = proj_s[rows, v_off + vc:v_off + vc + GLA_DV]
            g = proj_s[rows, g_off + vc:g_off + vc + GLA_DV]
            qd = q * (GLA_DK ** -0.5) * jnp.exp(bc)
            kd = k * jnp.exp(-bc)
            ke = k * jnp.exp(bl - bc)
            att = jnp.where(tril, _bdot_nt(qd, kd), 0.0)
            s_prev = sout_ref[hd] if chained else s0_ref[n, hd]
            o = _bdot(att, v) + _bdot(qd, s_prev)
            ds = _bdot_tn(ke, v)
            s_new = dec_cols[hd][:, n:n + 1] * s_prev + ds
            if chained:
                sout_ref[hd] = s_new
            else:
                sout_ref[n, hd] = s_new
            o = o * lax.rsqrt(jnp.mean(o * o, axis=-1, keepdims=True) + EPS) * gnw_ref[...]
            o_s[rows, vc:vc + GLA_DV] = o * _silu(g)
            front.step()
        return carry

    for n in range(nc):
        chunk(n, 0)
    front.flush()
    out = x + jnp.dot(o_s[...].astype(BF16), wout_ref[...], preferred_element_type=F32)
    if final_norm:
        out = _rms(out, fnw_ref[...])
    y_ref[...] = out
    if chained:
        proj_s[...] = pin_s[...]
        glr_s[...] = glr_in
        xs_s[...] = x_in


def _lru_body(*refs, T, c, chained):
    refs = list(refs)
    (x_ref, nw_ref, win_ref, cw_ref, cb_ref, wband_ref, ba_ref, bx_ref, lam_ref, wout_ref) = refs[:10]
    refs = refs[10:]
    if chained:
        cbuf_in = h0_ref = None
    else:
        cbuf_in = refs.pop(0)
        h0_ref = refs.pop(0)
    if chained:
        y_ref, cbuf_out, h_out, xpad_s, xc_s, a_s, u_s, hs_s, hin_s, gate_s, xs_s, pin_s = refs
    else:
        y_ref, cbuf_out, h_out, xpad_s, xc_s, a_s, u_s, hs_s, hin_s = refs
    W = LRU_WIDTH
    seg = T // SUBLANES

    if chained:
        @pl.when(pl.program_id(0) <= 1)
        def _():
            h_out[...] = jnp.zeros(h_out.shape, F32)

        @pl.when(pl.program_id(0) == 0)
        def _():
            xpad_s[...] = jnp.zeros(xpad_s.shape, F32)
            gate_s[...] = jnp.zeros(gate_s.shape, F32)
            xs_s[...] = jnp.zeros(xs_s.shape, F32)

    x_in = x_ref[...]
    hb = _rms(x_in, nw_ref[...]).astype(BF16)
    if chained:
        x = xs_s[...]
        gate = gate_s[...]
        front = _Interleave([functools.partial(_proj_slice, pin_s, hb, win_ref, lo)
                             for lo in range(0, win_ref.shape[1], PROJ_SLICE)], 2 * LRU_TILES)
    else:
        x = x_in
        proj = jnp.dot(hb, win_ref[...], preferred_element_type=F32)
        gate = proj[:, W:]
        _conv_fill(xpad_s, proj, cbuf_in, T=T, c=c, chained=False)
        front = _Interleave([], 2 * LRU_TILES)
    _conv_apply(xpad_s, xc_s, cw_ref, cb_ref, cbuf_out, T=T, c=c, chained=chained, after_tile=front.step)

    pitch = _scan_pitch(seg)
    neg_c_sp = -LRU_C * _softplus(-lam_ref[...])
    for j in range(LRU_TILES):
        st = min(max(j - 1, 0), LRU_TILES - LRU_BAND)
        lo = j * LANES
        xin = jnp.concatenate([xc_s[st + m] for m in range(LRU_BAND)], axis=1)
        pre = _bdot(xin, wband_ref[j])
        r = jax.nn.sigmoid(pre[:, :LANES] + ba_ref[:, lo:lo + LANES])
        i = jax.nn.sigmoid(pre[:, LANES:] + bx_ref[:, lo:lo + LANES])
        log_a = r * neg_c_sp[:, lo:lo + LANES]
        th = jnp.tanh(log_a)
        one_minus_a2 = -2.0 * th / (1.0 - th)
        a = jnp.exp(log_a)
        u = jnp.sqrt(one_minus_a2) * (i * xc_s[j])
        for s in range(SUBLANES):
            a_s[j, s * pitch:s * pitch + seg, :] = a[s * seg:(s + 1) * seg, :]
            u_s[j, s * pitch:s * pitch + seg, :] = u[s * seg:(s + 1) * seg, :]
        front.step()

    def seg_rows(t):
        return pl.ds(t, SUBLANES, stride=pitch)

    tiles = range(LRU_TILES)
    if chained:
        def local_scan(t, carry):
            hs, ps = carry
            a = [a_s[j, seg_rows(t), :] for j in tiles]
            return (tuple(a[j] * hs[j] + u_s[j, seg_rows(t), :] for j in tiles),
                    tuple(ps[j] * a[j] for j in tiles))

        h_end, p_end = lax.fori_loop(
            0, seg, local_scan,
            (tuple(jnp.zeros((SUBLANES, LANES), F32) for _ in tiles),
             tuple(jnp.ones((SUBLANES, LANES), F32) for _ in tiles)))
        h_end = jnp.concatenate(h_end, axis=1)
        p_end = jnp.concatenate(p_end, axis=1)
        hc = h_out[...]
        for s in range(SUBLANES):
            hin_s[s:s + 1, :] = hc
            hc = h_end[s:s + 1, :] + p_end[s:s + 1, :] * hc
        h_out[...] = hc
        h_init = hin_s[...]
    else:
        h_init = h0_ref[...]

    def full_scan(t, hs):
        new = []
        for j in tiles:
            h = a_s[j, seg_rows(t), :] * hs[j] + u_s[j, seg_rows(t), :]
            hs_s[j, seg_rows(t), :] = h
            new.append(h)
        return tuple(new)

    h_fin = lax.fori_loop(0, seg, full_scan, tuple(h_init[:, j * LANES:(j + 1) * LANES] for j in tiles))
    if not chained:
        h_out[...] = jnp.concatenate(h_fin, axis=1)

    gated = []
    for j in tiles:
        hs = jnp.concatenate([hs_s[j, s * pitch:s * pitch + seg, :] for s in range(SUBLANES)], axis=0)
        gated.append((hs * _silu(gate[:, j * LANES:(j + 1) * LANES])).astype(BF16))
        front.step()
    front.flush()
    y_ref[...] = x + jnp.dot(jnp.concatenate(gated, axis=1), wout_ref[...], preferred_element_type=F32)
    if chained:
        _conv_fill(xpad_s, pin_s, None, T=T, c=c, chained=True)
        gate_s[...] = pin_s[:, W:]
        xs_s[...] = x_in


def _ssd_body(*refs, T, c, chained):
    refs = list(refs)
    (x_ref, nw_ref, wz_ref, wxbc_ref, wdt_ref, cw_ref, cb_ref, dtb_ref, alog_ref, de_ref, snw_ref,
     wout_ref, tri_ref, eye_ref, exp_ref) = refs[:15]
    refs = refs[15:]
    if chained:
        cbuf_in = s0_ref = None
    else:
        cbuf_in = refs.pop(0)
        s0_ref = refs.pop(0)
    if chained:
        (y_ref, cbuf_out, sout_ref, xpad_s, xbc_s, z_s, cum_s, dt_s, ecum_s, xw_s, yacc_s, cdec_s, st_s,
         dtp_s, xs_s, pin_s) = refs
    else:
        (y_ref, cbuf_out, sout_ref, xpad_s, xbc_s, z_s, cum_s, dt_s, ecum_s, xw_s, yacc_s, cdec_s) = refs
        st_s = None
    nc = T // c
    b_off = SSD_DI
    c_off = SSD_DI + SSD_GROUPS * SSD_STATE
    n_slots = SSD_CONV_DIM // LANES + nc * SSD_GROUPS * (SSD_HPG // 2)

    if chained:
        @pl.when(pl.program_id(0) <= 1)
        def _():
            st_s[...] = jnp.zeros(st_s.shape, F32)

        @pl.when(pl.program_id(0) == 0)
        def _():
            xpad_s[...] = jnp.zeros(xpad_s.shape, F32)
            z_s[...] = jnp.zeros(z_s.shape, F32)
            dtp_s[...] = jnp.zeros(dtp_s.shape, F32)
            xs_s[...] = jnp.zeros(xs_s.shape, F32)

    x_in = x_ref[...]
    hb = _rms(x_in, nw_ref[...]).astype(BF16)
    dtp_in = jnp.dot(hb, wdt_ref[...], preferred_element_type=F32)
    if chained:
        x = xs_s[...]
        dt_pre = dtp_s[...]
        pieces = [functools.partial(_proj_slice, pin_s, hb, wz_ref, lo) for lo in range(0, SSD_DI, PROJ_SLICE)]
        pieces += [functools.partial(_proj_slice, pin_s, hb, wxbc_ref, lo, SSD_DI)
                   for lo in range(0, SSD_CONV_DIM, PROJ_SLICE)]
        front = _Interleave(pieces, n_slots)
    else:
        x, dt_pre = x_in, dtp_in
        z_s[...] = jnp.dot(hb, wz_ref[...], preferred_element_type=F32)
        _conv_fill(xpad_s, jnp.dot(hb, wxbc_ref[...], preferred_element_type=F32), cbuf_in, T=T, c=c, chained=False)
        front = _Interleave([], n_slots)
    _conv_apply(xpad_s, xbc_s, cw_ref, cb_ref, cbuf_out, T=T, c=c, chained=chained, act=_silu,
                after_tile=front.step)

    dt = _softplus(dt_pre + dtb_ref[...])
    a_neg = -jnp.exp(alog_ref[...])
    cum = _sel_dot(tri_ref[...], dt * a_neg)
    cum_s[...] = cum
    dt_s[...] = dt
    lasts = [cum[n * c + c - 1:n * c + c, :] for n in range(nc)]
    cl = jnp.concatenate([jnp.broadcast_to(l, (c, LANES)) for l in lasts], axis=0)
    ecum_s[...] = _expand_heads(jnp.exp(cum), exp_ref[...])
    wb_e = _expand_heads(jnp.exp(cl - cum) * dt, exp_ref[...])
    for j in range(SSD_DI // LANES):
        xw_s[:, j * LANES:(j + 1) * LANES] = xbc_s[j] * wb_e[:, j * LANES:(j + 1) * LANES]
    cl_rows = jnp.concatenate(lasts + [jnp.zeros((SUBLANES - nc, LANES), F32)] * (nc < SUBLANES), axis=0)
    cdec_s[...] = _expand_heads(jnp.exp(cl_rows), exp_ref[...])

    tril = lax.broadcasted_iota(jnp.int32, (c, c), 0) >= lax.broadcasted_iota(jnp.int32, (c, c), 1)
    low_half = lax.broadcasted_iota(jnp.int32, (c, LANES), 1) < SSD_HEADDIM

    def chunk(n, carry):
        rows = pl.ds(n * c, c)
        cum_c = cum_s[rows, :]
        cum_t = _sel_dot_nt(eye_ref[...], cum_c)
        dt_t = _sel_dot_nt(eye_ref[...], dt_s[rows, :])
        for g in range(SSD_GROUPS):
            gl = g * SSD_GW
            bm = xbc_s[b_off // LANES + g, rows, :]
            cm = xbc_s[c_off // LANES + g, rows, :]
            cb = _bdot_nt(cm, bm)
            if chained:
                st_prev = st_s[g]
            else:
                st_prev = s0_ref[n, g].T
            y_inter = _bdot(cm, st_prev) * ecum_s[rows, gl:gl + SSD_GW]
            for pr in range(SSD_HPG // 2):
                h0 = g * SSD_HPG + 2 * pr
                xl = h0 * SSD_HEADDIM
                xp = xbc_s[h0 // 2, rows, :]
                acc = y_inter[:, pr * LANES:(pr + 1) * LANES]
                for e in range(2):
                    h = h0 + e
                    seg = cum_c[:, h:h + 1] - cum_t[h:h + 1, :]
                    dec = jnp.exp(jnp.where(tril, seg, -jnp.inf))
                    wgt = cb * dec * dt_t[h:h + 1, :]
                    xm = jnp.where(low_half if e == 0 else jnp.logical_not(low_half), xp, 0.0)
                    acc = acc + _bdot(wgt, xm)
                yacc_s[rows, xl:xl + LANES] = acc
                front.step()
            ds = _bdot_tn(bm, xw_s[rows, gl:gl + SSD_GW])
            st_new = st_prev * cdec_s[pl.ds(n, 1), gl:gl + SSD_GW] + ds
            if chained:
                st_s[g] = st_new
            else:
                sout_ref[n, g] = st_new.T
        return carry

    for n in range(nc):
        chunk(n, 0)
    front.flush()

    xs = jnp.concatenate([xbc_s[j] for j in range(SSD_DI // LANES)], axis=1)
    y = yacc_s[...] + de_ref[...] * xs
    yz = _rms(y * _silu(z_s[...]), snw_ref[...]).astype(BF16)
    y_ref[...] = x + jnp.dot(yz, wout_ref[...], preferred_element_type=F32)

    if chained:
        z_s[...] = pin_s[:, 0:SSD_DI]
        _conv_fill(xpad_s, pin_s.at[:, SSD_DI:], None, T=T, c=c, chained=True)
        dtp_s[...] = dtp_in
        xs_s[...] = x_in

        @pl.when(pl.program_id(0) == pl.num_programs(0) - 1)
        def _():
            for g in range(SSD_GROUPS):
                sout_ref[g] = st_s[g].T


def _const_spec(shape):
    nd = len(shape)
    return pl.BlockSpec(shape, lambda i, _nd=nd: (0,) * _nd, pipeline_mode=pl.Buffered(1))


def _row_specs(rows, T, skewed):
    n = rows // T
    if not skewed:
        spec = pl.BlockSpec((T, D_MODEL), lambda i: (i, 0))
        return (n,), spec, spec
    return ((n + 1,),
            pl.BlockSpec((T, D_MODEL), lambda i: (jnp.minimum(i, n - 1), 0)),
            pl.BlockSpec((T, D_MODEL), lambda i: (jnp.maximum(i - 1, 0), 0)))


def _params():
    return pltpu.CompilerParams(dimension_semantics=("arbitrary",), vmem_limit_bytes=VMEM_LIMIT_BYTES)


def _block_tri(T, c):
    r = np.arange(T)
    return jnp.asarray((r[:, None] // c == r[None, :] // c) & (r[:, None] >= r[None, :]), BF16)


def _eye():
    return jnp.asarray(np.eye(LANES), BF16)


def _gla_layer(x, s0, p, *, T, c, chained, final_norm_w=None):
    rows = x.shape[0]
    consts = [p["norm_w"], p["w_main"], p["w_glr"], p["w_gu"], p["b_gate"], p["gnorm_w"], p["w_out"],
              _block_tri(min(T, CUMSUM_ROWS), c)]
    args = [x] + consts
    grid, x_spec, y_spec = _row_specs(rows, T, chained)
    specs = [x_spec] + [_const_spec(a.shape) for a in consts]
    if not chained:
        args.append(s0)
        specs.append(_const_spec(s0.shape))
    if final_norm_w is not None:
        args.append(final_norm_w)
        specs.append(_const_spec(final_norm_w.shape))
    st_shape = (GLA_HEADS, GLA_DK, GLA_DV) if chained else s0.shape
    scratch = [pltpu.VMEM((T, 2 * GLA_QK + 2 * GLA_V), F32), pltpu.VMEM((T, GLA_QK), F32),
               pltpu.VMEM((T, GLA_V), F32)]
    if chained:
        scratch += [pltpu.VMEM((T, LANES), F32), pltpu.VMEM((T, D_MODEL), F32),
                    pltpu.VMEM((T, 2 * GLA_QK + 2 * GLA_V), F32)]
    body = functools.partial(_gla_body, T=T, c=c, chained=chained, final_norm=final_norm_w is not None)
    return pl.pallas_call(
        body,
        grid=grid,
        in_specs=specs,
        out_specs=[y_spec, pl.BlockSpec(st_shape, lambda i, _n=len(st_shape): (0,) * _n)],
        out_shape=[jax.ShapeDtypeStruct((rows, D_MODEL), F32), jax.ShapeDtypeStruct(st_shape, F32)],
        scratch_shapes=scratch,
        compiler_params=_params(),
        name="gla_chained" if chained else "gla_batched",
    )(*args)


def _lru_layer(x, cbuf, h0, p, *, T, c, chained):
    rows = x.shape[0]
    W = LRU_WIDTH
    consts = [p["norm_w"], p["w_in"], p["conv_w"], p["conv_b"], p["w_band"], p["b_a"], p["b_x"], p["lam"], p["w_out"]]
    args = [x] + consts
    grid, x_spec, y_spec = _row_specs(rows, T, chained)
    specs = [x_spec] + [_const_spec(a.shape) for a in consts]
    if chained:
        cb_shape, h_shape = (CONV_W - 1, W), (1, W)
        pad_rows = SUBLANES + T
    else:
        args += [cbuf, h0]
        specs += [_const_spec(cbuf.shape), _const_spec(h0.shape)]
        cb_shape, h_shape = cbuf.shape, h0.shape
        pad_rows = (T // c) * (SUBLANES + c)
    scan_rows = SUBLANES * _scan_pitch(T // SUBLANES)
    body = functools.partial(_lru_body, T=T, c=c, chained=chained)
    zero_map = lambda n: (lambda i: (0,) * n)
    return pl.pallas_call(
        body,
        grid=grid,
        in_specs=specs,
        out_specs=[y_spec, pl.BlockSpec(cb_shape, zero_map(len(cb_shape))),
                   pl.BlockSpec(h_shape, zero_map(len(h_shape)))],
        out_shape=[jax.ShapeDtypeStruct((rows, D_MODEL), F32), jax.ShapeDtypeStruct(cb_shape, F32),
                   jax.ShapeDtypeStruct(h_shape, F32)],
        scratch_shapes=[pltpu.VMEM((LRU_TILES, pad_rows, LANES), F32), pltpu.VMEM((LRU_TILES, T, LANES), F32)]
                       + [pltpu.VMEM((LRU_TILES, scan_rows, LANES), F32)] * 3 + [pltpu.VMEM((SUBLANES, W), F32)]
                       + [pltpu.VMEM((T, W), F32), pltpu.VMEM((T, D_MODEL), F32),
                          pltpu.VMEM((T, 2 * W), F32)] * chained,
        compiler_params=_params(),
        name="lru_chained" if chained else "lru_batched",
    )(*args)


def _ssd_layer(x, cbuf, s0, p, *, T, c, chained):
    rows = x.shape[0]
    consts = [p["norm_w"], p["w_z"], p["w_xbc"], p["w_dt"], p["conv_w"], p["conv_b"], p["dt_bias"], p["a_log"],
              p["d_exp"], p["snorm_w"], p["w_out"], _block_tri(T, c), _eye(), p["expand"]]
    args = [x] + consts
    grid, x_spec, y_spec = _row_specs(rows, T, chained)
    specs = [x_spec] + [_const_spec(a.shape) for a in consts]
    if chained:
        cb_shape = (CONV_W - 1, SSD_CONV_DIM)
        st_shape = (SSD_GROUPS, SSD_GW, SSD_STATE)
        pad_rows = SUBLANES + T
    else:
        args += [cbuf, s0]
        specs += [_const_spec(cbuf.shape), _const_spec(s0.shape)]
        cb_shape, st_shape = cbuf.shape, s0.shape
        pad_rows = (T // c) * (SUBLANES + c)
    conv_tiles = SSD_CONV_DIM // LANES
    scratch = [pltpu.VMEM((conv_tiles, pad_rows, LANES), F32), pltpu.VMEM((conv_tiles, T, LANES), F32),
               pltpu.VMEM((T, SSD_DI), F32), pltpu.VMEM((T, LANES), F32), pltpu.VMEM((T, LANES), F32),
               pltpu.VMEM((T, SSD_DI), F32), pltpu.VMEM((T, SSD_DI), F32), pltpu.VMEM((T, SSD_DI), F32),
               pltpu.VMEM((SUBLANES, SSD_DI), F32)]
    if chained:
        scratch += [pltpu.VMEM((SSD_GROUPS, SSD_STATE, SSD_GW), F32), pltpu.VMEM((T, LANES), F32),
                    pltpu.VMEM((T, D_MODEL), F32), pltpu.VMEM((T, SSD_DI + SSD_CONV_DIM), F32)]
    body = functools.partial(_ssd_body, T=T, c=c, chained=chained)
    zero_map = lambda n: (lambda i: (0,) * n)
    return pl.pallas_call(
        body,
        grid=grid,
        in_specs=specs,
        out_specs=[y_spec, pl.BlockSpec(cb_shape, zero_map(len(cb_shape))),
                   pl.BlockSpec(st_shape, zero_map(len(st_shape)))],
        out_shape=[jax.ShapeDtypeStruct((rows, D_MODEL), F32), jax.ShapeDtypeStruct(cb_shape, F32),
                   jax.ShapeDtypeStruct(st_shape, F32)],
        scratch_shapes=scratch,
        compiler_params=_params(),
        name="ssd_chained" if chained else "ssd_batched",
    )(*args)


def _row(v):
    return v.reshape(1, -1).astype(F32)


def _pad_lanes(a, width=LANES):
    return jnp.pad(a, [(0, 0)] * (a.ndim - 1) + [(0, width - a.shape[-1])])


def _band_slabs(w):
    eye = jnp.eye(LRU_BLOCKS, dtype=w.dtype)
    dense = (w[:, :, None, :] * eye[:, None, :, None]).reshape(LRU_WIDTH, LRU_WIDTH)
    slabs = []
    for j in range(LRU_TILES):
        st = min(max(j - 1, 0), LRU_TILES - LRU_BAND)
        slabs.append(dense[st * LANES:(st + LRU_BAND) * LANES, j * LANES:(j + 1) * LANES])
    return jnp.stack(slabs)


def _gla_params(norm_w, w_in, w_gate_up, b_gate, gnorm_w, w_out):
    n_main = 2 * GLA_QK + 2 * GLA_V
    return {
        "norm_w": _row(norm_w),
        "w_main": w_in[:, :n_main].astype(BF16),
        "w_glr": _pad_lanes(w_in[:, n_main:]).astype(BF16),
        "w_gu": jnp.pad(w_gate_up, ((0, LANES - GLA_RANK), (0, 0))).astype(BF16),
        "b_gate": _row(b_gate),
        "gnorm_w": _row(gnorm_w),
        "w_out": w_out.astype(BF16),
    }


def _lru_params(norm_w, w_in, conv_w, conv_b, w_a, b_a, w_x, b_x, lam, w_out):
    return {
        "norm_w": _row(norm_w),
        "w_in": w_in.astype(BF16),
        "conv_w": conv_w.astype(F32),
        "conv_b": _row(conv_b),
        "w_band": jnp.concatenate([_band_slabs(w_a), _band_slabs(w_x)], axis=-1).astype(BF16),
        "b_a": _row(b_a),
        "b_x": _row(b_x),
        "lam": _row(lam),
        "w_out": w_out.astype(BF16),
    }


def _ssd_params(norm_w, w_in, conv_w, conv_b, dt_bias, a_log, d_skip, snorm_w, w_out):
    heads = np.arange(SSD_DI) // SSD_HEADDIM
    lane = np.arange(LANES)
    expand = jnp.asarray((lane[:, None] % SSD_HEADS == heads[None, :]) & (lane[:, None] < 3 * SSD_HEADS), BF16)
    reps = LANES // SSD_HEADS
    return {
        "norm_w": _row(norm_w),
        "w_z": w_in[:, :SSD_DI].astype(BF16),
        "w_xbc": w_in[:, SSD_DI:SSD_DI + SSD_CONV_DIM].astype(BF16),
        "w_dt": jnp.tile(w_in[:, SSD_DI + SSD_CONV_DIM:], (1, reps)).astype(BF16),
        "conv_w": conv_w.astype(F32),
        "conv_b": _row(conv_b),
        "dt_bias": jnp.tile(_row(dt_bias), (1, reps)),
        "a_log": jnp.tile(_row(a_log), (1, reps)),
        "d_exp": _row(jnp.repeat(d_skip, SSD_HEADDIM)),
        "snorm_w": _row(snorm_w),
        "w_out": w_out.astype(BF16),
        "expand": expand,
    }


def _trunk(x, st_gla, st_lru_conv, st_lru_h, st_ssd_conv, st_ssd_h, layers, final_norm_w, *, tiles, c, chained):
    new = {"gla": [], "lru_conv": [], "lru_h": [], "ssd_conv": [], "ssd_h": []}
    n_layers = len(layers)
    for i, (kind, j, p) in enumerate(layers):
        T = tiles[kind]
        if kind == "gla":
            fnw = final_norm_w if i == n_layers - 1 else None
            x, s = _gla_layer(x, None if chained else st_gla[j], p, T=T, c=c, chained=chained, final_norm_w=fnw)
            new["gla"].append(s)
        elif kind == "lru":
            x, cb, h = _lru_layer(x, None if chained else st_lru_conv[j], None if chained else st_lru_h[j], p,
                                  T=T, c=c, chained=chained)
            new["lru_conv"].append(cb)
            new["lru_h"].append(h)
        else:
            s0 = None if chained else st_ssd_h[j].reshape(-1, SSD_GROUPS, SSD_GW, SSD_STATE)
            x, cb, s = _ssd_layer(x, None if chained else st_ssd_conv[j], s0, p, T=T, c=c, chained=chained)
            new["ssd_conv"].append(cb)
            new["ssd_h"].append(s)
    return x, new


def kernel(x_prompt, x_sample, state_gla, state_lru_conv, state_lru_h, state_ssd_conv, state_ssd_h, norm_w, final_norm_w, gla_w_in, gla_w_gate_up, gla_b_gate, gla_norm_w, gla_w_out, lru_w_in, lru_conv_w, lru_conv_b, lru_w_a, lru_b_a, lru_w_x, lru_b_x, lru_lambda, lru_w_out, ssd_w_in, ssd_conv_w, ssd_conv_b, ssd_dt_bias, ssd_a_log, ssd_d, ssd_norm_w, ssd_w_out):
    depth = norm_w.shape[0]
    layers = []
    for i in range(depth):
        j = i // 3
        if i % 3 == 0:
            layers.append(("gla", j, _gla_params(norm_w[i], gla_w_in[j], gla_w_gate_up[j], gla_b_gate[j],
                                                 gla_norm_w[j], gla_w_out[j])))
        elif i % 3 == 1:
            layers.append(("lru", j, _lru_params(norm_w[i], lru_w_in[j], lru_conv_w[j], lru_conv_b[j], lru_w_a[j],
                                                 lru_b_a[j], lru_w_x[j], lru_b_x[j], lru_lambda[j], lru_w_out[j])))
        else:
            layers.append(("ssd", j, _ssd_params(norm_w[i], ssd_w_in[j], ssd_conv_w[j], ssd_conv_b[j],
                                                 ssd_dt_bias[j], ssd_a_log[j], ssd_d[j], ssd_norm_w[j],
                                                 ssd_w_out[j])))
    assert layers[-1][0] == "gla", "the final RMSNorm is fused into a GLA layer"
    fnw = _row(final_norm_w)

    bp, lp, _ = x_prompt.shape
    bs, ls, _ = x_sample.shape
    p_tiles = {k: min(t, lp) for k, t in PROMPT_TILE.items()}
    assert bp == 1 and all(lp % t == 0 for t in p_tiles.values()) and bs * ls == LANES and bs == SUBLANES

    yp, new_p = _trunk(x_prompt.reshape(lp, D_MODEL), None, None, None, None, None, layers, fnw,
                       tiles=p_tiles, c=min(CHUNK, lp), chained=True)
    ys, new_s = _trunk(x_sample.reshape(bs * ls, D_MODEL), state_gla, state_lru_conv, state_lru_h,
                       state_ssd_conv, state_ssd_h, layers, fnw, tiles=dict.fromkeys(PROMPT_TILE, bs * ls),
                       c=min(CHUNK, ls), chained=False)

    def stack(xs, shape):
        return jnp.stack(xs).reshape(shape)

    n_gla, n_lru, n_ssd = len(new_p["gla"]), len(new_p["lru_h"]), len(new_p["ssd_h"])
    return (
        yp.reshape(x_prompt.shape), ys.reshape(x_sample.shape),
        stack(new_p["gla"], (n_gla, 1, GLA_HEADS, GLA_DK, GLA_DV)),
        stack(new_p["lru_conv"], (n_lru, 1, CONV_W - 1, LRU_WIDTH)),
        stack(new_p["lru_h"], (n_lru, 1, LRU_WIDTH)),
        stack(new_p["ssd_conv"], (n_ssd, 1, CONV_W - 1, SSD_CONV_DIM)),
        stack(new_p["ssd_h"], (n_ssd, 1, SSD_HEADS, SSD_HEADDIM, SSD_STATE)),
        stack(new_s["gla"], (n_gla, bs, GLA_HEADS, GLA_DK, GLA_DV)),
        stack(new_s["lru_conv"], (n_lru, bs, CONV_W - 1, LRU_WIDTH)),
        stack(new_s["lru_h"], (n_lru, bs, LRU_WIDTH)),
        stack(new_s["ssd_conv"], (n_ssd, bs, CONV_W - 1, SSD_CONV_DIM)),
        stack(new_s["ssd_h"], (n_ssd, bs, SSD_HEADS, SSD_HEADDIM, SSD_STATE)),
    )
```

```python
import functools

import numpy as np
import jax
import jax.numpy as jnp
from jax import lax
from jax.experimental import pallas as pl
from jax.experimental.pallas import tpu as pltpu

F32 = jnp.float32
BF16 = jnp.bfloat16

D_MODEL = 1024
CHUNK = 64
EPS = 1e-6
CONV_W = 4

GLA_HEADS = 4
GLA_QK = D_MODEL // 2
GLA_V = D_MODEL
GLA_DK = GLA_QK // GLA_HEADS
GLA_DV = GLA_V // GLA_HEADS
GLA_RANK = 16
GLA_TAU = 16.0

LRU_WIDTH = 1408
LRU_BLOCKS = 16
LRU_BW = LRU_WIDTH // LRU_BLOCKS
LRU_C = 8.0

SSD_DI = 2 * D_MODEL
SSD_HEADDIM = 64
SSD_HEADS = SSD_DI // SSD_HEADDIM
SSD_STATE = 128
SSD_GROUPS = 4
SSD_HPG = SSD_HEADS // SSD_GROUPS
SSD_CONV_DIM = SSD_DI + 2 * SSD_GROUPS * SSD_STATE
SSD_GW = SSD_HPG * SSD_HEADDIM

LANES = 128
SUBLANES = 8
PROMPT_TILE = {"gla": 512, "lru": 512, "ssd": 256}
ROW_BLOCK = 32
PROJ_SLICE = 256
CUMSUM_ROWS = 256
VMEM_LIMIT_BYTES = 56 * 1024 * 1024
LRU_TILES = LRU_WIDTH // LANES
LRU_BAND = 3


def _bdot(a, b):
    return jnp.dot(a.astype(BF16), b.astype(BF16), preferred_element_type=F32)


def _bdot_nt(a, b):
    return lax.dot_general(a.astype(BF16), b.astype(BF16), (((1,), (1,)), ((), ())),
                           preferred_element_type=F32)


def _bdot_tn(a, b):
    return lax.dot_general(a.astype(BF16), b.astype(BF16), (((0,), (0,)), ((), ())),
                           preferred_element_type=F32)


def _split3(x):
    hi = x.astype(BF16)
    r1 = x - hi.astype(F32)
    mid = r1.astype(BF16)
    lo = (r1 - mid.astype(F32)).astype(BF16)
    return hi, mid, lo


def _sel_dot(sel, x):
    hi, mid, lo = _split3(x)
    d = lambda p: jnp.dot(sel, p, preferred_element_type=F32)
    return d(hi) + d(mid) + d(lo)


def _expand_heads(x, expand):
    hi = x.astype(BF16).astype(F32)
    r1 = x - hi
    mid = r1.astype(BF16).astype(F32)
    lane = lax.broadcasted_iota(jnp.int32, x.shape, 1)
    packed = jnp.where(lane < SSD_HEADS, hi, jnp.where(lane < 2 * SSD_HEADS, mid, r1 - mid))
    return jnp.dot(packed.astype(BF16), expand, preferred_element_type=F32)


def _sel_dot_nt(sel, x):
    hi, mid, lo = _split3(x)
    d = lambda p: lax.dot_general(sel, p, (((1,), (1,)), ((), ())), preferred_element_type=F32)
    return d(hi) + d(mid) + d(lo)


def _rms(x, w):
    return x * lax.rsqrt(jnp.mean(x * x, axis=-1, keepdims=True) + EPS) * w


def _softplus(x):
    return jnp.maximum(x, 0.0) + jnp.log1p(jnp.exp(-jnp.abs(x)))


def _sigmoid(x):
    return 0.5 * jnp.tanh(0.5 * x) + 0.5


def _silu(x):
    return x * _sigmoid(x)


class _Interleave:
    def __init__(self, pieces, slots):
        self._pieces, self._slots, self._calls, self._total = list(pieces), slots, 0, None

    def take(self, k):
        for _ in range(min(k, len(self._pieces))):
            self._pieces.pop(0)()

    def step(self):
        if self._total is None:
            self._total = len(self._pieces)
        self._calls += 1
        left_after = self._total - self._calls * self._total // self._slots
        self.take(len(self._pieces) - max(left_after, 0))

    def flush(self):
        self.take(len(self._pieces))


def _proj_slice(dst_ref, hb, w_ref, lo, off=0):
    hi = min(lo + PROJ_SLICE, w_ref.shape[1])
    dst_ref[:, off + lo:off + hi] = jnp.dot(hb, w_ref[:, lo:hi], preferred_element_type=F32)


def _scan_pitch(seg):
    return seg if (seg // SUBLANES) % 2 == 1 else seg + SUBLANES


_HDR = SUBLANES
_PAD0 = _HDR - (CONV_W - 1)


def _conv_seqs(T, c, chained):
    return [(0, 0, T)] if chained else [(b * (_HDR + c), b * c, c) for b in range(T // c)]


def _conv_fill(xpad_s, xb, cbuf_in, *, T, c, chained):
    nt = xpad_s.shape[0]
    for b, (base, r0, n) in enumerate(_conv_seqs(T, c, chained)):
        prev = None if chained else cbuf_in[b]
        for j in range(nt):
            lanes = slice(j * LANES, (j + 1) * LANES)
            if prev is not None:
                xpad_s[j, base + _PAD0:base + _HDR, :] = prev[:, lanes]
            xpad_s[j, base + _HDR:base + _HDR + n, :] = xb[r0:r0 + n, lanes]


def _conv_apply(xpad_s, xc_s, cw_ref, cb_ref, cbuf_out, *, T, c, chained, act=None, after_tile=None):
    nt = xpad_s.shape[0]
    blk = 2 * SUBLANES
    seqs = _conv_seqs(T, c, chained)
    for j in range(nt):
        lanes = slice(j * LANES, (j + 1) * LANES)
        w = [jnp.broadcast_to(cw_ref[k:k + 1, lanes], (SUBLANES, LANES)) for k in range(CONV_W)]
        bias = jnp.broadcast_to(cb_ref[:, lanes], (SUBLANES, LANES))
        for b, (base, r0, n) in enumerate(seqs):
            for i in range(n // blk):
                p0 = base + _PAD0 + i * blk
                win = [xpad_s[j, pl.ds(p0 + m, SUBLANES, stride=2), :] for m in range(CONV_W + 1)]
                for par in range(2):
                    y = bias
                    for k in range(CONV_W):
                        y = y + w[k] * win[par + k]
                    xc_s[j, pl.ds(r0 + i * blk + par, SUBLANES, stride=2), :] = y if act is None else act(y)
            tail = xpad_s[j, base + _HDR + n - (CONV_W - 1):base + _HDR + n, :]
            if chained:
                cbuf_out[:, lanes] = tail
                xpad_s[j, _PAD0:_HDR, :] = tail
            else:
                cbuf_out[b, :, lanes] = tail
        if after_tile is not None:
            after_tile()


def _gla_body(*refs, T, c, chained, final_norm):
    refs = list(refs)
    (x_ref, nw_ref, wm_ref, wglr_ref, wgu_ref, bg_ref, gnw_ref, wout_ref, tri_ref) = refs[:9]
    refs = refs[9:]
    s0_ref = None if chained else refs.pop(0)
    fnw_ref = refs.pop(0) if final_norm else None
    if chained:
        y_ref, sout_ref, proj_s, bcum_s, o_s, glr_s, xs_s, pin_s = refs
    else:
        y_ref, sout_ref, proj_s, bcum_s, o_s = refs
    nc = T // c

    if chained:
        @pl.when(pl.program_id(0) <= 1)
        def _():
            sout_ref[...] = jnp.zeros(sout_ref.shape, F32)

        @pl.when(pl.program_id(0) == 0)
        def _():
            proj_s[...] = jnp.zeros(proj_s.shape, F32)
            glr_s[...] = jnp.zeros(glr_s.shape, F32)
            xs_s[...] = jnp.zeros(xs_s.shape, F32)

    x_in = x_ref[...]
    hb = _rms(x_in, nw_ref[...]).astype(BF16)
    glr_in = jnp.dot(hb, wglr_ref[...], preferred_element_type=F32)
    if chained:
        x = xs_s[...]
        glr = glr_s[...]
        front = _Interleave([functools.partial(_proj_slice, pin_s, hb, wm_ref, lo)
                             for lo in range(0, wm_ref.shape[1], PROJ_SLICE)], nc * GLA_HEADS)
    else:
        x, glr = x_in, glr_in
        proj_s[...] = jnp.dot(hb, wm_ref[...], preferred_element_type=F32)
        front = _Interleave([], nc * GLA_HEADS)

    zg = _bdot(glr, wgu_ref[...]) + bg_ref[...]
    front.take(2)
    log_a = -_softplus(-zg) / GLA_TAU
    tb = tri_ref.shape[0]
    for b in range(T // tb):
        bcum_s[b * tb:(b + 1) * tb, :] = _sel_dot(tri_ref[...], log_a[b * tb:(b + 1) * tb, :])

    tril = lax.broadcasted_iota(jnp.int32, (c, c), 0) >= lax.broadcasted_iota(jnp.int32, (c, c), 1)
    k_off, v_off, g_off = GLA_QK, 2 * GLA_QK, 2 * GLA_QK + GLA_V

    dec_cols = []
    for hd in range(GLA_HEADS):
        kc = hd * GLA_DK
        last_rows = [bcum_s[n * c + c - 1:n * c + c, kc:kc + GLA_DK] for n in range(nc)]
        blk = jnp.concatenate(last_rows + [jnp.zeros((GLA_DK - nc, GLA_DK), F32)], axis=0)
        dec_cols.append(jnp.exp(blk.T))

    def chunk(n, carry):
        rows = pl.ds(n * c, c)
        for hd in range(GLA_HEADS):
            kc = hd * GLA_DK
            vc = hd * GLA_DV
            bc = bcum_s[rows, kc:kc + GLA_DK]
            bl = bc[c - 1:c, :]
            q = proj_s[rows, kc:kc + GLA_DK]
            k = proj_s[rows, k_off + kc:k_off + kc + GLA_DK]
            v = proj_s[rows, v_off + vc:v_off + vc + GLA_DV]
            g = proj_s[rows, g_off + vc:g_off + vc + GLA_DV]
            qd = q * (GLA_DK ** -0.5) * jnp.exp(bc)
            kd = k * jnp.exp(-bc)
            ke = k * jnp.exp(bl - bc)
            att = jnp.where(tril, _bdot_nt(qd, kd), 0.0)
            s_prev = sout_ref[hd] if chained else s0_ref[n, hd]
            o = _bdot(att, v) + _bdot(qd, s_prev)
            ds = _bdot_tn(ke, v)
            s_new = dec_cols[hd][:, n:n + 1] * s_prev + ds
            if chained:
                sout_ref[hd] = s_new
            else:
                sout_ref[n, hd] = s_new
            o = o * lax.rsqrt(jnp.mean(o * o, axis=-1, keepdims=True) + EPS) * gnw_ref[...]
            o_s[rows, vc:vc + GLA_DV] = o * _silu(g)
            front.step()
        return carry

    for n in range(nc):
        chunk(n, 0)
    front.flush()
    out = x + jnp.dot(o_s[...].astype(BF16), wout_ref[...], preferred_element_type=F32)
    if final_norm:
        out = _rms(out, fnw_ref[...])
    y_ref[...] = out
    if chained:
        proj_s[...] = pin_s[...]
        glr_s[...] = glr_in
        xs_s[...] = x_in


def _out_slice(y_ref, res_ref, lhs_ref, w_ref, lo):
    hi = min(lo + PROJ_SLICE, w_ref.shape[1])
    y_ref[:, lo:hi] = res_ref[:, lo:hi] + jnp.dot(lhs_ref[...], w_ref[:, lo:hi], preferred_element_type=F32)


def _lru_body(*refs, T, c, chained):
    refs = list(refs)
    (x_ref, nw_ref, win_ref, cw_ref, cb_ref, wband_ref, ba_ref, bx_ref, lam_ref, wout_ref) = refs[:10]
    refs = refs[10:]
    if chained:
        cbuf_in = h0_ref = None
        (y_ref, cbuf_out, h_out, xpad_s, xc_s, a_s, u_s, hin_s,
         gate_s, xs1_s, xs2_s, pin_s, gated_s, hc_s) = refs
    else:
        cbuf_in = refs.pop(0)
        h0_ref = refs.pop(0)
        y_ref, cbuf_out, h_out, xpad_s, xc_s, a_s, u_s, hin_s = refs
    W = LRU_WIDTH
    seg = T // SUBLANES
    tiles = range(LRU_TILES)
    lanes = lambda j: slice(j * LANES, (j + 1) * LANES)

    if chained:
        step = pl.program_id(0)

        @pl.when(step <= 1)
        def _():
            hc_s[...] = jnp.zeros(hc_s.shape, F32)

        @pl.when(step == 0)
        def _():
            xpad_s[...] = jnp.zeros(xpad_s.shape, F32)
            gate_s[...] = jnp.zeros(gate_s.shape, F32)
            xs1_s[...] = jnp.zeros(xs1_s.shape, F32)
            xs2_s[...] = jnp.zeros(xs2_s.shape, F32)
            gated_s[...] = jnp.zeros(gated_s.shape, BF16)

    x_in = x_ref[...]
    hb = _rms(x_in, nw_ref[...]).astype(BF16)
    if chained:
        pieces = [functools.partial(_out_slice, y_ref, xs2_s, gated_s, wout_ref, lo)
                  for lo in range(0, D_MODEL, PROJ_SLICE)]
        pieces += [functools.partial(_proj_slice, pin_s, hb, win_ref, lo) for lo in range(0, 2 * W, PROJ_SLICE)]
        n_conv, n_scan = 2, 4
        back = _Interleave(pieces[:n_conv], LRU_TILES)
        mid = _Interleave(pieces[n_conv:len(pieces) - n_scan], LRU_TILES)
        front = _Interleave(pieces[len(pieces) - n_scan:], 2 * seg)
    else:
        proj = jnp.dot(hb, win_ref[...], preferred_element_type=F32)
        _conv_fill(xpad_s, proj, cbuf_in, T=T, c=c, chained=False)
        back = mid = front = _Interleave([], 1)
    _conv_apply(xpad_s, xc_s, cw_ref, cb_ref, cbuf_out, T=T, c=c, chained=chained, after_tile=back.step)
    back.flush()

    pitch = _scan_pitch(seg)
    neg_c_sp = -LRU_C * _softplus(-lam_ref[...])
    def gate_pre(j):
        st = min(max(j - 1, 0), LRU_TILES - LRU_BAND)
        xin = jnp.concatenate([xc_s[st + m] for m in range(LRU_BAND)], axis=1)
        return _bdot(xin, wband_ref[j])

    pre_next = gate_pre(0)
    for j in tiles:
        pre = pre_next
        if j + 1 < LRU_TILES:
            pre_next = gate_pre(j + 1)
        mid.step()
        rb = min(seg, ROW_BLOCK)
        for s in range(SUBLANES):
            for o in range(0, seg, rb):
                rows = slice(s * seg + o, s * seg + o + rb)
                dst = slice(s * pitch + o, s * pitch + o + rb)
                r = _sigmoid(pre[rows, :LANES] + ba_ref[:, lanes(j)])
                i = _sigmoid(pre[rows, LANES:] + bx_ref[:, lanes(j)])
                log_a = r * neg_c_sp[:, lanes(j)]
                th = jnp.tanh(log_a)
                a_s[j, dst, :] = jnp.exp(log_a)
                u_s[j, dst, :] = ((jnp.sqrt(jnp.maximum(-2.0 * th, 0.0)) * lax.rsqrt(1.0 - th))
                                  * (i * xc_s[j, rows, :]))
    mid.flush()

    def seg_rows(t):
        return pl.ds(t, SUBLANES, stride=pitch)

    if chained:
        hs = [jnp.zeros((SUBLANES, LANES), F32) for _ in tiles]
        ps = [jnp.ones((SUBLANES, LANES), F32) for _ in tiles]
        for t in range(seg):
            for j in tiles:
                a = a_s[j, seg_rows(t), :]
                hs[j] = a * hs[j] + u_s[j, seg_rows(t), :]
                ps[j] = ps[j] * a
            front.step()
        h_end = jnp.concatenate(hs, axis=1)
        p_end = jnp.concatenate(ps, axis=1)
        hc = hc_s[...]
        for s in range(SUBLANES):
            hin_s[s:s + 1, :] = hc
            hc = h_end[s:s + 1, :] + p_end[s:s + 1, :] * hc
        hc_s[...] = hc
        h_init = hin_s[...]
    else:
        h_init = h0_ref[...]

    hs = [h_init[:, lanes(j)] for j in tiles]
    for t in range(seg):
        for j in tiles:
            hs[j] = a_s[j, seg_rows(t), :] * hs[j] + u_s[j, seg_rows(t), :]
            u_s[j, seg_rows(t), :] = hs[j]
        front.step()
    front.flush()

    def gated_tile(j, gate_j):
        h = jnp.concatenate([u_s[j, s * pitch:s * pitch + seg, :] for s in range(SUBLANES)], axis=0)
        return (h * _silu(gate_j)).astype(BF16)

    if chained:
        for j in tiles:
            for s in range(SUBLANES):
                rows = slice(s * seg, (s + 1) * seg)
                gated_s[rows, lanes(j)] = (u_s[j, s * pitch:s * pitch + seg, :]
                                           * _silu(gate_s[rows, lanes(j)])).astype(BF16)
        xs2_s[...] = xs1_s[...]
        xs1_s[...] = x_in
        _conv_fill(xpad_s, pin_s, None, T=T, c=c, chained=True)
        gate_s[...] = pin_s[:, W:]

        @pl.when(step == pl.num_programs(0) - 2)
        def _():
            h_out[...] = hc_s[...]
    else:
        h_out[...] = jnp.concatenate(hs, axis=1)
        gated = jnp.concatenate([gated_tile(j, proj[:, W + j * LANES:W + (j + 1) * LANES]) for j in tiles], axis=1)
        y_ref[...] = x_in + jnp.dot(gated, wout_ref[...], preferred_element_type=F32)


def _ssd_body(*refs, T, c, chained):
    refs = list(refs)
    (x_ref, nw_ref, wz_ref, wxbc_ref, wdt_ref, cw_ref, cb_ref, dtb_ref, alog_ref, de_ref, snw_ref,
     wout_ref, tri_ref, eye_ref, exp_ref) = refs[:15]
    refs = refs[15:]
    if chained:
        cbuf_in = s0_ref = None
    else:
        cbuf_in = refs.pop(0)
        s0_ref = refs.pop(0)
    if chained:
        (y_ref, cbuf_out, sout_ref, xpad_s, xbc_s, z_s, cum_s, dt_s, ecum_s, xw_s, yacc_s, cdec_s, st_s,
         dtp_s, xs_s, pin_s) = refs
    else:
        (y_ref, cbuf_out, sout_ref, xpad_s, xbc_s, z_s, cum_s, dt_s, ecum_s, xw_s, yacc_s, cdec_s) = refs
        st_s = None
    nc = T // c
    b_off = SSD_DI
    c_off = SSD_DI + SSD_GROUPS * SSD_STATE
    n_slots = SSD_CONV_DIM // LANES + nc * SSD_GROUPS * (SSD_HPG // 2)

    if chained:
        @pl.when(pl.program_id(0) <= 1)
        def _():
            st_s[...] = jnp.zeros(st_s.shape, F32)

        @pl.when(pl.program_id(0) == 0)
        def _():
            xpad_s[...] = jnp.zeros(xpad_s.shape, F32)
            z_s[...] = jnp.zeros(z_s.shape, F32)
            dtp_s[...] = jnp.zeros(dtp_s.shape, F32)
            xs_s[...] = jnp.zeros(xs_s.shape, F32)

    x_in = x_ref[...]
    hb = _rms(x_in, nw_ref[...]).astype(BF16)
    dtp_in = jnp.dot(hb, wdt_ref[...], preferred_element_type=F32)
    if chained:
        x = xs_s[...]
        dt_pre = dtp_s[...]
        pieces = [functools.partial(_proj_slice, pin_s, hb, wz_ref, lo) for lo in range(0, SSD_DI, PROJ_SLICE)]
        pieces += [functools.partial(_proj_slice, pin_s, hb, wxbc_ref, lo, SSD_DI)
                   for lo in range(0, SSD_CONV_DIM, PROJ_SLICE)]
        front = _Interleave(pieces, n_slots)
    else:
        x, dt_pre = x_in, dtp_in
        z_s[...] = jnp.dot(hb, wz_ref[...], preferred_element_type=F32)
        _conv_fill(xpad_s, jnp.dot(hb, wxbc_ref[...], preferred_element_type=F32), cbuf_in, T=T, c=c, chained=False)
        front = _Interleave([], n_slots)
    _conv_apply(xpad_s, xbc_s, cw_ref, cb_ref, cbuf_out, T=T, c=c, chained=chained, act=_silu,
                after_tile=front.step)

    dt = _softplus(dt_pre + dtb_ref[...])
    a_neg = -jnp.exp(alog_ref[...])
    cum = _sel_dot(tri_ref[...], dt * a_neg)
    cum_s[...] = cum
    dt_s[...] = dt
    lasts = [cum[n * c + c - 1:n * c + c, :] for n in range(nc)]
    cl = jnp.concatenate([jnp.broadcast_to(l, (c, LANES)) for l in lasts], axis=0)
    ecum_s[...] = _expand_heads(jnp.exp(cum), exp_ref[...])
    wb_e = _expand_heads(jnp.exp(cl - cum) * dt, exp_ref[...])
    for j in range(SSD_DI // LANES):
        xw_s[:, j * LANES:(j + 1) * LANES] = xbc_s[j] * wb_e[:, j * LANES:(j + 1) * LANES]
    cl_rows = jnp.concatenate(lasts + [jnp.zeros((SUBLANES - nc, LANES), F32)] * (nc < SUBLANES), axis=0)
    cdec_s[...] = _expand_heads(jnp.exp(cl_rows), exp_ref[...])

    tril = lax.broadcasted_iota(jnp.int32, (c, c), 0) >= lax.broadcasted_iota(jnp.int32, (c, c), 1)
    low_half = lax.broadcasted_iota(jnp.int32, (c, LANES), 1) < SSD_HEADDIM

    def chunk(n, carry):
        rows = pl.ds(n * c, c)
        cum_c = cum_s[rows, :]
        cum_t = _sel_dot_nt(eye_ref[...], cum_c)
        dt_t = _sel_dot_nt(eye_ref[...], dt_s[rows, :])
        for g in range(SSD_GROUPS):
            gl = g * SSD_GW
            bm = xbc_s[b_off // LANES + g, rows, :]
            cm = xbc_s[c_off // LANES + g, rows, :]
            cb = _bdot_nt(cm, bm)
            if chained:
                st_prev = st_s[g]
            else:
                st_prev = s0_ref[n, g].T
            y_inter = _bdot(cm, st_prev) * ecum_s[rows, gl:gl + SSD_GW]
            for pr in range(SSD_HPG // 2):
                h0 = g * SSD_HPG + 2 * pr
                xl = h0 * SSD_HEADDIM
                xp = xbc_s[h0 // 2, rows, :]
                acc = y_inter[:, pr * LANES:(pr + 1) * LANES]
                for e in range(2):
                    h = h0 + e
                    seg = cum_c[:, h:h + 1] - cum_t[h:h + 1, :]
                    dec = jnp.exp(jnp.where(tril, seg, -jnp.inf))
                    wgt = cb * dec * dt_t[h:h + 1, :]
                    xm = jnp.where(low_half if e == 0 else jnp.logical_not(low_half), xp, 0.0)
                    acc = acc + _bdot(wgt, xm)
                yacc_s[rows, xl:xl + LANES] = acc
                front.step()
            ds = _bdot_tn(bm, xw_s[rows, gl:gl + SSD_GW])
            st_new = st_prev * cdec_s[pl.ds(n, 1), gl:gl + SSD_GW] + ds
            if chained:
                st_s[g] = st_new
            else:
                sout_ref[n, g] = st_new.T
        return carry

    for n in range(nc):
        chunk(n, 0)
    front.flush()

    xs = jnp.concatenate([xbc_s[j] for j in range(SSD_DI // LANES)], axis=1)
    y = yacc_s[...] + de_ref[...] * xs
    yz = _rms(y * _silu(z_s[...]), snw_ref[...]).astype(BF16)
    y_ref[...] = x + jnp.dot(yz, wout_ref[...], preferred_element_type=F32)

    if chained:
        z_s[...] = pin_s[:, 0:SSD_DI]
        _conv_fill(xpad_s, pin_s.at[:, SSD_DI:], None, T=T, c=c, chained=True)
        dtp_s[...] = dtp_in
        xs_s[...] = x_in

        @pl.when(pl.program_id(0) == pl.num_programs(0) - 1)
        def _():
            for g in range(SSD_GROUPS):
                sout_ref[g] = st_s[g].T


def _const_spec(shape):
    nd = len(shape)
    return pl.BlockSpec(shape, lambda i, _nd=nd: (0,) * _nd, pipeline_mode=pl.Buffered(1))


def _row_specs(rows, T, lag):
    n = rows // T
    lag = int(lag)
    if lag == 0:
        spec = pl.BlockSpec((T, D_MODEL), lambda i: (i, 0))
        return (n,), spec, spec
    return ((n + lag,),
            pl.BlockSpec((T, D_MODEL), lambda i: (jnp.minimum(i, n - 1), 0)),
            pl.BlockSpec((T, D_MODEL), lambda i: (jnp.maximum(i - lag, 0), 0)))


def _params():
    return pltpu.CompilerParams(dimension_semantics=("arbitrary",), vmem_limit_bytes=VMEM_LIMIT_BYTES)


def _block_tri(T, c):
    r = np.arange(T)
    return jnp.asarray((r[:, None] // c == r[None, :] // c) & (r[:, None] >= r[None, :]), BF16)


def _eye():
    return jnp.asarray(np.eye(LANES), BF16)


def _gla_layer(x, s0, p, *, T, c, chained, final_norm_w=None):
    rows = x.shape[0]
    consts = [p["norm_w"], p["w_main"], p["w_glr"], p["w_gu"], p["b_gate"], p["gnorm_w"], p["w_out"],
              _block_tri(min(T, CUMSUM_ROWS), c)]
    args = [x] + consts
    grid, x_spec, y_spec = _row_specs(rows, T, chained)
    specs = [x_spec] + [_const_spec(a.shape) for a in consts]
    if not chained:
        args.append(s0)
        specs.append(_const_spec(s0.shape))
    if final_norm_w is not None:
        args.append(final_norm_w)
        specs.append(_const_spec(final_norm_w.shape))
    st_shape = (GLA_HEADS, GLA_DK, GLA_DV) if chained else s0.shape
    scratch = [pltpu.VMEM((T, 2 * GLA_QK + 2 * GLA_V), F32), pltpu.VMEM((T, GLA_QK), F32),
               pltpu.VMEM((T, GLA_V), F32)]
    if chained:
        scratch += [pltpu.VMEM((T, LANES), F32), pltpu.VMEM((T, D_MODEL), F32),
                    pltpu.VMEM((T, 2 * GLA_QK + 2 * GLA_V), F32)]
    body = functools.partial(_gla_body, T=T, c=c, chained=chained, final_norm=final_norm_w is not None)
    return pl.pallas_call(
        body,
        grid=grid,
        in_specs=specs,
        out_specs=[y_spec, pl.BlockSpec(st_shape, lambda i, _n=len(st_shape): (0,) * _n)],
        out_shape=[jax.ShapeDtypeStruct((rows, D_MODEL), F32), jax.ShapeDtypeStruct(st_shape, F32)],
        scratch_shapes=scratch,
        compiler_params=_params(),
        name="gla_chained" if chained else "gla_batched",
    )(*args)


def _lru_layer(x, cbuf, h0, p, *, T, c, chained):
    rows = x.shape[0]
    W = LRU_WIDTH
    consts = [p["norm_w"], p["w_in"], p["conv_w"], p["conv_b"], p["w_band"], p["b_a"], p["b_x"], p["lam"], p["w_out"]]
    args = [x] + consts
    grid, x_spec, y_spec = _row_specs(rows, T, 2 if chained else 0)
    specs = [x_spec] + [_const_spec(a.shape) for a in consts]
    if chained:
        cb_shape, h_shape = (CONV_W - 1, W), (1, W)
        pad_rows = SUBLANES + T
    else:
        args += [cbuf, h0]
        specs += [_const_spec(cbuf.shape), _const_spec(h0.shape)]
        cb_shape, h_shape = cbuf.shape, h0.shape
        pad_rows = (T // c) * (SUBLANES + c)
    scan_rows = SUBLANES * _scan_pitch(T // SUBLANES)
    body = functools.partial(_lru_body, T=T, c=c, chained=chained)
    zero_map = lambda n: (lambda i: (0,) * n)
    return pl.pallas_call(
        body,
        grid=grid,
        in_specs=specs,
        out_specs=[y_spec, pl.BlockSpec(cb_shape, zero_map(len(cb_shape))),
                   pl.BlockSpec(h_shape, zero_map(len(h_shape)))],
        out_shape=[jax.ShapeDtypeStruct((rows, D_MODEL), F32), jax.ShapeDtypeStruct(cb_shape, F32),
                   jax.ShapeDtypeStruct(h_shape, F32)],
        scratch_shapes=[pltpu.VMEM((LRU_TILES, pad_rows, LANES), F32), pltpu.VMEM((LRU_TILES, T, LANES), F32)]
                       + [pltpu.VMEM((LRU_TILES, scan_rows, LANES), F32)] * 2 + [pltpu.VMEM((SUBLANES, W), F32)]
                       + [pltpu.VMEM((T, W), F32), pltpu.VMEM((T, D_MODEL), F32), pltpu.VMEM((T, D_MODEL), F32),
                          pltpu.VMEM((T, 2 * W), F32), pltpu.VMEM((T, W), BF16), pltpu.VMEM((1, W), F32)] * chained,
        compiler_params=_params(),
        name="lru_chained" if chained else "lru_batched",
    )(*args)


def _ssd_layer(x, cbuf, s0, p, *, T, c, chained):
    rows = x.shape[0]
    consts = [p["norm_w"], p["w_z"], p["w_xbc"], p["w_dt"], p["conv_w"], p["conv_b"], p["dt_bias"], p["a_log"],
              p["d_exp"], p["snorm_w"], p["w_out"], _block_tri(T, c), _eye(), p["expand"]]
    args = [x] + consts
    grid, x_spec, y_spec = _row_specs(rows, T, chained)
    specs = [x_spec] + [_const_spec(a.shape) for a in consts]
    if chained:
        cb_shape = (CONV_W - 1, SSD_CONV_DIM)
        st_shape = (SSD_GROUPS, SSD_GW, SSD_STATE)
        pad_rows = SUBLANES + T
    else:
        args += [cbuf, s0]
        specs += [_const_spec(cbuf.shape), _const_spec(s0.shape)]
        cb_shape, st_shape = cbuf.shape, s0.shape
        pad_rows = (T // c) * (SUBLANES + c)
    conv_tiles = SSD_CONV_DIM // LANES
    scratch = [pltpu.VMEM((conv_tiles, pad_rows, LANES), F32), pltpu.VMEM((conv_tiles, T, LANES), F32),
               pltpu.VMEM((T, SSD_DI), F32), pltpu.VMEM((T, LANES), F32), pltpu.VMEM((T, LANES), F32),
               pltpu.VMEM((T, SSD_DI), F32), pltpu.VMEM((T, SSD_DI), F32), pltpu.VMEM((T, SSD_DI), F32),
               pltpu.VMEM((SUBLANES, SSD_DI), F32)]
    if chained:
        scratch += [pltpu.VMEM((SSD_GROUPS, SSD_STATE, SSD_GW), F32), pltpu.VMEM((T, LANES), F32),
                    pltpu.VMEM((T, D_MODEL), F32), pltpu.VMEM((T, SSD_DI + SSD_CONV_DIM), F32)]
    body = functools.partial(_ssd_body, T=T, c=c, chained=chained)
    zero_map = lambda n: (lambda i: (0,) * n)
    return pl.pallas_call(
        body,
        grid=grid,
        in_specs=specs,
        out_specs=[y_spec, pl.BlockSpec(cb_shape, zero_map(len(cb_shape))),
                   pl.BlockSpec(st_shape, zero_map(len(st_shape)))],
        out_shape=[jax.ShapeDtypeStruct((rows, D_MODEL), F32), jax.ShapeDtypeStruct(cb_shape, F32),
                   jax.ShapeDtypeStruct(st_shape, F32)],
        scratch_shapes=scratch,
        compiler_params=_params(),
        name="ssd_chained" if chained else "ssd_batched",
    )(*args)


def _row(v):
    return v.reshape(1, -1).astype(F32)


def _pad_lanes(a, width=LANES):
    return jnp.pad(a, [(0, 0)] * (a.ndim - 1) + [(0, width - a.shape[-1])])


def _band_slabs(w):
    eye = jnp.eye(LRU_BLOCKS, dtype=w.dtype)
    dense = (w[:, :, None, :] * eye[:, None, :, None]).reshape(LRU_WIDTH, LRU_WIDTH)
    slabs = []
    for j in range(LRU_TILES):
        st = min(max(j - 1, 0), LRU_TILES - LRU_BAND)
        slabs.append(dense[st * LANES:(st + LRU_BAND) * LANES, j * LANES:(j + 1) * LANES])
    return jnp.stack(slabs)


def _gla_params(norm_w, w_in, w_gate_up, b_gate, gnorm_w, w_out):
    n_main = 2 * GLA_QK + 2 * GLA_V
    return {
        "norm_w": _row(norm_w),
        "w_main": w_in[:, :n_main].astype(BF16),
        "w_glr": _pad_lanes(w_in[:, n_main:]).astype(BF16),
        "w_gu": jnp.pad(w_gate_up, ((0, LANES - GLA_RANK), (0, 0))).astype(BF16),
        "b_gate": _row(b_gate),
        "gnorm_w": _row(gnorm_w),
        "w_out": w_out.astype(BF16),
    }


def _lru_params(norm_w, w_in, conv_w, conv_b, w_a, b_a, w_x, b_x, lam, w_out):
    return {
        "norm_w": _row(norm_w),
        "w_in": w_in.astype(BF16),
        "conv_w": conv_w.astype(F32),
        "conv_b": _row(conv_b),
        "w_band": jnp.concatenate([_band_slabs(w_a), _band_slabs(w_x)], axis=-1).astype(BF16),
        "b_a": _row(b_a),
        "b_x": _row(b_x),
        "lam": _row(lam),
        "w_out": w_out.astype(BF16),
    }


def _ssd_params(norm_w, w_in, conv_w, conv_b, dt_bias, a_log, d_skip, snorm_w, w_out):
    heads = np.arange(SSD_DI) // SSD_HEADDIM
    lane = np.arange(LANES)
    expand = jnp.asarray((lane[:, None] % SSD_HEADS == heads[None, :]) & (lane[:, None] < 3 * SSD_HEADS), BF16)
    reps = LANES // SSD_HEADS
    return {
        "norm_w": _row(norm_w),
        "w_z": w_in[:, :SSD_DI].astype(BF16),
        "w_xbc": w_in[:, SSD_DI:SSD_DI + SSD_CONV_DIM].astype(BF16),
        "w_dt": jnp.tile(w_in[:, SSD_DI + SSD_CONV_DIM:], (1, reps)).astype(BF16),
        "conv_w": conv_w.astype(F32),
        "conv_b": _row(conv_b),
        "dt_bias": jnp.tile(_row(dt_bias), (1, reps)),
        "a_log": jnp.tile(_row(a_log), (1, reps)),
        "d_exp": _row(jnp.repeat(d_skip, SSD_HEADDIM)),
        "snorm_w": _row(snorm_w),
        "w_out": w_out.astype(BF16),
        "expand": expand,
    }


def _trunk(x, st_gla, st_lru_conv, st_lru_h, st_ssd_conv, st_ssd_h, layers, final_norm_w, *, tiles, c, chained):
    new = {"gla": [], "lru_conv": [], "lru_h": [], "ssd_conv": [], "ssd_h": []}
    n_layers = len(layers)
    for i, (kind, j, p) in enumerate(layers):
        T = tiles[kind]
        if kind == "gla":
            fnw = final_norm_w if i == n_layers - 1 else None
            x, s = _gla_layer(x, None if chained else st_gla[j], p, T=T, c=c, chained=chained, final_norm_w=fnw)
            new["gla"].append(s)
        elif kind == "lru":
            x, cb, h = _lru_layer(x, None if chained else st_lru_conv[j], None if chained else st_lru_h[j], p,
                                  T=T, c=c, chained=chained)
            new["lru_conv"].append(cb)
            new["lru_h"].append(h)
        else:
            s0 = None if chained else st_ssd_h[j].reshape(-1, SSD_GROUPS, SSD_GW, SSD_STATE)
            x, cb, s = _ssd_layer(x, None if chained else st_ssd_conv[j], s0, p, T=T, c=c, chained=chained)
            new["ssd_conv"].append(cb)
            new["ssd_h"].append(s)
    return x, new


def kernel(x_prompt, x_sample, state_gla, state_lru_conv, state_lru_h, state_ssd_conv, state_ssd_h, norm_w, final_norm_w, gla_w_in, gla_w_gate_up, gla_b_gate, gla_norm_w, gla_w_out, lru_w_in, lru_conv_w, lru_conv_b, lru_w_a, lru_b_a, lru_w_x, lru_b_x, lru_lambda, lru_w_out, ssd_w_in, ssd_conv_w, ssd_conv_b, ssd_dt_bias, ssd_a_log, ssd_d, ssd_norm_w, ssd_w_out):
    depth = norm_w.shape[0]
    layers = []
    for i in range(depth):
        j = i // 3
        if i % 3 == 0:
            layers.append(("gla", j, _gla_params(norm_w[i], gla_w_in[j], gla_w_gate_up[j], gla_b_gate[j],
                                                 gla_norm_w[j], gla_w_out[j])))
        elif i % 3 == 1:
            layers.append(("lru", j, _lru_params(norm_w[i], lru_w_in[j], lru_conv_w[j], lru_conv_b[j], lru_w_a[j],
                                                 lru_b_a[j], lru_w_x[j], lru_b_x[j], lru_lambda[j], lru_w_out[j])))
        else:
            layers.append(("ssd", j, _ssd_params(norm_w[i], ssd_w_in[j], ssd_conv_w[j], ssd_conv_b[j],
                                                 ssd_dt_bias[j], ssd_a_log[j], ssd_d[j], ssd_norm_w[j],
                                                 ssd_w_out[j])))
    assert layers[-1][0] == "gla", "the final RMSNorm is fused into a GLA layer"
    fnw = _row(final_norm_w)

    bp, lp, _ = x_prompt.shape
    bs, ls, _ = x_sample.shape
    p_tiles = {k: min(t, lp) for k, t in PROMPT_TILE.items()}
    assert bp == 1 and all(lp % t == 0 for t in p_tiles.values()) and bs * ls == LANES and bs == SUBLANES

    yp, new_p = _trunk(x_prompt.reshape(lp, D_MODEL), None, None, None, None, None, layers, fnw,
                       tiles=p_tiles, c=min(CHUNK, lp), chained=True)
    ys, new_s = _trunk(x_sample.reshape(bs * ls, D_MODEL), state_gla, state_lru_conv, state_lru_h,
                       state_ssd_conv, state_ssd_h, layers, fnw, tiles=dict.fromkeys(PROMPT_TILE, bs * ls),
                       c=min(CHUNK, ls), chained=False)

    def stack(xs, shape):
        return jnp.stack(xs).reshape(shape)

    n_gla, n_lru, n_ssd = len(new_p["gla"]), len(new_p["lru_h"]), len(new_p["ssd_h"])
    return (
        yp.reshape(x_prompt.shape), ys.reshape(x_sample.shape),
        stack(new_p["gla"], (n_gla, 1, GLA_HEADS, GLA_DK, GLA_DV)),
        stack(new_p["lru_conv"], (n_lru, 1, CONV_W - 1, LRU_WIDTH)),
        stack(new_p["lru_h"], (n_lru, 1, LRU_WIDTH)),
        stack(new_p["ssd_conv"], (n_ssd, 1, CONV_W - 1, SSD_CONV_DIM)),
        stack(new_p["ssd_h"], (n_ssd, 1, SSD_HEADS, SSD_HEADDIM, SSD_STATE)),
        stack(new_s["gla"], (n_gla, bs, GLA_HEADS, GLA_DK, GLA_DV)),
        stack(new_s["lru_conv"], (n_lru, bs, CONV_W - 1, LRU_WIDTH)),
        stack(new_s["lru_h"], (n_lru, bs, LRU_WIDTH)),
        stack(new_s["ssd_conv"], (n_ssd, bs, CONV_W - 1, SSD_CONV_DIM)),
        stack(new_s["ssd_h"], (n_ssd, bs, SSD_HEADS, SSD_HEADDIM, SSD_STATE)),
    )
```

```python
import functools

import numpy as np
import jax
import jax.numpy as jnp
from jax import lax
from jax.experimental import pallas as pl
from jax.experimental.pallas import tpu as pltpu

F32 = jnp.float32
BF16 = jnp.bfloat16

D_MODEL = 1024
CHUNK = 64
EPS = 1e-6
CONV_W = 4

GLA_HEADS = 4
GLA_QK = D_MODEL // 2
GLA_V = D_MODEL
GLA_DK = GLA_QK // GLA_HEADS
GLA_DV = GLA_V // GLA_HEADS
GLA_RANK = 16
GLA_TAU = 16.0
GLA_MAIN = 2 * GLA_QK + 2 * GLA_V

LRU_WIDTH = 1408
LRU_BLOCKS = 16
LRU_BW = LRU_WIDTH // LRU_BLOCKS
LRU_C = 8.0

SSD_DI = 2 * D_MODEL
SSD_HEADDIM = 64
SSD_HEADS = SSD_DI // SSD_HEADDIM
SSD_STATE = 128
SSD_GROUPS = 4
SSD_HPG = SSD_HEADS // SSD_GROUPS
SSD_CONV_DIM = SSD_DI + 2 * SSD_GROUPS * SSD_STATE
SSD_GW = SSD_HPG * SSD_HEADDIM

LANES = 128
SUBLANES = 8
PROMPT_TILE = {"gla": 512, "lru": 512, "ssd": 256}
ROW_BLOCK = 32
PROJ_SLICE = 256
CUMSUM_ROWS = 256
VMEM_LIMIT_BYTES = 56 * 1024 * 1024
LRU_TILES = LRU_WIDTH // LANES
LRU_BAND = 3


def _bdot(a, b):
    return jnp.dot(a.astype(BF16), b.astype(BF16), preferred_element_type=F32)


def _bdot_nt(a, b):
    return lax.dot_general(a.astype(BF16), b.astype(BF16), (((1,), (1,)), ((), ())),
                           preferred_element_type=F32)


def _bdot_tn(a, b):
    return lax.dot_general(a.astype(BF16), b.astype(BF16), (((0,), (0,)), ((), ())),
                           preferred_element_type=F32)


def _split3(x):
    hi = x.astype(BF16)
    r1 = x - hi.astype(F32)
    mid = r1.astype(BF16)
    lo = (r1 - mid.astype(F32)).astype(BF16)
    return hi, mid, lo


def _sel_dot(sel, x):
    hi, mid, lo = _split3(x)
    d = lambda p: jnp.dot(sel, p, preferred_element_type=F32)
    return d(hi) + d(mid) + d(lo)


def _expand_heads(x, expand):
    hi = x.astype(BF16).astype(F32)
    r1 = x - hi
    mid = r1.astype(BF16).astype(F32)
    lane = lax.broadcasted_iota(jnp.int32, x.shape, 1)
    packed = jnp.where(lane < SSD_HEADS, hi, jnp.where(lane < 2 * SSD_HEADS, mid, r1 - mid))
    return jnp.dot(packed.astype(BF16), expand, preferred_element_type=F32)


def _sel_dot_nt(sel, x):
    hi, mid, lo = _split3(x)
    d = lambda p: lax.dot_general(sel, p, (((1,), (1,)), ((), ())), preferred_element_type=F32)
    return d(hi) + d(mid) + d(lo)


def _rms(x, w):
    return x * lax.rsqrt(jnp.mean(x * x, axis=-1, keepdims=True) + EPS) * w


def _softplus(x):
    return jnp.maximum(x, 0.0) + jnp.log1p(jnp.exp(-jnp.abs(x)))


def _sigmoid(x):
    return 0.5 * jnp.tanh(0.5 * x) + 0.5


def _silu(x):
    return x * _sigmoid(x)


class _Interleave:
    def __init__(self, pieces, slots):
        self._pieces, self._slots, self._calls, self._total = list(pieces), slots, 0, None

    def take(self, k):
        for _ in range(min(k, len(self._pieces))):
            self._pieces.pop(0)()

    def step(self):
        if self._total is None:
            self._total = len(self._pieces)
        self._calls += 1
        left_after = self._total - self._calls * self._total // self._slots
        self.take(len(self._pieces) - max(left_after, 0))

    def flush(self):
        self.take(len(self._pieces))


def _proj_slice(dst_ref, hb, w_ref, lo, off=0):
    hi = min(lo + PROJ_SLICE, w_ref.shape[1])
    dst_ref[:, off + lo:off + hi] = jnp.dot(hb, w_ref[:, lo:hi], preferred_element_type=F32)


def _scan_pitch(seg):
    return seg if (seg // SUBLANES) % 2 == 1 else seg + SUBLANES


_HDR = SUBLANES
_PAD0 = _HDR - (CONV_W - 1)


def _conv_seqs(T, c, chained):
    return [(0, 0, T)] if chained else [(b * (_HDR + c), b * c, c) for b in range(T // c)]


def _conv_fill(xpad_s, xb, cbuf_in, *, T, c, chained):
    nt = xpad_s.shape[0]
    for b, (base, r0, n) in enumerate(_conv_seqs(T, c, chained)):
        prev = None if chained else cbuf_in[b]
        for j in range(nt):
            lanes = slice(j * LANES, (j + 1) * LANES)
            if prev is not None:
                xpad_s[j, base + _PAD0:base + _HDR, :] = prev[:, lanes]
            xpad_s[j, base + _HDR:base + _HDR + n, :] = xb[r0:r0 + n, lanes]


def _conv_apply(xpad_s, xc_s, cw_ref, cb_ref, cbuf_out, *, T, c, chained, act=None, after_tile=None):
    nt = xpad_s.shape[0]
    blk = 2 * SUBLANES
    seqs = _conv_seqs(T, c, chained)
    for j in range(nt):
        lanes = slice(j * LANES, (j + 1) * LANES)
        w = [jnp.broadcast_to(cw_ref[k:k + 1, lanes], (SUBLANES, LANES)) for k in range(CONV_W)]
        bias = jnp.broadcast_to(cb_ref[:, lanes], (SUBLANES, LANES))
        for b, (base, r0, n) in enumerate(seqs):
            for i in range(n // blk):
                p0 = base + _PAD0 + i * blk
                win = [xpad_s[j, pl.ds(p0 + m, SUBLANES, stride=2), :] for m in range(CONV_W + 1)]
                for par in range(2):
                    y = bias
                    for k in range(CONV_W):
                        y = y + w[k] * win[par + k]
                    xc_s[j, pl.ds(r0 + i * blk + par, SUBLANES, stride=2), :] = y if act is None else act(y)
            tail = xpad_s[j, base + _HDR + n - (CONV_W - 1):base + _HDR + n, :]
            if chained:
                cbuf_out[:, lanes] = tail
                xpad_s[j, _PAD0:_HDR, :] = tail
            else:
                cbuf_out[b, :, lanes] = tail
        if after_tile is not None:
            after_tile()


def _gla_body(*refs, T, c, chained, final_norm):
    refs = list(refs)
    (x_ref, nw_ref, wm_ref, wglr_ref, wgu_ref, bg_ref, gnw_ref, wout_ref, tri_ref) = refs[:9]
    refs = refs[9:]
    s0_ref = None if chained else refs.pop(0)
    fnw_ref = refs.pop(0) if final_norm else None
    if chained:
        y_ref, sout_ref, proj_s, bcum_s, o_s, glr_s, xs_s, pin_s = refs
    else:
        y_ref, sout_ref, proj_s, bcum_s, o_s = refs
    nc = T // c

    if chained:
        @pl.when(pl.program_id(0) <= 1)
        def _():
            sout_ref[...] = jnp.zeros(sout_ref.shape, F32)

        @pl.when(pl.program_id(0) == 0)
        def _():
            proj_s[...] = jnp.zeros(proj_s.shape, F32)
            glr_s[...] = jnp.zeros(glr_s.shape, F32)
            xs_s[...] = jnp.zeros(xs_s.shape, F32)

    x_in = x_ref[...]
    hb = _rms(x_in, nw_ref[...]).astype(BF16)
    glr_in = jnp.dot(hb, wglr_ref[...], preferred_element_type=F32)
    if chained:
        x = xs_s[...]
        glr = glr_s[...]
        front = _Interleave([functools.partial(_proj_slice, pin_s, hb, wm_ref, lo)
                             for lo in range(0, GLA_MAIN, PROJ_SLICE)], nc * GLA_HEADS)
    else:
        x, glr = x_in, glr_in
        proj_s[...] = jnp.dot(hb, wm_ref[:, 0:GLA_MAIN], preferred_element_type=F32)
        front = _Interleave([], nc * GLA_HEADS)

    zg = _bdot(glr, wgu_ref[...]) + bg_ref[...]
    front.take(2)
    log_a = -_softplus(-zg) / GLA_TAU
    tb = tri_ref.shape[0]
    for b in range(T // tb):
        bcum_s[b * tb:(b + 1) * tb, :] = _sel_dot(tri_ref[...], log_a[b * tb:(b + 1) * tb, :])

    tril = lax.broadcasted_iota(jnp.int32, (c, c), 0) >= lax.broadcasted_iota(jnp.int32, (c, c), 1)
    k_off, v_off, g_off = GLA_QK, 2 * GLA_QK, 2 * GLA_QK + GLA_V

    dec_cols = []
    for hd in range(GLA_HEADS):
        kc = hd * GLA_DK
        last_rows = [bcum_s[n * c + c - 1:n * c + c, kc:kc + GLA_DK] for n in range(nc)]
        blk = jnp.concatenate(last_rows + [jnp.zeros((GLA_DK - nc, GLA_DK), F32)], axis=0)
        dec_cols.append(jnp.exp(blk.T))

    def chunk(n, carry):
        rows = pl.ds(n * c, c)
        for hd in range(GLA_HEADS):
            kc = hd * GLA_DK
            vc = hd * GLA_DV
            bc = bcum_s[rows, kc:kc + GLA_DK]
            bl = bc[c - 1:c, :]
            q = proj_s[rows, kc:kc + GLA_DK]
            k = proj_s[rows, k_off + kc:k_off + kc + GLA_DK]
            v = proj_s[rows, v_off + vc:v_off + vc + GLA_DV]
            g = proj_s[rows, g_off + vc:g_off + vc + GLA_DV]
            qd = q * (GLA_DK ** -0.5) * jnp.exp(bc)
            kd = k * jnp.exp(-bc)
            ke = k * jnp.exp(bl - bc)
            att = jnp.where(tril, _bdot_nt(qd, kd), 0.0)
            s_prev = sout_ref[hd] if chained else s0_ref[0, n, hd]
            o = _bdot(att, v) + _bdot(qd, s_prev)
            ds = _bdot_tn(ke, v)
            s_new = dec_cols[hd][:, n:n + 1] * s_prev + ds
            if chained:
                sout_ref[hd] = s_new
            else:
                sout_ref[n, hd] = s_new
            o = o * lax.rsqrt(jnp.mean(o * o, axis=-1, keepdims=True) + EPS) * gnw_ref[...]
            o_s[rows, vc:vc + GLA_DV] = o * _silu(g)
            front.step()
        return carry

    for n in range(nc):
        chunk(n, 0)
    front.flush()
    out = x + jnp.dot(o_s[...].astype(BF16), wout_ref[...], preferred_element_type=F32)
    if final_norm:
        out = _rms(out, fnw_ref[...])
    y_ref[...] = out
    if chained:
        proj_s[...] = pin_s[...]
        glr_s[...] = glr_in
        xs_s[...] = x_in


def _out_slice(y_ref, res_ref, lhs_ref, w_ref, lo):
    hi = min(lo + PROJ_SLICE, w_ref.shape[1])
    y_ref[:, lo:hi] = res_ref[:, lo:hi] + jnp.dot(lhs_ref[...], w_ref[:, lo:hi], preferred_element_type=F32)


def _lru_body(*refs, T, c, chained):
    refs = list(refs)
    (x_ref, nw_ref, win_ref, cw_ref, cb_ref, wband_ref, ba_ref, bx_ref, lam_ref, wout_ref) = refs[:10]
    refs = refs[10:]
    if chained:
        cbuf_in = h0_ref = None
        (y_ref, cbuf_out, h_out, xpad_s, xc_s, a_s, u_s, hin_s,
         gate_s, xs1_s, xs2_s, pin_s, gated_s, hc_s) = refs
    else:
        cbuf_in = refs.pop(0)
        h0_ref = refs.pop(0)
        y_ref, cbuf_out, h_out, xpad_s, xc_s, a_s, u_s, hin_s = refs
    W = LRU_WIDTH
    seg = T // SUBLANES
    tiles = range(LRU_TILES)
    lanes = lambda j: slice(j * LANES, (j + 1) * LANES)

    if chained:
        step = pl.program_id(0)

        @pl.when(step <= 1)
        def _():
            hc_s[...] = jnp.zeros(hc_s.shape, F32)

        @pl.when(step == 0)
        def _():
            xpad_s[...] = jnp.zeros(xpad_s.shape, F32)
            gate_s[...] = jnp.zeros(gate_s.shape, F32)
            xs1_s[...] = jnp.zeros(xs1_s.shape, F32)
            xs2_s[...] = jnp.zeros(xs2_s.shape, F32)
            gated_s[...] = jnp.zeros(gated_s.shape, BF16)

    x_in = x_ref[...]
    hb = _rms(x_in, nw_ref[...]).astype(BF16)
    if chained:
        pieces = [functools.partial(_out_slice, y_ref, xs2_s, gated_s, wout_ref, lo)
                  for lo in range(0, D_MODEL, PROJ_SLICE)]
        pieces += [functools.partial(_proj_slice, pin_s, hb, win_ref, lo) for lo in range(0, 2 * W, PROJ_SLICE)]
        n_conv, n_scan = 2, 4
        back = _Interleave(pieces[:n_conv], LRU_TILES)
        mid = _Interleave(pieces[n_conv:len(pieces) - n_scan], LRU_TILES)
        front = _Interleave(pieces[len(pieces) - n_scan:], 2 * seg)
    else:
        proj = jnp.dot(hb, win_ref[...], preferred_element_type=F32)
        _conv_fill(xpad_s, proj, cbuf_in, T=T, c=c, chained=False)
        back = mid = front = _Interleave([], 1)
    _conv_apply(xpad_s, xc_s, cw_ref, cb_ref, cbuf_out, T=T, c=c, chained=chained, after_tile=back.step)
    back.flush()

    pitch = _scan_pitch(seg)
    neg_c_sp = -LRU_C * _softplus(-lam_ref[...])
    def gate_pre(j):
        st = min(max(j - 1, 0), LRU_TILES - LRU_BAND)
        xin = jnp.concatenate([xc_s[st + m] for m in range(LRU_BAND)], axis=1)
        return _bdot(xin, wband_ref[j])

    pre_next = gate_pre(0)
    for j in tiles:
        pre = pre_next
        if j + 1 < LRU_TILES:
            pre_next = gate_pre(j + 1)
        mid.step()
        rb = min(seg, ROW_BLOCK)
        for s in range(SUBLANES):
            for o in range(0, seg, rb):
                rows = slice(s * seg + o, s * seg + o + rb)
                dst = slice(s * pitch + o, s * pitch + o + rb)
                r = _sigmoid(pre[rows, :LANES] + ba_ref[:, lanes(j)])
                i = _sigmoid(pre[rows, LANES:] + bx_ref[:, lanes(j)])
                log_a = r * neg_c_sp[:, lanes(j)]
                th = jnp.tanh(log_a)
                a_s[j, dst, :] = jnp.exp(log_a)
                u_s[j, dst, :] = ((jnp.sqrt(jnp.maximum(-2.0 * th, 0.0)) * lax.rsqrt(1.0 - th))
                                  * (i * xc_s[j, rows, :]))
    mid.flush()

    def seg_rows(t):
        return pl.ds(t, SUBLANES, stride=pitch)

    if chained:
        hs = [jnp.zeros((SUBLANES, LANES), F32) for _ in tiles]
        ps = [jnp.ones((SUBLANES, LANES), F32) for _ in tiles]
        for t in range(seg):
            for j in tiles:
                a = a_s[j, seg_rows(t), :]
                hs[j] = a * hs[j] + u_s[j, seg_rows(t), :]
                ps[j] = ps[j] * a
            front.step()
        h_end = jnp.concatenate(hs, axis=1)
        p_end = jnp.concatenate(ps, axis=1)
        hc = hc_s[...]
        for s in range(SUBLANES):
            hin_s[s:s + 1, :] = hc
            hc = h_end[s:s + 1, :] + p_end[s:s + 1, :] * hc
        hc_s[...] = hc
        h_init = hin_s[...]
    else:
        h_init = h0_ref[...]

    hs = [h_init[:, lanes(j)] for j in tiles]
    for t in range(seg):
        for j in tiles:
            hs[j] = a_s[j, seg_rows(t), :] * hs[j] + u_s[j, seg_rows(t), :]
            u_s[j, seg_rows(t), :] = hs[j]
        front.step()
    front.flush()

    def gated_tile(j, gate_j):
        h = jnp.concatenate([u_s[j, s * pitch:s * pitch + seg, :] for s in range(SUBLANES)], axis=0)
        return (h * _silu(gate_j)).astype(BF16)

    if chained:
        for j in tiles:
            for s in range(SUBLANES):
                rows = slice(s * seg, (s + 1) * seg)
                gated_s[rows, lanes(j)] = (u_s[j, s * pitch:s * pitch + seg, :]
                                           * _silu(gate_s[rows, lanes(j)])).astype(BF16)
        xs2_s[...] = xs1_s[...]
        xs1_s[...] = x_in
        _conv_fill(xpad_s, pin_s, None, T=T, c=c, chained=True)
        gate_s[...] = pin_s[:, W:]

        @pl.when(step == pl.num_programs(0) - 2)
        def _():
            h_out[...] = hc_s[...]
    else:
        h_out[...] = jnp.concatenate(hs, axis=1)
        gated = jnp.concatenate([gated_tile(j, proj[:, W + j * LANES:W + (j + 1) * LANES]) for j in tiles], axis=1)
        y_ref[...] = x_in + jnp.dot(gated, wout_ref[...], preferred_element_type=F32)


def _ssd_body(*refs, T, c, chained):
    refs = list(refs)
    (x_ref, nw_ref, win_ref, wdt_ref, cw_ref, cb_ref, dtb_ref, alog_ref, de_ref, snw_ref,
     wout_ref, tri_ref, eye_ref, exp_ref) = refs[:14]
    refs = refs[14:]
    n_zx = SSD_DI + SSD_CONV_DIM
    if chained:
        cbuf_in = s0_ref = None
    else:
        cbuf_in = refs.pop(0)
        s0_ref = refs.pop(0)
    if chained:
        (y_ref, cbuf_out, sout_ref, xpad_s, xbc_s, z_s, cum_s, dt_s, ecum_s, xw_s, yacc_s, cdec_s, st_s,
         dtp_s, xs_s, pin_s) = refs
    else:
        (y_ref, cbuf_out, sout_ref, xpad_s, xbc_s, z_s, cum_s, dt_s, ecum_s, xw_s, yacc_s, cdec_s) = refs
        st_s = None
    nc = T // c
    b_off = SSD_DI
    c_off = SSD_DI + SSD_GROUPS * SSD_STATE
    n_slots = SSD_CONV_DIM // LANES + nc * SSD_GROUPS * (SSD_HPG // 2)

    if chained:
        @pl.when(pl.program_id(0) <= 1)
        def _():
            st_s[...] = jnp.zeros(st_s.shape, F32)

        @pl.when(pl.program_id(0) == 0)
        def _():
            xpad_s[...] = jnp.zeros(xpad_s.shape, F32)
            z_s[...] = jnp.zeros(z_s.shape, F32)
            dtp_s[...] = jnp.zeros(dtp_s.shape, F32)
            xs_s[...] = jnp.zeros(xs_s.shape, F32)

    x_in = x_ref[...]
    hb = _rms(x_in, nw_ref[...]).astype(BF16)
    dtp_in = jnp.dot(hb, wdt_ref[...], preferred_element_type=F32)
    if chained:
        x = xs_s[...]
        dt_pre = dtp_s[...]
        front = _Interleave([functools.partial(_proj_slice, pin_s, hb, win_ref, lo)
                             for lo in range(0, n_zx, PROJ_SLICE)], n_slots)
    else:
        x, dt_pre = x_in, dtp_in
        z_s[...] = jnp.dot(hb, win_ref[:, 0:SSD_DI], preferred_element_type=F32)
        _conv_fill(xpad_s, jnp.dot(hb, win_ref[:, SSD_DI:n_zx], preferred_element_type=F32), cbuf_in,
                   T=T, c=c, chained=False)
        front = _Interleave([], n_slots)
    _conv_apply(xpad_s, xbc_s, cw_ref, cb_ref, cbuf_out, T=T, c=c, chained=chained, act=_silu,
                after_tile=front.step)

    dt = _softplus(dt_pre + dtb_ref[...])
    a_neg = -jnp.exp(alog_ref[...])
    cum = _sel_dot(tri_ref[...], dt * a_neg)
    cum_s[...] = cum
    dt_s[...] = dt
    lasts = [cum[n * c + c - 1:n * c + c, :] for n in range(nc)]
    cl = jnp.concatenate([jnp.broadcast_to(l, (c, LANES)) for l in lasts], axis=0)
    ecum_s[...] = _expand_heads(jnp.exp(cum), exp_ref[...])
    wb_e = _expand_heads(jnp.exp(cl - cum) * dt, exp_ref[...])
    for j in range(SSD_DI // LANES):
        xw_s[:, j * LANES:(j + 1) * LANES] = xbc_s[j] * wb_e[:, j * LANES:(j + 1) * LANES]
    cl_rows = jnp.concatenate(lasts + [jnp.zeros((SUBLANES - nc, LANES), F32)] * (nc < SUBLANES), axis=0)
    cdec_s[...] = _expand_heads(jnp.exp(cl_rows), exp_ref[...])

    tril = lax.broadcasted_iota(jnp.int32, (c, c), 0) >= lax.broadcasted_iota(jnp.int32, (c, c), 1)
    low_half = lax.broadcasted_iota(jnp.int32, (c, LANES), 1) < SSD_HEADDIM

    def chunk(n, carry):
        rows = pl.ds(n * c, c)
        cum_c = cum_s[rows, :]
        cum_t = _sel_dot_nt(eye_ref[...], cum_c)
        dt_t = _sel_dot_nt(eye_ref[...], dt_s[rows, :])
        for g in range(SSD_GROUPS):
            gl = g * SSD_GW
            bm = xbc_s[b_off // LANES + g, rows, :]
            cm = xbc_s[c_off // LANES + g, rows, :]
            cb = _bdot_nt(cm, bm)
            if chained:
                st_prev = st_s[g]
            else:
                st_prev = s0_ref[n, g].T
            y_inter = _bdot(cm, st_prev) * ecum_s[rows, gl:gl + SSD_GW]
            for pr in range(SSD_HPG // 2):
                h0 = g * SSD_HPG + 2 * pr
                xl = h0 * SSD_HEADDIM
                xp = xbc_s[h0 // 2, rows, :]
                acc = y_inter[:, pr * LANES:(pr + 1) * LANES]
                for e in range(2):
                    h = h0 + e
                    seg = cum_c[:, h:h + 1] - cum_t[h:h + 1, :]
                    dec = jnp.exp(jnp.where(tril, seg, -jnp.inf))
                    wgt = cb * dec * dt_t[h:h + 1, :]
                    xm = jnp.where(low_half if e == 0 else jnp.logical_not(low_half), xp, 0.0)
                    acc = acc + _bdot(wgt, xm)
                yacc_s[rows, xl:xl + LANES] = acc
                front.step()
            ds = _bdot_tn(bm, xw_s[rows, gl:gl + SSD_GW])
            st_new = st_prev * cdec_s[pl.ds(n, 1), gl:gl + SSD_GW] + ds
            if chained:
                st_s[g] = st_new
            else:
                sout_ref[n, g] = st_new.T
        return carry

    for n in range(nc):
        chunk(n, 0)
    front.flush()

    xs = jnp.concatenate([xbc_s[j] for j in range(SSD_DI // LANES)], axis=1)
    y = yacc_s[...] + de_ref[...] * xs
    yz = _rms(y * _silu(z_s[...]), snw_ref[...]).astype(BF16)
    y_ref[...] = x + jnp.dot(yz, wout_ref[...], preferred_element_type=F32)

    if chained:
        z_s[...] = pin_s[:, 0:SSD_DI]
        _conv_fill(xpad_s, pin_s.at[:, SSD_DI:], None, T=T, c=c, chained=True)
        dtp_s[...] = dtp_in
        xs_s[...] = x_in

        @pl.when(pl.program_id(0) == pl.num_programs(0) - 1)
        def _():
            for g in range(SSD_GROUPS):
                sout_ref[g] = st_s[g].T


def _const_spec(shape):
    nd = len(shape)
    return pl.BlockSpec(shape, lambda i, _nd=nd: (0,) * _nd, pipeline_mode=pl.Buffered(1))


def _row_specs(rows, T, lag):
    n = rows // T
    lag = int(lag)
    if lag == 0:
        spec = pl.BlockSpec((T, D_MODEL), lambda i: (i, 0))
        return (n,), spec, spec
    return ((n + lag,),
            pl.BlockSpec((T, D_MODEL), lambda i: (jnp.minimum(i, n - 1), 0)),
            pl.BlockSpec((T, D_MODEL), lambda i: (jnp.maximum(i - lag, 0), 0)))


def _params():
    return pltpu.CompilerParams(dimension_semantics=("arbitrary",), vmem_limit_bytes=VMEM_LIMIT_BYTES)


def _block_tri(T, c):
    r = np.arange(T)
    return jnp.asarray((r[:, None] // c == r[None, :] // c) & (r[:, None] >= r[None, :]), BF16)


def _eye():
    return jnp.asarray(np.eye(LANES), BF16)


def _gla_layer(x, s0, p, *, T, c, chained, final_norm_w=None):
    rows = x.shape[0]
    consts = [p["norm_w"], p["w_main"], p["w_glr"], p["w_gu"], p["b_gate"], p["gnorm_w"], p["w_out"],
              _block_tri(min(T, CUMSUM_ROWS), c)]
    args = [x] + consts
    grid, x_spec, y_spec = _row_specs(rows, T, chained)
    specs = [x_spec] + [_const_spec(a.shape) for a in consts]
    if not chained:
        s_all, layer = s0
        args.append(s_all)
        specs.append(pl.BlockSpec((1,) + s_all.shape[1:], lambda i, _l=layer: (_l, 0, 0, 0, 0),
                                  pipeline_mode=pl.Buffered(1)))
    if final_norm_w is not None:
        args.append(final_norm_w)
        specs.append(_const_spec(final_norm_w.shape))
    st_shape = (GLA_HEADS, GLA_DK, GLA_DV) if chained else s0[0].shape[1:]
    scratch = [pltpu.VMEM((T, 2 * GLA_QK + 2 * GLA_V), F32), pltpu.VMEM((T, GLA_QK), F32),
               pltpu.VMEM((T, GLA_V), F32)]
    if chained:
        scratch += [pltpu.VMEM((T, LANES), F32), pltpu.VMEM((T, D_MODEL), F32),
                    pltpu.VMEM((T, 2 * GLA_QK + 2 * GLA_V), F32)]
    body = functools.partial(_gla_body, T=T, c=c, chained=chained, final_norm=final_norm_w is not None)
    return pl.pallas_call(
        body,
        grid=grid,
        in_specs=specs,
        out_specs=[y_spec, pl.BlockSpec(st_shape, lambda i, _n=len(st_shape): (0,) * _n)],
        out_shape=[jax.ShapeDtypeStruct((rows, D_MODEL), F32), jax.ShapeDtypeStruct(st_shape, F32)],
        scratch_shapes=scratch,
        compiler_params=_params(),
        name="gla_chained" if chained else "gla_batched",
    )(*args)


def _lru_layer(x, cbuf, h0, p, *, T, c, chained):
    rows = x.shape[0]
    W = LRU_WIDTH
    consts = [p["norm_w"], p["w_in"], p["conv_w"], p["conv_b"], p["w_band"], p["b_a"], p["b_x"], p["lam"], p["w_out"]]
    args = [x] + consts
    grid, x_spec, y_spec = _row_specs(rows, T, 2 if chained else 0)
    specs = [x_spec] + [_const_spec(a.shape) for a in consts]
    if chained:
        cb_shape, h_shape = (CONV_W - 1, W), (1, W)
        pad_rows = SUBLANES + T
    else:
        args += [cbuf, h0]
        specs += [_const_spec(cbuf.shape), _const_spec(h0.shape)]
        cb_shape, h_shape = cbuf.shape, h0.shape
        pad_rows = (T // c) * (SUBLANES + c)
    scan_rows = SUBLANES * _scan_pitch(T // SUBLANES)
    body = functools.partial(_lru_body, T=T, c=c, chained=chained)
    zero_map = lambda n: (lambda i: (0,) * n)
    return pl.pallas_call(
        body,
        grid=grid,
        in_specs=specs,
        out_specs=[y_spec, pl.BlockSpec(cb_shape, zero_map(len(cb_shape))),
                   pl.BlockSpec(h_shape, zero_map(len(h_shape)))],
        out_shape=[jax.ShapeDtypeStruct((rows, D_MODEL), F32), jax.ShapeDtypeStruct(cb_shape, F32),
                   jax.ShapeDtypeStruct(h_shape, F32)],
        scratch_shapes=[pltpu.VMEM((LRU_TILES, pad_rows, LANES), F32), pltpu.VMEM((LRU_TILES, T, LANES), F32)]
                       + [pltpu.VMEM((LRU_TILES, scan_rows, LANES), F32)] * 2 + [pltpu.VMEM((SUBLANES, W), F32)]
                       + [pltpu.VMEM((T, W), F32), pltpu.VMEM((T, D_MODEL), F32), pltpu.VMEM((T, D_MODEL), F32),
                          pltpu.VMEM((T, 2 * W), F32), pltpu.VMEM((T, W), BF16), pltpu.VMEM((1, W), F32)] * chained,
        compiler_params=_params(),
        name="lru_chained" if chained else "lru_batched",
    )(*args)


def _ssd_layer(x, cbuf, s0, p, *, T, c, chained):
    rows = x.shape[0]
    consts = [p["norm_w"], p["w_in"], p["w_dt"], p["conv_w"], p["conv_b"], p["dt_bias"], p["a_log"],
              p["d_exp"], p["snorm_w"], p["w_out"], _block_tri(T, c), _eye(), p["expand"]]
    args = [x] + consts
    grid, x_spec, y_spec = _row_specs(rows, T, chained)
    specs = [x_spec] + [_const_spec(a.shape) for a in consts]
    if chained:
        cb_shape = (CONV_W - 1, SSD_CONV_DIM)
        st_shape = (SSD_GROUPS, SSD_GW, SSD_STATE)
        pad_rows = SUBLANES + T
    else:
        args += [cbuf, s0]
        specs += [_const_spec(cbuf.shape), _const_spec(s0.shape)]
        cb_shape, st_shape = cbuf.shape, s0.shape
        pad_rows = (T // c) * (SUBLANES + c)
    conv_tiles = SSD_CONV_DIM // LANES
    scratch = [pltpu.VMEM((conv_tiles, pad_rows, LANES), F32), pltpu.VMEM((conv_tiles, T, LANES), F32),
               pltpu.VMEM((T, SSD_DI), F32), pltpu.VMEM((T, LANES), F32), pltpu.VMEM((T, LANES), F32),
               pltpu.VMEM((T, SSD_DI), F32), pltpu.VMEM((T, SSD_DI), F32), pltpu.VMEM((T, SSD_DI), F32),
               pltpu.VMEM((SUBLANES, SSD_DI), F32)]
    if chained:
        scratch += [pltpu.VMEM((SSD_GROUPS, SSD_STATE, SSD_GW), F32), pltpu.VMEM((T, LANES), F32),
                    pltpu.VMEM((T, D_MODEL), F32), pltpu.VMEM((T, SSD_DI + SSD_CONV_DIM), F32)]
    body = functools.partial(_ssd_body, T=T, c=c, chained=chained)
    zero_map = lambda n: (lambda i: (0,) * n)
    return pl.pallas_call(
        body,
        grid=grid,
        in_specs=specs,
        out_specs=[y_spec, pl.BlockSpec(cb_shape, zero_map(len(cb_shape))),
                   pl.BlockSpec(st_shape, zero_map(len(st_shape)))],
        out_shape=[jax.ShapeDtypeStruct((rows, D_MODEL), F32), jax.ShapeDtypeStruct(cb_shape, F32),
                   jax.ShapeDtypeStruct(st_shape, F32)],
        scratch_shapes=scratch,
        compiler_params=_params(),
        name="ssd_chained" if chained else "ssd_batched",
    )(*args)


def _row(v):
    return v.reshape(1, -1).astype(F32)


def _pad_lanes(a, width=LANES):
    return jnp.pad(a, [(0, 0)] * (a.ndim - 1) + [(0, width - a.shape[-1])])


def _band_slabs(w):
    blk = np.arange(LRU_WIDTH) // LRU_BW
    on_diag = jnp.asarray(blk[:, None] == blk[None, :], w.dtype)
    dense = jnp.tile(w.reshape(LRU_WIDTH, LRU_BW), (1, LRU_BLOCKS)) * on_diag
    slabs = []
    for j in range(LRU_TILES):
        st = min(max(j - 1, 0), LRU_TILES - LRU_BAND)
        slabs.append(dense[st * LANES:(st + LRU_BAND) * LANES, j * LANES:(j + 1) * LANES])
    return jnp.stack(slabs)


def _gla_params(norm_w, w_in, w_gate_up, b_gate, gnorm_w, w_out):
    n_main = 2 * GLA_QK + 2 * GLA_V
    return {
        "norm_w": _row(norm_w),
        "w_main": w_in.astype(BF16),
        "w_glr": _pad_lanes(w_in[:, n_main:]).astype(BF16),
        "w_gu": jnp.pad(w_gate_up, ((0, LANES - GLA_RANK), (0, 0))).astype(BF16),
        "b_gate": _row(b_gate),
        "gnorm_w": _row(gnorm_w),
        "w_out": w_out.astype(BF16),
    }


def _lru_params(norm_w, w_in, conv_w, conv_b, w_a, b_a, w_x, b_x, lam, w_out):
    return {
        "norm_w": _row(norm_w),
        "w_in": w_in.astype(BF16),
        "conv_w": conv_w.astype(F32),
        "conv_b": _row(conv_b),
        "w_band": jnp.concatenate([_band_slabs(w_a), _band_slabs(w_x)], axis=-1).astype(BF16),
        "b_a": _row(b_a),
        "b_x": _row(b_x),
        "lam": _row(lam),
        "w_out": w_out.astype(BF16),
    }


def _ssd_params(norm_w, w_in, conv_w, conv_b, dt_bias, a_log, d_skip, snorm_w, w_out):
    heads = np.arange(SSD_DI) // SSD_HEADDIM
    lane = np.arange(LANES)
    expand = jnp.asarray((lane[:, None] % SSD_HEADS == heads[None, :]) & (lane[:, None] < 3 * SSD_HEADS), BF16)
    reps = LANES // SSD_HEADS
    return {
        "norm_w": _row(norm_w),
        "w_in": w_in.astype(BF16),
        "w_dt": jnp.tile(w_in[:, SSD_DI + SSD_CONV_DIM:], (1, reps)).astype(BF16),
        "conv_w": conv_w.astype(F32),
        "conv_b": _row(conv_b),
        "dt_bias": jnp.tile(_row(dt_bias), (1, reps)),
        "a_log": jnp.tile(_row(a_log), (1, reps)),
        "d_exp": _row(jnp.repeat(d_skip, SSD_HEADDIM)),
        "snorm_w": _row(snorm_w),
        "w_out": w_out.astype(BF16),
        "expand": expand,
    }


def _trunk(x, st_gla, st_lru_conv, st_lru_h, st_ssd_conv, st_ssd_h, layers, final_norm_w, *, tiles, c, chained):
    new = {"gla": [], "lru_conv": [], "lru_h": [], "ssd_conv": [], "ssd_h": []}
    n_layers = len(layers)
    for i, (kind, j, p) in enumerate(layers):
        T = tiles[kind]
        if kind == "gla":
            fnw = final_norm_w if i == n_layers - 1 else None
            x, s = _gla_layer(x, None if chained else (st_gla, j), p, T=T, c=c, chained=chained, final_norm_w=fnw)
            new["gla"].append(s)
        elif kind == "lru":
            x, cb, h = _lru_layer(x, None if chained else st_lru_conv[j], None if chained else st_lru_h[j], p,
                                  T=T, c=c, chained=chained)
            new["lru_conv"].append(cb)
            new["lru_h"].append(h)
        else:
            s0 = None if chained else st_ssd_h[j].reshape(-1, SSD_GROUPS, SSD_GW, SSD_STATE)
            x, cb, s = _ssd_layer(x, None if chained else st_ssd_conv[j], s0, p, T=T, c=c, chained=chained)
            new["ssd_conv"].append(cb)
            new["ssd_h"].append(s)
    return x, new


def kernel(x_prompt, x_sample, state_gla, state_lru_conv, state_lru_h, state_ssd_conv, state_ssd_h, norm_w, final_norm_w, gla_w_in, gla_w_gate_up, gla_b_gate, gla_norm_w, gla_w_out, lru_w_in, lru_conv_w, lru_conv_b, lru_w_a, lru_b_a, lru_w_x, lru_b_x, lru_lambda, lru_w_out, ssd_w_in, ssd_conv_w, ssd_conv_b, ssd_dt_bias, ssd_a_log, ssd_d, ssd_norm_w, ssd_w_out):
    depth = norm_w.shape[0]
    layers = []
    for i in range(depth):
        j = i // 3
        if i % 3 == 0:
            layers.append(("gla", j, _gla_params(norm_w[i], gla_w_in[j], gla_w_gate_up[j], gla_b_gate[j],
                                                 gla_norm_w[j], gla_w_out[j])))
        elif i % 3 == 1:
            layers.append(("lru", j, _lru_params(norm_w[i], lru_w_in[j], lru_conv_w[j], lru_conv_b[j], lru_w_a[j],
                                                 lru_b_a[j], lru_w_x[j], lru_b_x[j], lru_lambda[j], lru_w_out[j])))
        else:
            layers.append(("ssd", j, _ssd_params(norm_w[i], ssd_w_in[j], ssd_conv_w[j], ssd_conv_b[j],
                                                 ssd_dt_bias[j], ssd_a_log[j], ssd_d[j], ssd_norm_w[j],
                                                 ssd_w_out[j])))
    assert layers[-1][0] == "gla", "the final RMSNorm is fused into a GLA layer"
    fnw = _row(final_norm_w)

    bp, lp, _ = x_prompt.shape
    bs, ls, _ = x_sample.shape
    p_tiles = {k: min(t, lp) for k, t in PROMPT_TILE.items()}
    assert bp == 1 and all(lp % t == 0 for t in p_tiles.values()) and bs * ls == LANES and bs == SUBLANES

    yp, new_p = _trunk(x_prompt.reshape(lp, D_MODEL), None, None, None, None, None, layers, fnw,
                       tiles=p_tiles, c=min(CHUNK, lp), chained=True)
    ys, new_s = _trunk(x_sample.reshape(bs * ls, D_MODEL), state_gla, state_lru_conv, state_lru_h,
                       state_ssd_conv, state_ssd_h, layers, fnw, tiles=dict.fromkeys(PROMPT_TILE, bs * ls),
                       c=min(CHUNK, ls), chained=False)

    def stack(xs, shape):
        return jnp.stack(xs).reshape(shape)

    n_gla, n_lru, n_ssd = len(new_p["gla"]), len(new_p["lru_h"]), len(new_p["ssd_h"])
    return (
        yp.reshape(x_prompt.shape), ys.reshape(x_sample.shape),
        stack(new_p["gla"], (n_gla, 1, GLA_HEADS, GLA_DK, GLA_DV)),
        stack(new_p["lru_conv"], (n_lru, 1, CONV_W - 1, LRU_WIDTH)),
        stack(new_p["lru_h"], (n_lru, 1, LRU_WIDTH)),
        stack(new_p["ssd_conv"], (n_ssd, 1, CONV_W - 1, SSD_CONV_DIM)),
        stack(new_p["ssd_h"], (n_ssd, 1, SSD_HEADS, SSD_HEADDIM, SSD_STATE)),
        stack(new_s["gla"], (n_gla, bs, GLA_HEADS, GLA_DK, GLA_DV)),
        stack(new_s["lru_conv"], (n_lru, bs, CONV_W - 1, LRU_WIDTH)),
        stack(new_s["lru_h"], (n_lru, bs, LRU_WIDTH)),
        stack(new_s["ssd_conv"], (n_ssd, bs, CONV_W - 1, SSD_CONV_DIM)),
        stack(new_s["ssd_h"], (n_ssd, bs, SSD_HEADS, SSD_HEADDIM, SSD_STATE)),
    )
```

```python
import functools

import numpy as np
import jax
import jax.numpy as jnp
from jax import lax
from jax.experimental import pallas as pl
from jax.experimental.pallas import tpu as pltpu

F32 = jnp.float32
BF16 = jnp.bfloat16

D_MODEL = 1024
CHUNK = 64
EPS = 1e-6
CONV_W = 4

GLA_HEADS = 4
GLA_QK = D_MODEL // 2
GLA_V = D_MODEL
GLA_DK = GLA_QK // GLA_HEADS
GLA_DV = GLA_V // GLA_HEADS
GLA_RANK = 16
GLA_TAU = 16.0
GLA_MAIN = 2 * GLA_QK + 2 * GLA_V

LRU_WIDTH = 1408
LRU_BLOCKS = 16
LRU_BW = LRU_WIDTH // LRU_BLOCKS
LRU_C = 8.0

SSD_DI = 2 * D_MODEL
SSD_HEADDIM = 64
SSD_HEADS = SSD_DI // SSD_HEADDIM
SSD_STATE = 128
SSD_GROUPS = 4
SSD_HPG = SSD_HEADS // SSD_GROUPS
SSD_CONV_DIM = SSD_DI + 2 * SSD_GROUPS * SSD_STATE
SSD_GW = SSD_HPG * SSD_HEADDIM

LANES = 128
SUBLANES = 8
PROMPT_TILE = {"gla": 512, "lru": 512, "ssd": 256}
CONV_SLOT_WEIGHT = 1
ROW_BLOCK = 32
PROJ_SLICE = 256
PROJ_ROWS = 512
CUMSUM_ROWS = 256
VMEM_LIMIT_BYTES = 56 * 1024 * 1024
LRU_TILES = LRU_WIDTH // LANES
LRU_BAND = 3


def _bdot(a, b):
    return jnp.dot(a.astype(BF16), b.astype(BF16), preferred_element_type=F32)


def _bdot_nt(a, b):
    return lax.dot_general(a.astype(BF16), b.astype(BF16), (((1,), (1,)), ((), ())),
                           preferred_element_type=F32)


def _bdot_tn(a, b):
    return lax.dot_general(a.astype(BF16), b.astype(BF16), (((0,), (0,)), ((), ())),
                           preferred_element_type=F32)


def _split3(x):
    hi = x.astype(BF16)
    r1 = x - hi.astype(F32)
    mid = r1.astype(BF16)
    lo = (r1 - mid.astype(F32)).astype(BF16)
    return hi, mid, lo


def _sel_dot(sel, x):
    hi, mid, lo = _split3(x)
    d = lambda p: jnp.dot(sel, p, preferred_element_type=F32)
    return d(hi) + d(mid) + d(lo)


def _expand_heads(x, expand):
    hi = x.astype(BF16).astype(F32)
    r1 = x - hi
    mid = r1.astype(BF16).astype(F32)
    lane = lax.broadcasted_iota(jnp.int32, x.shape, 1)
    packed = jnp.where(lane < SSD_HEADS, hi, jnp.where(lane < 2 * SSD_HEADS, mid, r1 - mid))
    return jnp.dot(packed.astype(BF16), expand, preferred_element_type=F32)


def _sel_dot_nt(sel, x):
    hi, mid, lo = _split3(x)
    d = lambda p: lax.dot_general(sel, p, (((1,), (1,)), ((), ())), preferred_element_type=F32)
    return d(hi) + d(mid) + d(lo)


def _rms(x, w):
    return x * lax.rsqrt(jnp.mean(x * x, axis=-1, keepdims=True) + EPS) * w


def _softplus(x):
    return jnp.maximum(x, 0.0) + jnp.log1p(jnp.exp(-jnp.abs(x)))


def _sigmoid(x):
    return 0.5 * jnp.tanh(0.5 * x) + 0.5


def _silu(x):
    return x * _sigmoid(x)


class _Interleave:
    def __init__(self, pieces, slots):
        self._pieces, self._slots, self._calls, self._total = list(pieces), slots, 0, None

    def take(self, k):
        for _ in range(min(k, len(self._pieces))):
            self._pieces.pop(0)()

    def step(self):
        if self._total is None:
            self._total = len(self._pieces)
        self._calls += 1
        left_after = self._total - self._calls * self._total // self._slots
        self.take(len(self._pieces) - max(left_after, 0))

    def flush(self):
        self.take(len(self._pieces))


def _proj_pieces(dst_ref, hb, w_ref, n_cols):
    rows = hb.shape[0]
    rstep = min(rows, PROJ_ROWS)

    def piece(r0, lo):
        hi = min(lo + PROJ_SLICE, n_cols)
        dst_ref[r0:r0 + rstep, lo:hi] = jnp.dot(hb[r0:r0 + rstep, :], w_ref[:, lo:hi], preferred_element_type=F32)

    return [functools.partial(piece, r0, lo) for lo in range(0, n_cols, PROJ_SLICE) for r0 in range(0, rows, rstep)]


def _scan_pitch(seg):
    return seg if (seg // SUBLANES) % 2 == 1 else seg + SUBLANES


_HDR = SUBLANES
_PAD0 = _HDR - (CONV_W - 1)


def _conv_seqs(T, c, chained):
    return [(0, 0, T)] if chained else [(b * (_HDR + c), b * c, c) for b in range(T // c)]


def _conv_fill(xpad_s, xb, cbuf_in, *, T, c, chained):
    nt = xpad_s.shape[0]
    for b, (base, r0, n) in enumerate(_conv_seqs(T, c, chained)):
        prev = None if chained else cbuf_in[b]
        for j in range(nt):
            lanes = slice(j * LANES, (j + 1) * LANES)
            if prev is not None:
                xpad_s[j, base + _PAD0:base + _HDR, :] = prev[:, lanes]
            xpad_s[j, base + _HDR:base + _HDR + n, :] = xb[r0:r0 + n, lanes]


def _conv_apply(xpad_s, xc_s, cw_ref, cb_ref, cbuf_out, *, T, c, chained, act=None, after_tile=None):
    nt = xpad_s.shape[0]
    blk = 2 * SUBLANES
    seqs = _conv_seqs(T, c, chained)
    for j in range(nt):
        lanes = slice(j * LANES, (j + 1) * LANES)
        w = [jnp.broadcast_to(cw_ref[k:k + 1, lanes], (SUBLANES, LANES)) for k in range(CONV_W)]
        bias = jnp.broadcast_to(cb_ref[:, lanes], (SUBLANES, LANES))
        for b, (base, r0, n) in enumerate(seqs):
            for i in range(n // blk):
                p0 = base + _PAD0 + i * blk
                win = [xpad_s[j, pl.ds(p0 + m, SUBLANES, stride=2), :] for m in range(CONV_W + 1)]
                for par in range(2):
                    y = bias
                    for k in range(CONV_W):
                        y = y + w[k] * win[par + k]
                    xc_s[j, pl.ds(r0 + i * blk + par, SUBLANES, stride=2), :] = y if act is None else act(y)
            tail = xpad_s[j, base + _HDR + n - (CONV_W - 1):base + _HDR + n, :]
            if chained:
                cbuf_out[:, lanes] = tail
                xpad_s[j, _PAD0:_HDR, :] = tail
            else:
                cbuf_out[b, :, lanes] = tail
        if after_tile is not None:
            after_tile()


def _gla_body(*refs, T, c, chained, final_norm):
    refs = list(refs)
    (x_ref, nw_ref, wm_ref, wglr_ref, wgu_ref, bg_ref, gnw_ref, wout_ref, tri_ref) = refs[:9]
    refs = refs[9:]
    s0_ref = None if chained else refs.pop(0)
    fnw_ref = refs.pop(0) if final_norm else None
    if chained:
        y_ref, sout_ref, proj_s, bcum_s, o_s, glr_s, xs_s, pin_s = refs
    else:
        y_ref, sout_ref, proj_s, bcum_s, o_s = refs
    nc = T // c

    if chained:
        @pl.when(pl.program_id(0) <= 1)
        def _():
            sout_ref[...] = jnp.zeros(sout_ref.shape, F32)

        @pl.when(pl.program_id(0) == 0)
        def _():
            proj_s[...] = jnp.zeros(proj_s.shape, F32)
            glr_s[...] = jnp.zeros(glr_s.shape, F32)
            xs_s[...] = jnp.zeros(xs_s.shape, F32)

    x_in = x_ref[...]
    hb = _rms(x_in, nw_ref[...]).astype(BF16)
    glr_in = jnp.dot(hb, wglr_ref[...], preferred_element_type=F32)
    if chained:
        x = xs_s[...]
        glr = glr_s[...]
        front = _Interleave(_proj_pieces(pin_s, hb, wm_ref, GLA_MAIN), nc * GLA_HEADS)
    else:
        x, glr = x_in, glr_in
        proj_s[...] = jnp.dot(hb, wm_ref[:, 0:GLA_MAIN], preferred_element_type=F32)
        front = _Interleave([], nc * GLA_HEADS)

    zg = _bdot(glr, wgu_ref[...]) + bg_ref[...]
    front.take(2)
    log_a = -_softplus(-zg) / GLA_TAU
    tb = tri_ref.shape[0]
    for b in range(T // tb):
        bcum_s[b * tb:(b + 1) * tb, :] = _sel_dot(tri_ref[...], log_a[b * tb:(b + 1) * tb, :])

    tril = lax.broadcasted_iota(jnp.int32, (c, c), 0) >= lax.broadcasted_iota(jnp.int32, (c, c), 1)
    k_off, v_off, g_off = GLA_QK, 2 * GLA_QK, 2 * GLA_QK + GLA_V

    dec_cols = []
    for hd in range(GLA_HEADS):
        kc = hd * GLA_DK
        last_rows = [bcum_s[n * c + c - 1:n * c + c, kc:kc + GLA_DK] for n in range(nc)]
        blk = jnp.concatenate(last_rows + [jnp.zeros((GLA_DK - nc, GLA_DK), F32)], axis=0)
        dec_cols.append(jnp.exp(blk.T))

    def chunk(n, carry):
        rows = pl.ds(n * c, c)
        for hd in range(GLA_HEADS):
            kc = hd * GLA_DK
            vc = hd * GLA_DV
            bc = bcum_s[rows, kc:kc + GLA_DK]
            bl = bc[c - 1:c, :]
            q = proj_s[rows, kc:kc + GLA_DK]
            k = proj_s[rows, k_off + kc:k_off + kc + GLA_DK]
            v = proj_s[rows, v_off + vc:v_off + vc + GLA_DV]
            g = proj_s[rows, g_off + vc:g_off + vc + GLA_DV]
            qd = q * (GLA_DK ** -0.5) * jnp.exp(bc)
            kd = k * jnp.exp(-bc)
            ke = k * jnp.exp(bl - bc)
            att = jnp.where(tril, _bdot_nt(qd, kd), 0.0)
            s_prev = sout_ref[hd] if chained else s0_ref[0, n, hd]
            o = _bdot(att, v) + _bdot(qd, s_prev)
            ds = _bdot_tn(ke, v)
            s_new = dec_cols[hd][:, n:n + 1] * s_prev + ds
            if chained:
                sout_ref[hd] = s_new
            else:
                sout_ref[n, hd] = s_new
            o = o * lax.rsqrt(jnp.mean(o * o, axis=-1, keepdims=True) + EPS) * gnw_ref[...]
            o_s[rows, vc:vc + GLA_DV] = (o * _silu(g)).astype(BF16)
            front.step()
        return carry

    for n in range(nc):
        chunk(n, 0)
    front.flush()
    out = x + jnp.dot(o_s[...], wout_ref[...], preferred_element_type=F32)
    if final_norm:
        out = _rms(out, fnw_ref[...])
    y_ref[...] = out
    if chained:
        proj_s[...] = pin_s[...]
        glr_s[...] = glr_in
        xs_s[...] = x_in


def _out_slice(y_ref, res_ref, lhs_ref, w_ref, lo):
    hi = min(lo + PROJ_SLICE, w_ref.shape[1])
    y_ref[:, lo:hi] = res_ref[:, lo:hi] + jnp.dot(lhs_ref[...], w_ref[:, lo:hi], preferred_element_type=F32)


def _lru_body(*refs, T, c, chained):
    refs = list(refs)
    (x_ref, nw_ref, win_ref, cw_ref, cb_ref, wband_ref, ba_ref, bx_ref, lam_ref, wout_ref) = refs[:10]
    refs = refs[10:]
    if chained:
        cbuf_in = h0_ref = None
        (y_ref, cbuf_out, h_out, xpad_s, xc_s, a_s, u_s, hin_s,
         gate_s, xs1_s, xs2_s, pin_s, gated_s, hc_s) = refs
    else:
        cbuf_in = refs.pop(0)
        h0_ref = refs.pop(0)
        y_ref, cbuf_out, h_out, xpad_s, xc_s, a_s, u_s, hin_s = refs
    W = LRU_WIDTH
    seg = T // SUBLANES
    tiles = range(LRU_TILES)
    lanes = lambda j: slice(j * LANES, (j + 1) * LANES)

    if chained:
        step = pl.program_id(0)

        @pl.when(step <= 1)
        def _():
            hc_s[...] = jnp.zeros(hc_s.shape, F32)

        @pl.when(step == 0)
        def _():
            xpad_s[...] = jnp.zeros(xpad_s.shape, F32)
            gate_s[...] = jnp.zeros(gate_s.shape, F32)
            xs1_s[...] = jnp.zeros(xs1_s.shape, F32)
            xs2_s[...] = jnp.zeros(xs2_s.shape, F32)
            gated_s[...] = jnp.zeros(gated_s.shape, BF16)

    x_in = x_ref[...]
    hb = _rms(x_in, nw_ref[...]).astype(BF16)
    if chained:
        pieces = [functools.partial(_out_slice, y_ref, xs2_s, gated_s, wout_ref, lo)
                  for lo in range(0, D_MODEL, PROJ_SLICE)]
        pieces += _proj_pieces(pin_s, hb, win_ref, 2 * W)
        n_conv, n_scan = 2, 5
        back = _Interleave(pieces[:n_conv], LRU_TILES)
        mid = _Interleave(pieces[n_conv:len(pieces) - n_scan], LRU_TILES)
        front = _Interleave(pieces[len(pieces) - n_scan:], 2 * seg)
    else:
        proj = jnp.dot(hb, win_ref[...], preferred_element_type=F32)
        _conv_fill(xpad_s, proj, cbuf_in, T=T, c=c, chained=False)
        back = mid = front = _Interleave([], 1)
    _conv_apply(xpad_s, xc_s, cw_ref, cb_ref, cbuf_out, T=T, c=c, chained=chained, after_tile=back.step)
    back.flush()

    pitch = _scan_pitch(seg)
    neg_c_sp = -LRU_C * _softplus(-lam_ref[...])
    def gate_pre(j):
        st = min(max(j - 1, 0), LRU_TILES - LRU_BAND)
        xin = jnp.concatenate([xc_s[st + m] for m in range(LRU_BAND)], axis=1)
        return _bdot(xin, wband_ref[j])

    pre_next = gate_pre(0)
    for j in tiles:
        pre = pre_next
        if j + 1 < LRU_TILES:
            pre_next = gate_pre(j + 1)
        mid.step()
        rb = min(seg, ROW_BLOCK)
        for s in range(SUBLANES):
            for o in range(0, seg, rb):
                rows = slice(s * seg + o, s * seg + o + rb)
                dst = slice(s * pitch + o, s * pitch + o + rb)
                r = _sigmoid(pre[rows, :LANES] + ba_ref[:, lanes(j)])
                i = _sigmoid(pre[rows, LANES:] + bx_ref[:, lanes(j)])
                log_a = r * neg_c_sp[:, lanes(j)]
                th = jnp.tanh(log_a)
                a_s[j, dst, :] = jnp.exp(log_a)
                u_s[j, dst, :] = ((jnp.sqrt(jnp.maximum(-2.0 * th, 0.0)) * lax.rsqrt(1.0 - th))
                                  * (i * xc_s[j, rows, :]))
    mid.flush()

    def seg_rows(t):
        return pl.ds(t, SUBLANES, stride=pitch)

    if chained:
        hs = [jnp.zeros((SUBLANES, LANES), F32) for _ in tiles]
        ps = [jnp.ones((SUBLANES, LANES), F32) for _ in tiles]
        for t in range(seg):
            for j in tiles:
                a = a_s[j, seg_rows(t), :]
                hs[j] = a * hs[j] + u_s[j, seg_rows(t), :]
                ps[j] = ps[j] * a
            front.step()
        h_end = jnp.concatenate(hs, axis=1)
        p_end = jnp.concatenate(ps, axis=1)
        hc = hc_s[...]
        for s in range(SUBLANES):
            hin_s[s:s + 1, :] = hc
            hc = h_end[s:s + 1, :] + p_end[s:s + 1, :] * hc
        hc_s[...] = hc
        h_init = hin_s[...]
    else:
        h_init = h0_ref[...]

    hs = [h_init[:, lanes(j)] for j in tiles]
    for t in range(seg):
        for j in tiles:
            hs[j] = a_s[j, seg_rows(t), :] * hs[j] + u_s[j, seg_rows(t), :]
            u_s[j, seg_rows(t), :] = hs[j]
        front.step()
    front.flush()

    def gated_tile(j, gate_j):
        h = jnp.concatenate([u_s[j, s * pitch:s * pitch + seg, :] for s in range(SUBLANES)], axis=0)
        return (h * _silu(gate_j)).astype(BF16)

    if chained:
        for j in tiles:
            for s in range(SUBLANES):
                rows = slice(s * seg, (s + 1) * seg)
                gated_s[rows, lanes(j)] = (u_s[j, s * pitch:s * pitch + seg, :]
                                           * _silu(gate_s[rows, lanes(j)])).astype(BF16)
        xs2_s[...] = xs1_s[...]
        xs1_s[...] = x_in
        _conv_fill(xpad_s, pin_s, None, T=T, c=c, chained=True)
        gate_s[...] = pin_s[:, W:]

        @pl.when(step == pl.num_programs(0) - 2)
        def _():
            h_out[...] = hc_s[...]
    else:
        h_out[...] = jnp.concatenate(hs, axis=1)
        gated = jnp.concatenate([gated_tile(j, proj[:, W + j * LANES:W + (j + 1) * LANES]) for j in tiles], axis=1)
        y_ref[...] = x_in + jnp.dot(gated, wout_ref[...], preferred_element_type=F32)


def _ssd_body(*refs, T, c, chained):
    refs = list(refs)
    (x_ref, nw_ref, win_ref, wdt_ref, cw_ref, cb_ref, dtb_ref, alog_ref, de_ref, snw_ref,
     wout_ref, tri_ref, eye_ref, exp_ref) = refs[:14]
    refs = refs[14:]
    n_zx = SSD_DI + SSD_CONV_DIM
    if chained:
        cbuf_in = s0_ref = None
    else:
        cbuf_in = refs.pop(0)
        s0_ref = refs.pop(0)
    if chained:
        (y_ref, cbuf_out, sout_ref, xpad_s, xbc_s, z_s, cum_s, dt_s, ecum_s, xw_s, yacc_s, cdec_s, st_s,
         dtp_s, xs_s, pin_s) = refs
    else:
        (y_ref, cbuf_out, sout_ref, xpad_s, xbc_s, z_s, cum_s, dt_s, ecum_s, xw_s, yacc_s, cdec_s) = refs
        st_s = None
    nc = T // c
    b_off = SSD_DI
    c_off = SSD_DI + SSD_GROUPS * SSD_STATE
    n_slots = (CONV_SLOT_WEIGHT * (SSD_CONV_DIM // LANES) + SSD_DI // LANES
               + nc * SSD_GROUPS * (SSD_HPG // 2))

    if chained:
        @pl.when(pl.program_id(0) <= 1)
        def _():
            st_s[...] = jnp.zeros(st_s.shape, F32)

        @pl.when(pl.program_id(0) == 0)
        def _():
            xpad_s[...] = jnp.zeros(xpad_s.shape, F32)
            z_s[...] = jnp.zeros(z_s.shape, F32)
            dtp_s[...] = jnp.zeros(dtp_s.shape, F32)
            xs_s[...] = jnp.zeros(xs_s.shape, F32)

    x_in = x_ref[...]
    hb = _rms(x_in, nw_ref[...]).astype(BF16)
    dtp_in = jnp.dot(hb, wdt_ref[...], preferred_element_type=F32)
    if chained:
        x = xs_s[...]
        dt_pre = dtp_s[...]
        front = _Interleave(_proj_pieces(pin_s, hb, win_ref, n_zx), n_slots)
    else:
        x, dt_pre = x_in, dtp_in
        z_s[...] = jnp.dot(hb, win_ref[:, 0:SSD_DI], preferred_element_type=F32)
        _conv_fill(xpad_s, jnp.dot(hb, win_ref[:, SSD_DI:n_zx], preferred_element_type=F32), cbuf_in,
                   T=T, c=c, chained=False)
        front = _Interleave([], n_slots)

    dt = _softplus(dt_pre + dtb_ref[...])
    a_neg = -jnp.exp(alog_ref[...])
    cum = _sel_dot(tri_ref[...], dt * a_neg)
    cum_s[...] = cum
    dt_s[...] = dt
    lasts = [cum[n * c + c - 1:n * c + c, :] for n in range(nc)]
    cl = jnp.concatenate([jnp.broadcast_to(l, (c, LANES)) for l in lasts], axis=0)
    ecum_s[...] = _expand_heads(jnp.exp(cum), exp_ref[...])
    wb_e = _expand_heads(jnp.exp(cl - cum) * dt, exp_ref[...])
    _conv_apply(xpad_s, xbc_s, cw_ref, cb_ref, cbuf_out, T=T, c=c, chained=chained, act=_silu,
                after_tile=lambda: [front.step() for _ in range(CONV_SLOT_WEIGHT)])
    for j in range(SSD_DI // LANES):
        xw_s[:, j * LANES:(j + 1) * LANES] = xbc_s[j] * wb_e[:, j * LANES:(j + 1) * LANES]
        front.step()
    cl_rows = jnp.concatenate(lasts + [jnp.zeros((SUBLANES - nc, LANES), F32)] * (nc < SUBLANES), axis=0)
    cdec_s[...] = _expand_heads(jnp.exp(cl_rows), exp_ref[...])

    tril = lax.broadcasted_iota(jnp.int32, (c, c), 0) >= lax.broadcasted_iota(jnp.int32, (c, c), 1)
    low_half = lax.broadcasted_iota(jnp.int32, (c, LANES), 1) < SSD_HEADDIM

    def chunk(n, carry):
        rows = pl.ds(n * c, c)
        cum_c = cum_s[rows, :]
        cum_t = _sel_dot_nt(eye_ref[...], cum_c)
        dt_t = _sel_dot_nt(eye_ref[...], dt_s[rows, :])
        for g in range(SSD_GROUPS):
            gl = g * SSD_GW
            bm = xbc_s[b_off // LANES + g, rows, :]
            cm = xbc_s[c_off // LANES + g, rows, :]
            cb = _bdot_nt(cm, bm)
            if chained:
                st_prev = st_s[g]
            else:
                st_prev = s0_ref[n, g].T
            y_inter = _bdot(cm, st_prev) * ecum_s[rows, gl:gl + SSD_GW]
            for pr in range(SSD_HPG // 2):
                h0 = g * SSD_HPG + 2 * pr
                xl = h0 * SSD_HEADDIM
                xp = xbc_s[h0 // 2, rows, :]
                acc = y_inter[:, pr * LANES:(pr + 1) * LANES]
                for e in range(2):
                    h = h0 + e
                    seg = cum_c[:, h:h + 1] - cum_t[h:h + 1, :]
                    dec = jnp.exp(jnp.where(tril, seg, -jnp.inf))
                    wgt = cb * dec * dt_t[h:h + 1, :]
                    xm = jnp.where(low_half if e == 0 else jnp.logical_not(low_half), xp, 0.0)
                    acc = acc + _bdot(wgt, xm)
                yacc_s[rows, xl:xl + LANES] = acc
                front.step()
            ds = _bdot_tn(bm, xw_s[rows, gl:gl + SSD_GW])
            st_new = st_prev * cdec_s[pl.ds(n, 1), gl:gl + SSD_GW] + ds
            if chained:
                st_s[g] = st_new
            else:
                sout_ref[n, g] = st_new.T
        return carry

    for n in range(nc):
        chunk(n, 0)
    front.flush()

    xs = jnp.concatenate([xbc_s[j] for j in range(SSD_DI // LANES)], axis=1)
    y = yacc_s[...] + de_ref[...] * xs
    yz = _rms(y * _silu(z_s[...]), snw_ref[...]).astype(BF16)
    y_ref[...] = x + jnp.dot(yz, wout_ref[...], preferred_element_type=F32)

    if chained:
        z_s[...] = pin_s[:, 0:SSD_DI]
        _conv_fill(xpad_s, pin_s.at[:, SSD_DI:], None, T=T, c=c, chained=True)
        dtp_s[...] = dtp_in
        xs_s[...] = x_in

        @pl.when(pl.program_id(0) == pl.num_programs(0) - 1)
        def _():
            for g in range(SSD_GROUPS):
                sout_ref[g] = st_s[g].T


def _const_spec(shape):
    nd = len(shape)
    return pl.BlockSpec(shape, lambda i, _nd=nd: (0,) * _nd, pipeline_mode=pl.Buffered(1))


def _row_specs(rows, T, lag):
    n = rows // T
    lag = int(lag)
    if lag == 0:
        spec = pl.BlockSpec((T, D_MODEL), lambda i: (i, 0))
        return (n,), spec, spec
    return ((n + lag,),
            pl.BlockSpec((T, D_MODEL), lambda i: (jnp.minimum(i, n - 1), 0)),
            pl.BlockSpec((T, D_MODEL), lambda i: (jnp.maximum(i - lag, 0), 0)))


def _params():
    return pltpu.CompilerParams(dimension_semantics=("arbitrary",), vmem_limit_bytes=VMEM_LIMIT_BYTES)


def _block_tri(T, c):
    r = np.arange(T)
    return jnp.asarray((r[:, None] // c == r[None, :] // c) & (r[:, None] >= r[None, :]), BF16)


def _eye():
    return jnp.asarray(np.eye(LANES), BF16)


def _gla_layer(x, s0, p, *, T, c, chained, final_norm_w=None):
    rows = x.shape[0]
    consts = [p["norm_w"], p["w_main"], p["w_glr"], p["w_gu"], p["b_gate"], p["gnorm_w"], p["w_out"],
              _block_tri(min(T, CUMSUM_ROWS), c)]
    args = [x] + consts
    grid, x_spec, y_spec = _row_specs(rows, T, 1 if chained else 0)
    specs = [x_spec] + [_const_spec(a.shape) for a in consts]
    if not chained:
        s_all, layer = s0
        args.append(s_all)
        specs.append(pl.BlockSpec((1,) + s_all.shape[1:], lambda i, _l=layer: (_l, 0, 0, 0, 0),
                                  pipeline_mode=pl.Buffered(1)))
    if final_norm_w is not None:
        args.append(final_norm_w)
        specs.append(_const_spec(final_norm_w.shape))
    st_shape = (GLA_HEADS, GLA_DK, GLA_DV) if chained else s0[0].shape[1:]
    scratch = [pltpu.VMEM((T, GLA_MAIN), F32), pltpu.VMEM((T, GLA_QK), F32), pltpu.VMEM((T, GLA_V), BF16)]
    if chained:
        scratch += [pltpu.VMEM((T, LANES), F32), pltpu.VMEM((T, D_MODEL), F32), pltpu.VMEM((T, GLA_MAIN), F32)]
    body = functools.partial(_gla_body, T=T, c=c, chained=chained, final_norm=final_norm_w is not None)
    return pl.pallas_call(
        body,
        grid=grid,
        in_specs=specs,
        out_specs=[y_spec, pl.BlockSpec(st_shape, lambda i, _n=len(st_shape): (0,) * _n)],
        out_shape=[jax.ShapeDtypeStruct((rows, D_MODEL), F32), jax.ShapeDtypeStruct(st_shape, F32)],
        scratch_shapes=scratch,
        compiler_params=_params(),
        name="gla_chained" if chained else "gla_batched",
    )(*args)


def _lru_layer(x, cbuf, h0, p, *, T, c, chained):
    rows = x.shape[0]
    W = LRU_WIDTH
    consts = [p["norm_w"], p["w_in"], p["conv_w"], p["conv_b"], p["w_band"], p["b_a"], p["b_x"], p["lam"], p["w_out"]]
    args = [x] + consts
    grid, x_spec, y_spec = _row_specs(rows, T, 2 if chained else 0)
    specs = [x_spec] + [_const_spec(a.shape) for a in consts]
    if chained:
        cb_shape, h_shape = (CONV_W - 1, W), (1, W)
        pad_rows = SUBLANES + T
    else:
        args += [cbuf, h0]
        specs += [_const_spec(cbuf.shape), _const_spec(h0.shape)]
        cb_shape, h_shape = cbuf.shape, h0.shape
        pad_rows = (T // c) * (SUBLANES + c)
    scan_rows = SUBLANES * _scan_pitch(T // SUBLANES)
    body = functools.partial(_lru_body, T=T, c=c, chained=chained)
    zero_map = lambda n: (lambda i: (0,) * n)
    return pl.pallas_call(
        body,
        grid=grid,
        in_specs=specs,
        out_specs=[y_spec, pl.BlockSpec(cb_shape, zero_map(len(cb_shape))),
                   pl.BlockSpec(h_shape, zero_map(len(h_shape)))],
        out_shape=[jax.ShapeDtypeStruct((rows, D_MODEL), F32), jax.ShapeDtypeStruct(cb_shape, F32),
                   jax.ShapeDtypeStruct(h_shape, F32)],
        scratch_shapes=[pltpu.VMEM((LRU_TILES, pad_rows, LANES), F32), pltpu.VMEM((LRU_TILES, T, LANES), F32)]
                       + [pltpu.VMEM((LRU_TILES, scan_rows, LANES), F32)] * 2 + [pltpu.VMEM((SUBLANES, W), F32)]
                       + [pltpu.VMEM((T, W), F32), pltpu.VMEM((T, D_MODEL), F32), pltpu.VMEM((T, D_MODEL), F32),
                          pltpu.VMEM((T, 2 * W), F32), pltpu.VMEM((T, W), BF16), pltpu.VMEM((1, W), F32)] * chained,
        compiler_params=_params(),
        name="lru_chained" if chained else "lru_batched",
    )(*args)


def _ssd_layer(x, cbuf, s0, p, *, T, c, chained):
    rows = x.shape[0]
    consts = [p["norm_w"], p["w_in"], p["w_dt"], p["conv_w"], p["conv_b"], p["dt_bias"], p["a_log"],
              p["d_exp"], p["snorm_w"], p["w_out"], _block_tri(T, c), _eye(), p["expand"]]
    args = [x] + consts
    grid, x_spec, y_spec = _row_specs(rows, T, 1 if chained else 0)
    specs = [x_spec] + [_const_spec(a.shape) for a in consts]
    if chained:
        cb_shape = (CONV_W - 1, SSD_CONV_DIM)
        st_shape = (SSD_GROUPS, SSD_GW, SSD_STATE)
        pad_rows = SUBLANES + T
    else:
        args += [cbuf, s0]
        specs += [_const_spec(cbuf.shape), _const_spec(s0.shape)]
        cb_shape, st_shape = cbuf.shape, s0.shape
        pad_rows = (T // c) * (SUBLANES + c)
    conv_tiles = SSD_CONV_DIM // LANES
    scratch = [pltpu.VMEM((conv_tiles, pad_rows, LANES), F32), pltpu.VMEM((conv_tiles, T, LANES), F32),
               pltpu.VMEM((T, SSD_DI), F32), pltpu.VMEM((T, LANES), F32), pltpu.VMEM((T, LANES), F32),
               pltpu.VMEM((T, SSD_DI), F32), pltpu.VMEM((T, SSD_DI), F32), pltpu.VMEM((T, SSD_DI), F32),
               pltpu.VMEM((SUBLANES, SSD_DI), F32)]
    if chained:
        scratch += [pltpu.VMEM((SSD_GROUPS, SSD_STATE, SSD_GW), F32), pltpu.VMEM((T, LANES), F32),
                    pltpu.VMEM((T, D_MODEL), F32), pltpu.VMEM((T, SSD_DI + SSD_CONV_DIM), F32)]
    body = functools.partial(_ssd_body, T=T, c=c, chained=chained)
    zero_map = lambda n: (lambda i: (0,) * n)
    return pl.pallas_call(
        body,
        grid=grid,
        in_specs=specs,
        out_specs=[y_spec, pl.BlockSpec(cb_shape, zero_map(len(cb_shape))),
                   pl.BlockSpec(st_shape, zero_map(len(st_shape)))],
        out_shape=[jax.ShapeDtypeStruct((rows, D_MODEL), F32), jax.ShapeDtypeStruct(cb_shape, F32),
                   jax.ShapeDtypeStruct(st_shape, F32)],
        scratch_shapes=scratch,
        compiler_params=_params(),
        name="ssd_chained" if chained else "ssd_batched",
    )(*args)


def _row(v):
    return v.reshape(1, -1).astype(F32)


def _pad_lanes(a, width=LANES):
    return jnp.pad(a, [(0, 0)] * (a.ndim - 1) + [(0, width - a.shape[-1])])


def _band_slabs(w):
    blk = np.arange(LRU_WIDTH) // LRU_BW
    on_diag = jnp.asarray(blk[:, None] == blk[None, :], w.dtype)
    dense = jnp.tile(w.reshape(LRU_WIDTH, LRU_BW), (1, LRU_BLOCKS)) * on_diag
    slabs = []
    for j in range(LRU_TILES):
        st = min(max(j - 1, 0), LRU_TILES - LRU_BAND)
        slabs.append(dense[st * LANES:(st + LRU_BAND) * LANES, j * LANES:(j + 1) * LANES])
    return jnp.stack(slabs)


def _gla_params(norm_w, w_in, w_gate_up, b_gate, gnorm_w, w_out):
    n_main = 2 * GLA_QK + 2 * GLA_V
    return {
        "norm_w": _row(norm_w),
        "w_main": w_in.astype(BF16),
        "w_glr": _pad_lanes(w_in[:, n_main:]).astype(BF16),
        "w_gu": jnp.pad(w_gate_up, ((0, LANES - GLA_RANK), (0, 0))).astype(BF16),
        "b_gate": _row(b_gate),
        "gnorm_w": _row(gnorm_w),
        "w_out": w_out.astype(BF16),
    }


def _lru_params(norm_w, w_in, conv_w, conv_b, w_a, b_a, w_x, b_x, lam, w_out):
    return {
        "norm_w": _row(norm_w),
        "w_in": w_in.astype(BF16),
        "conv_w": conv_w.astype(F32),
        "conv_b": _row(conv_b),
        "w_band": jnp.concatenate([_band_slabs(w_a), _band_slabs(w_x)], axis=-1).astype(BF16),
        "b_a": _row(b_a),
        "b_x": _row(b_x),
        "lam": _row(lam),
        "w_out": w_out.astype(BF16),
    }


def _ssd_params(norm_w, w_in, conv_w, conv_b, dt_bias, a_log, d_skip, snorm_w, w_out):
    heads = np.arange(SSD_DI) // SSD_HEADDIM
    lane = np.arange(LANES)
    expand = jnp.asarray((lane[:, None] % SSD_HEADS == heads[None, :]) & (lane[:, None] < 3 * SSD_HEADS), BF16)
    reps = LANES // SSD_HEADS
    return {
        "norm_w": _row(norm_w),
        "w_in": w_in.astype(BF16),
        "w_dt": jnp.tile(w_in[:, SSD_DI + SSD_CONV_DIM:], (1, reps)).astype(BF16),
        "conv_w": conv_w.astype(F32),
        "conv_b": _row(conv_b),
        "dt_bias": jnp.tile(_row(dt_bias), (1, reps)),
        "a_log": jnp.tile(_row(a_log), (1, reps)),
        "d_exp": _row(jnp.repeat(d_skip, SSD_HEADDIM)),
        "snorm_w": _row(snorm_w),
        "w_out": w_out.astype(BF16),
        "expand": expand,
    }


def _trunk(x, st_gla, st_lru_conv, st_lru_h, st_ssd_conv, st_ssd_h, layers, final_norm_w, *, tiles, c, chained):
    new = {"gla": [], "lru_conv": [], "lru_h": [], "ssd_conv": [], "ssd_h": []}
    n_layers = len(layers)
    for i, (kind, j, p) in enumerate(layers):
        T = tiles[kind]
        if kind == "gla":
            fnw = final_norm_w if i == n_layers - 1 else None
            x, s = _gla_layer(x, None if chained else (st_gla, j), p, T=T, c=c, chained=chained, final_norm_w=fnw)
            new["gla"].append(s)
        elif kind == "lru":
            x, cb, h = _lru_layer(x, None if chained else st_lru_conv[j], None if chained else st_lru_h[j], p,
                                  T=T, c=c, chained=chained)
            new["lru_conv"].append(cb)
            new["lru_h"].append(h)
        else:
            s0 = None if chained else st_ssd_h[j].reshape(-1, SSD_GROUPS, SSD_GW, SSD_STATE)
            x, cb, s = _ssd_layer(x, None if chained else st_ssd_conv[j], s0, p, T=T, c=c, chained=chained)
            new["ssd_conv"].append(cb)
            new["ssd_h"].append(s)
    return x, new


def kernel(x_prompt, x_sample, state_gla, state_lru_conv, state_lru_h, state_ssd_conv, state_ssd_h, norm_w, final_norm_w, gla_w_in, gla_w_gate_up, gla_b_gate, gla_norm_w, gla_w_out, lru_w_in, lru_conv_w, lru_conv_b, lru_w_a, lru_b_a, lru_w_x, lru_b_x, lru_lambda, lru_w_out, ssd_w_in, ssd_conv_w, ssd_conv_b, ssd_dt_bias, ssd_a_log, ssd_d, ssd_norm_w, ssd_w_out):
    depth = norm_w.shape[0]
    layers = []
    for i in range(depth):
        j = i // 3
        if i % 3 == 0:
            layers.append(("gla", j, _gla_params(norm_w[i], gla_w_in[j], gla_w_gate_up[j], gla_b_gate[j],
                                                 gla_norm_w[j], gla_w_out[j])))
        elif i % 3 == 1:
            layers.append(("lru", j, _lru_params(norm_w[i], lru_w_in[j], lru_conv_w[j], lru_conv_b[j], lru_w_a[j],
                                                 lru_b_a[j], lru_w_x[j], lru_b_x[j], lru_lambda[j], lru_w_out[j])))
        else:
            layers.append(("ssd", j, _ssd_params(norm_w[i], ssd_w_in[j], ssd_conv_w[j], ssd_conv_b[j],
                                                 ssd_dt_bias[j], ssd_a_log[j], ssd_d[j], ssd_norm_w[j],
                                                 ssd_w_out[j])))
    assert layers[-1][0] == "gla", "the final RMSNorm is fused into a GLA layer"
    fnw = _row(final_norm_w)

    bp, lp, _ = x_prompt.shape
    bs, ls, _ = x_sample.shape
    p_tiles = {k: min(t, lp) for k, t in PROMPT_TILE.items()}
    assert bp == 1 and all(lp % t == 0 for t in p_tiles.values()) and bs * ls == LANES and bs == SUBLANES

    yp, new_p = _trunk(x_prompt.reshape(lp, D_MODEL), None, None, None, None, None, layers, fnw,
                       tiles=p_tiles, c=min(CHUNK, lp), chained=True)
    ys, new_s = _trunk(x_sample.reshape(bs * ls, D_MODEL), state_gla, state_lru_conv, state_lru_h,
                       state_ssd_conv, state_ssd_h, layers, fnw, tiles=dict.fromkeys(PROMPT_TILE, bs * ls),
                       c=min(CHUNK, ls), chained=False)

    def stack(xs, shape):
        return jnp.stack(xs).reshape(shape)

    n_gla, n_lru, n_ssd = len(new_p["gla"]), len(new_p["lru_h"]), len(new_p["ssd_h"])
    return (
        yp.reshape(x_prompt.shape), ys.reshape(x_sample.shape),
        stack(new_p["gla"], (n_gla, 1, GLA_HEADS, GLA_DK, GLA_DV)),
        stack(new_p["lru_conv"], (n_lru, 1, CONV_W - 1, LRU_WIDTH)),
        stack(new_p["lru_h"], (n_lru, 1, LRU_WIDTH)),
        stack(new_p["ssd_conv"], (n_ssd, 1, CONV_W - 1, SSD_CONV_DIM)),
        stack(new_p["ssd_h"], (n_ssd, 1, SSD_HEADS, SSD_HEADDIM, SSD_STATE)),
        stack(new_s["gla"], (n_gla, bs, GLA_HEADS, GLA_DK, GLA_DV)),
        stack(new_s["lru_conv"], (n_lru, bs, CONV_W - 1, LRU_WIDTH)),
        stack(new_s["lru_h"], (n_lru, bs, LRU_WIDTH)),
        stack(new_s["ssd_conv"], (n_ssd, bs, CONV_W - 1, SSD_CONV_DIM)),
        stack(new_s["ssd_h"], (n_ssd, bs, SSD_HEADS, SSD_HEADDIM, SSD_STATE)),
    )
```

```python
import functools

import numpy as np
import jax
import jax.numpy as jnp
from jax import lax
from jax.experimental import pallas as pl
from jax.experimental.pallas import tpu as pltpu

F32 = jnp.float32
BF16 = jnp.bfloat16

D_MODEL = 1024
CHUNK = 64
EPS = 1e-6
CONV_W = 4

GLA_HEADS = 4
GLA_QK = D_MODEL // 2
GLA_V = D_MODEL
GLA_DK = GLA_QK // GLA_HEADS
GLA_DV = GLA_V // GLA_HEADS
GLA_RANK = 16
GLA_TAU = 16.0
GLA_MAIN = 2 * GLA_QK + 2 * GLA_V

LRU_WIDTH = 1408
LRU_BLOCKS = 16
LRU_BW = LRU_WIDTH // LRU_BLOCKS
LRU_C = 8.0

SSD_DI = 2 * D_MODEL
SSD_HEADDIM = 64
SSD_HEADS = SSD_DI // SSD_HEADDIM
SSD_STATE = 128
SSD_GROUPS = 4
SSD_HPG = SSD_HEADS // SSD_GROUPS
SSD_CONV_DIM = SSD_DI + 2 * SSD_GROUPS * SSD_STATE
SSD_GW = SSD_HPG * SSD_HEADDIM

LANES = 128
SUBLANES = 8
PROMPT_TILE = {"gla": 512, "lru": 512, "ssd": 256}
CONV_SLOT_WEIGHT = 1
ROW_BLOCK = 32
PROJ_SLICE = 256
PROJ_ROWS = 512
CUMSUM_ROWS = 256
VMEM_LIMIT_BYTES = 56 * 1024 * 1024
LRU_TILES = LRU_WIDTH // LANES
LRU_BAND = 3


def _bdot(a, b):
    return jnp.dot(a.astype(BF16), b.astype(BF16), preferred_element_type=F32)


def _bdot_nt(a, b):
    return lax.dot_general(a.astype(BF16), b.astype(BF16), (((1,), (1,)), ((), ())),
                           preferred_element_type=F32)


def _bdot_tn(a, b):
    return lax.dot_general(a.astype(BF16), b.astype(BF16), (((0,), (0,)), ((), ())),
                           preferred_element_type=F32)


def _split3(x):
    hi = x.astype(BF16)
    r1 = x - hi.astype(F32)
    mid = r1.astype(BF16)
    lo = (r1 - mid.astype(F32)).astype(BF16)
    return hi, mid, lo


def _sel_dot(sel, x):
    hi, mid, lo = _split3(x)
    d = lambda p: jnp.dot(sel, p, preferred_element_type=F32)
    return d(hi) + d(mid) + d(lo)


def _expand_heads(x, expand):
    hi = x.astype(BF16).astype(F32)
    r1 = x - hi
    mid = r1.astype(BF16).astype(F32)
    lane = lax.broadcasted_iota(jnp.int32, x.shape, 1)
    packed = jnp.where(lane < SSD_HEADS, hi, jnp.where(lane < 2 * SSD_HEADS, mid, r1 - mid))
    return jnp.dot(packed.astype(BF16), expand, preferred_element_type=F32)


def _sel_dot_nt(sel, x):
    hi, mid, lo = _split3(x)
    d = lambda p: lax.dot_general(sel, p, (((1,), (1,)), ((), ())), preferred_element_type=F32)
    return d(hi) + d(mid) + d(lo)


def _rms(x, w):
    return x * lax.rsqrt(jnp.mean(x * x, axis=-1, keepdims=True) + EPS) * w


def _softplus(x):
    return jnp.maximum(x, 0.0) + jnp.log1p(jnp.exp(-jnp.abs(x)))


def _sigmoid(x):
    return 0.5 * jnp.tanh(0.5 * x) + 0.5


def _silu(x):
    return x * _sigmoid(x)


class _Interleave:
    def __init__(self, pieces, slots):
        self._pieces, self._slots, self._calls, self._total = list(pieces), slots, 0, None

    def take(self, k):
        for _ in range(min(k, len(self._pieces))):
            self._pieces.pop(0)()

    def step(self):
        if self._total is None:
            self._total = len(self._pieces)
        self._calls += 1
        left_after = self._total - self._calls * self._total // self._slots
        self.take(len(self._pieces) - max(left_after, 0))

    def flush(self):
        self.take(len(self._pieces))


def _proj_pieces(dst_ref, hb, w_ref, n_cols):
    rows = hb.shape[0]
    rstep = min(rows, PROJ_ROWS)

    def piece(r0, lo):
        hi = min(lo + PROJ_SLICE, n_cols)
        dst_ref[r0:r0 + rstep, lo:hi] = jnp.dot(hb[r0:r0 + rstep, :], w_ref[:, lo:hi], preferred_element_type=F32)

    return [functools.partial(piece, r0, lo) for lo in range(0, n_cols, PROJ_SLICE) for r0 in range(0, rows, rstep)]


def _scan_pitch(seg):
    return seg if (seg // SUBLANES) % 2 == 1 else seg + SUBLANES


_HDR = SUBLANES
_PAD0 = _HDR - (CONV_W - 1)


def _conv_seqs(T, c, chained):
    return [(0, 0, T)] if chained else [(b * (_HDR + c), b * c, c) for b in range(T // c)]


def _conv_fill(xpad_s, xb, cbuf_in, *, T, c, chained):
    nt = xpad_s.shape[0]
    for b, (base, r0, n) in enumerate(_conv_seqs(T, c, chained)):
        prev = None if chained else cbuf_in[b]
        for j in range(nt):
            lanes = slice(j * LANES, (j + 1) * LANES)
            if prev is not None:
                xpad_s[j, base + _PAD0:base + _HDR, :] = prev[:, lanes]
            xpad_s[j, base + _HDR:base + _HDR + n, :] = xb[r0:r0 + n, lanes]


def _conv_apply(xpad_s, xc_s, cw_ref, cb_ref, cbuf_out, *, T, c, chained, act=None, after_tile=None):
    nt = xpad_s.shape[0]
    blk = 2 * SUBLANES
    seqs = _conv_seqs(T, c, chained)
    for j in range(nt):
        lanes = slice(j * LANES, (j + 1) * LANES)
        w = [jnp.broadcast_to(cw_ref[k:k + 1, lanes], (SUBLANES, LANES)) for k in range(CONV_W)]
        bias = jnp.broadcast_to(cb_ref[:, lanes], (SUBLANES, LANES))
        for b, (base, r0, n) in enumerate(seqs):
            for i in range(n // blk):
                p0 = base + _PAD0 + i * blk
                win = [xpad_s[j, pl.ds(p0 + m, SUBLANES, stride=2), :] for m in range(CONV_W + 1)]
                for par in range(2):
                    y = bias
                    for k in range(CONV_W):
                        y = y + w[k] * win[par + k]
                    xc_s[j, pl.ds(r0 + i * blk + par, SUBLANES, stride=2), :] = y if act is None else act(y)
            tail = xpad_s[j, base + _HDR + n - (CONV_W - 1):base + _HDR + n, :]
            if chained:
                cbuf_out[:, lanes] = tail
                xpad_s[j, _PAD0:_HDR, :] = tail
            else:
                cbuf_out[b, :, lanes] = tail
        if after_tile is not None:
            after_tile()


def _gla_body(*refs, T, c, chained, final_norm):
    refs = list(refs)
    (x_ref, nw_ref, wm_ref, wglr_ref, wgu_ref, bg_ref, gnw_ref, wout_ref, tri_ref) = refs[:9]
    refs = refs[9:]
    s0_ref = None if chained else refs.pop(0)
    fnw_ref = refs.pop(0) if final_norm else None
    if chained:
        y_ref, sout_ref, proj_s, bcum_s, o_s, glr_s, xs_s, pin_s = refs
    else:
        y_ref, sout_ref, proj_s, bcum_s, o_s = refs
    nc = T // c

    if chained:
        @pl.when(pl.program_id(0) <= 1)
        def _():
            sout_ref[...] = jnp.zeros(sout_ref.shape, F32)

        @pl.when(pl.program_id(0) == 0)
        def _():
            proj_s[...] = jnp.zeros(proj_s.shape, F32)
            glr_s[...] = jnp.zeros(glr_s.shape, F32)
            xs_s[...] = jnp.zeros(xs_s.shape, F32)

    x_in = x_ref[...]
    hb = _rms(x_in, nw_ref[...]).astype(BF16)
    glr_in = jnp.dot(hb, wglr_ref[...], preferred_element_type=F32)
    if chained:
        x = xs_s[...]
        glr = glr_s[...]
        front = _Interleave(_proj_pieces(pin_s, hb, wm_ref, GLA_MAIN), nc * GLA_HEADS)
    else:
        x, glr = x_in, glr_in
        proj_s[...] = jnp.dot(hb, wm_ref[:, 0:GLA_MAIN], preferred_element_type=F32)
        front = _Interleave([], nc * GLA_HEADS)

    zg = _bdot(glr, wgu_ref[...]) + bg_ref[...]
    front.take(2)
    log_a = -_softplus(-zg) / GLA_TAU
    tb = tri_ref.shape[0]
    for b in range(T // tb):
        bcum_s[b * tb:(b + 1) * tb, :] = _sel_dot(tri_ref[...], log_a[b * tb:(b + 1) * tb, :])

    tril = lax.broadcasted_iota(jnp.int32, (c, c), 0) >= lax.broadcasted_iota(jnp.int32, (c, c), 1)
    k_off, v_off, g_off = GLA_QK, 2 * GLA_QK, 2 * GLA_QK + GLA_V

    dec_cols = []
    for hd in range(GLA_HEADS):
        kc = hd * GLA_DK
        last_rows = [bcum_s[n * c + c - 1:n * c + c, kc:kc + GLA_DK] for n in range(nc)]
        blk = jnp.concatenate(last_rows + [jnp.zeros((GLA_DK - nc, GLA_DK), F32)], axis=0)
        dec_cols.append(jnp.exp(blk.T))

    def state_free(n, hd):
        rows = pl.ds(n * c, c)
        kc = hd * GLA_DK
        vc = hd * GLA_DV
        bc = bcum_s[rows, kc:kc + GLA_DK]
        bl = bc[c - 1:c, :]
        q = proj_s[rows, kc:kc + GLA_DK]
        k = proj_s[rows, k_off + kc:k_off + kc + GLA_DK]
        v = proj_s[rows, v_off + vc:v_off + vc + GLA_DV].astype(BF16)
        qd = (q * (GLA_DK ** -0.5) * jnp.exp(bc)).astype(BF16)
        kd = k * jnp.exp(-bc)
        ke = k * jnp.exp(bl - bc)
        att = jnp.where(tril, _bdot_nt(qd, kd), 0.0).astype(BF16)
        return qd, v, att, _bdot_tn(ke, v)

    def state_step(n, hd, qd, v, att, ds):
        rows = pl.ds(n * c, c)
        vc = hd * GLA_DV
        g = proj_s[rows, g_off + vc:g_off + vc + GLA_DV]
        s_prev = sout_ref[hd] if chained else s0_ref[0, n, hd]
        o = _bdot(att, v) + _bdot(qd, s_prev)
        s_new = dec_cols[hd][:, n:n + 1] * s_prev + ds
        if chained:
            sout_ref[hd] = s_new
        else:
            sout_ref[n, hd] = s_new
        o = o * lax.rsqrt(jnp.mean(o * o, axis=-1, keepdims=True) + EPS) * gnw_ref[...]
        o_s[rows, vc:vc + GLA_DV] = (o * _silu(g)).astype(BF16)

    order = [(n, hd) for n in range(nc) for hd in range(GLA_HEADS)]
    ahead = state_free(*order[0])
    for i, (n, hd) in enumerate(order):
        cur = ahead
        if i + 1 < len(order):
            ahead = state_free(*order[i + 1])
        front.step()
        state_step(n, hd, *cur)
    front.flush()
    out = x + jnp.dot(o_s[...], wout_ref[...], preferred_element_type=F32)
    if final_norm:
        out = _rms(out, fnw_ref[...])
    y_ref[...] = out
    if chained:
        proj_s[...] = pin_s[...]
        glr_s[...] = glr_in
        xs_s[...] = x_in


def _out_slice(y_ref, res_ref, lhs_ref, w_ref, lo):
    hi = min(lo + PROJ_SLICE, w_ref.shape[1])
    y_ref[:, lo:hi] = res_ref[:, lo:hi] + jnp.dot(lhs_ref[...], w_ref[:, lo:hi], preferred_element_type=F32)


def _lru_body(*refs, T, c, chained):
    refs = list(refs)
    (x_ref, nw_ref, win_ref, cw_ref, cb_ref, wband_ref, ba_ref, bx_ref, lam_ref, wout_ref) = refs[:10]
    refs = refs[10:]
    if chained:
        cbuf_in = h0_ref = None
        (y_ref, cbuf_out, h_out, xpad_s, xc_s, a_s, u_s, hin_s,
         gate_s, xs1_s, xs2_s, pin_s, gated_s, hc_s) = refs
    else:
        cbuf_in = refs.pop(0)
        h0_ref = refs.pop(0)
        y_ref, cbuf_out, h_out, xpad_s, xc_s, a_s, u_s, hin_s = refs
    W = LRU_WIDTH
    seg = T // SUBLANES
    tiles = range(LRU_TILES)
    lanes = lambda j: slice(j * LANES, (j + 1) * LANES)

    if chained:
        step = pl.program_id(0)

        @pl.when(step <= 1)
        def _():
            hc_s[...] = jnp.zeros(hc_s.shape, F32)

        @pl.when(step == 0)
        def _():
            xpad_s[...] = jnp.zeros(xpad_s.shape, F32)
            gate_s[...] = jnp.zeros(gate_s.shape, F32)
            xs1_s[...] = jnp.zeros(xs1_s.shape, F32)
            xs2_s[...] = jnp.zeros(xs2_s.shape, F32)
            gated_s[...] = jnp.zeros(gated_s.shape, BF16)

    x_in = x_ref[...]
    hb = _rms(x_in, nw_ref[...]).astype(BF16)
    if chained:
        pieces = [functools.partial(_out_slice, y_ref, xs2_s, gated_s, wout_ref, lo)
                  for lo in range(0, D_MODEL, PROJ_SLICE)]
        pieces += _proj_pieces(pin_s, hb, win_ref, 2 * W)
        n_conv, n_scan = 2, 5
        back = _Interleave(pieces[:n_conv], LRU_TILES)
        mid = _Interleave(pieces[n_conv:len(pieces) - n_scan], LRU_TILES)
        front = _Interleave(pieces[len(pieces) - n_scan:], 2 * seg)
    else:
        proj = jnp.dot(hb, win_ref[...], preferred_element_type=F32)
        _conv_fill(xpad_s, proj, cbuf_in, T=T, c=c, chained=False)
        back = mid = front = _Interleave([], 1)
    _conv_apply(xpad_s, xc_s, cw_ref, cb_ref, cbuf_out, T=T, c=c, chained=chained, after_tile=back.step)
    back.flush()

    pitch = _scan_pitch(seg)
    neg_c_sp = -LRU_C * _softplus(-lam_ref[...])
    def gate_pre(j):
        st = min(max(j - 1, 0), LRU_TILES - LRU_BAND)
        xin = jnp.concatenate([xc_s[st + m] for m in range(LRU_BAND)], axis=1)
        return _bdot(xin, wband_ref[j])

    pre_next = gate_pre(0)
    for j in tiles:
        pre = pre_next
        if j + 1 < LRU_TILES:
            pre_next = gate_pre(j + 1)
        mid.step()
        rb = min(seg, ROW_BLOCK)
        for s in range(SUBLANES):
            for o in range(0, seg, rb):
                rows = slice(s * seg + o, s * seg + o + rb)
                dst = slice(s * pitch + o, s * pitch + o + rb)
                r = _sigmoid(pre[rows, :LANES] + ba_ref[:, lanes(j)])
                i = _sigmoid(pre[rows, LANES:] + bx_ref[:, lanes(j)])
                log_a = r * neg_c_sp[:, lanes(j)]
                th = jnp.tanh(log_a)
                a_s[j, dst, :] = jnp.exp(log_a)
                u_s[j, dst, :] = ((jnp.sqrt(jnp.maximum(-2.0 * th, 0.0)) * lax.rsqrt(1.0 - th))
                                  * (i * xc_s[j, rows, :]))
    mid.flush()

    def seg_rows(t):
        return pl.ds(t, SUBLANES, stride=pitch)

    if chained:
        hs = [jnp.zeros((SUBLANES, LANES), F32) for _ in tiles]
        ps = [jnp.ones((SUBLANES, LANES), F32) for _ in tiles]
        for t in range(seg):
            for j in tiles:
                a = a_s[j, seg_rows(t), :]
                hs[j] = a * hs[j] + u_s[j, seg_rows(t), :]
                ps[j] = ps[j] * a
            front.step()
        h_end = jnp.concatenate(hs, axis=1)
        p_end = jnp.concatenate(ps, axis=1)
        hc = hc_s[...]
        for s in range(SUBLANES):
            hin_s[s:s + 1, :] = hc
            hc = h_end[s:s + 1, :] + p_end[s:s + 1, :] * hc
        hc_s[...] = hc
        h_init = hin_s[...]
    else:
        h_init = h0_ref[...]

    hs = [h_init[:, lanes(j)] for j in tiles]
    for t in range(seg):
        for j in tiles:
            hs[j] = a_s[j, seg_rows(t), :] * hs[j] + u_s[j, seg_rows(t), :]
            u_s[j, seg_rows(t), :] = hs[j]
        front.step()
    front.flush()

    def gated_tile(j, gate_j):
        h = jnp.concatenate([u_s[j, s * pitch:s * pitch + seg, :] for s in range(SUBLANES)], axis=0)
        return (h * _silu(gate_j)).astype(BF16)

    if chained:
        for j in tiles:
            for s in range(SUBLANES):
                rows = slice(s * seg, (s + 1) * seg)
                gated_s[rows, lanes(j)] = (u_s[j, s * pitch:s * pitch + seg, :]
                                           * _silu(gate_s[rows, lanes(j)])).astype(BF16)
        xs2_s[...] = xs1_s[...]
        xs1_s[...] = x_in
        _conv_fill(xpad_s, pin_s, None, T=T, c=c, chained=True)
        gate_s[...] = pin_s[:, W:]

        @pl.when(step == pl.num_programs(0) - 2)
        def _():
            h_out[...] = hc_s[...]
    else:
        h_out[...] = jnp.concatenate(hs, axis=1)
        gated = jnp.concatenate([gated_tile(j, proj[:, W + j * LANES:W + (j + 1) * LANES]) for j in tiles], axis=1)
        y_ref[...] = x_in + jnp.dot(gated, wout_ref[...], preferred_element_type=F32)


def _ssd_body(*refs, T, c, chained):
    refs = list(refs)
    (x_ref, nw_ref, win_ref, wdt_ref, cw_ref, cb_ref, dtb_ref, alog_ref, de_ref, snw_ref,
     wout_ref, tri_ref, eye_ref, exp_ref) = refs[:14]
    refs = refs[14:]
    n_zx = SSD_DI + SSD_CONV_DIM
    if chained:
        cbuf_in = s0_ref = None
    else:
        cbuf_in = refs.pop(0)
        s0_ref = refs.pop(0)
    if chained:
        (y_ref, cbuf_out, sout_ref, xpad_s, xbc_s, z_s, cum_s, dt_s, ecum_s, xw_s, yacc_s, cdec_s, st_s,
         dtp_s, xs_s, pin_s) = refs
    else:
        (y_ref, cbuf_out, sout_ref, xpad_s, xbc_s, z_s, cum_s, dt_s, ecum_s, xw_s, yacc_s, cdec_s) = refs
        st_s = None
    nc = T // c
    b_off = SSD_DI
    c_off = SSD_DI + SSD_GROUPS * SSD_STATE
    n_slots = (CONV_SLOT_WEIGHT * (SSD_CONV_DIM // LANES) + SSD_DI // LANES
               + nc * SSD_GROUPS * (SSD_HPG // 2))

    if chained:
        @pl.when(pl.program_id(0) <= 1)
        def _():
            st_s[...] = jnp.zeros(st_s.shape, F32)

        @pl.when(pl.program_id(0) == 0)
        def _():
            xpad_s[...] = jnp.zeros(xpad_s.shape, F32)
            z_s[...] = jnp.zeros(z_s.shape, F32)
            dtp_s[...] = jnp.zeros(dtp_s.shape, F32)
            xs_s[...] = jnp.zeros(xs_s.shape, F32)

    x_in = x_ref[...]
    hb = _rms(x_in, nw_ref[...]).astype(BF16)
    dtp_in = jnp.dot(hb, wdt_ref[...], preferred_element_type=F32)
    if chained:
        x = xs_s[...]
        dt_pre = dtp_s[...]
        front = _Interleave(_proj_pieces(pin_s, hb, win_ref, n_zx), n_slots)
    else:
        x, dt_pre = x_in, dtp_in
        z_s[...] = jnp.dot(hb, win_ref[:, 0:SSD_DI], preferred_element_type=F32)
        _conv_fill(xpad_s, jnp.dot(hb, win_ref[:, SSD_DI:n_zx], preferred_element_type=F32), cbuf_in,
                   T=T, c=c, chained=False)
        front = _Interleave([], n_slots)

    dt = _softplus(dt_pre + dtb_ref[...])
    a_neg = -jnp.exp(alog_ref[...])
    cum = _sel_dot(tri_ref[...], dt * a_neg)
    cum_s[...] = cum
    dt_s[...] = dt
    lasts = [cum[n * c + c - 1:n * c + c, :] for n in range(nc)]
    cl = jnp.concatenate([jnp.broadcast_to(l, (c, LANES)) for l in lasts], axis=0)
    ecum_s[...] = _expand_heads(jnp.exp(cum), exp_ref[...])
    wb_e = _expand_heads(jnp.exp(cl - cum) * dt, exp_ref[...])
    _conv_apply(xpad_s, xbc_s, cw_ref, cb_ref, cbuf_out, T=T, c=c, chained=chained, act=_silu,
                after_tile=lambda: [front.step() for _ in range(CONV_SLOT_WEIGHT)])
    for j in range(SSD_DI // LANES):
        xw_s[:, j * LANES:(j + 1) * LANES] = xbc_s[j] * wb_e[:, j * LANES:(j + 1) * LANES]
        front.step()
    cl_rows = jnp.concatenate(lasts + [jnp.zeros((SUBLANES - nc, LANES), F32)] * (nc < SUBLANES), axis=0)
    cdec_s[...] = _expand_heads(jnp.exp(cl_rows), exp_ref[...])

    tril = lax.broadcasted_iota(jnp.int32, (c, c), 0) >= lax.broadcasted_iota(jnp.int32, (c, c), 1)
    low_half = lax.broadcasted_iota(jnp.int32, (c, LANES), 1) < SSD_HEADDIM

    def chunk_rows_t(n):
        rows = pl.ds(n * c, c)
        return _sel_dot_nt(eye_ref[...], cum_s[rows, :]), _sel_dot_nt(eye_ref[...], dt_s[rows, :])

    def state_free(n, g):
        rows = pl.ds(n * c, c)
        bm = xbc_s[b_off // LANES + g, rows, :].astype(BF16)
        cm = xbc_s[c_off // LANES + g, rows, :].astype(BF16)
        cb = _bdot_nt(cm, bm)
        ds = _bdot_tn(bm, xw_s[rows, g * SSD_GW:(g + 1) * SSD_GW])
        return cm, cb, ds

    def group_step(n, g, cum_t, dt_t, cm, cb, ds):
        rows = pl.ds(n * c, c)
        gl = g * SSD_GW
        cum_c = cum_s[rows, :]
        if chained:
            st_prev = st_s[g]
        else:
            st_prev = s0_ref[n, g].T
        y_inter = _bdot(cm, st_prev) * ecum_s[rows, gl:gl + SSD_GW]
        for pr in range(SSD_HPG // 2):
            h0 = g * SSD_HPG + 2 * pr
            xl = h0 * SSD_HEADDIM
            xp = xbc_s[h0 // 2, rows, :]
            acc = y_inter[:, pr * LANES:(pr + 1) * LANES]
            for e in range(2):
                h = h0 + e
                seg = cum_c[:, h:h + 1] - cum_t[h:h + 1, :]
                dec = jnp.exp(jnp.where(tril, seg, -jnp.inf))
                wgt = cb * dec * dt_t[h:h + 1, :]
                xm = jnp.where(low_half if e == 0 else jnp.logical_not(low_half), xp, 0.0)
                acc = acc + _bdot(wgt, xm)
            yacc_s[rows, xl:xl + LANES] = acc
            front.step()
        st_new = st_prev * cdec_s[pl.ds(n, 1), gl:gl + SSD_GW] + ds
        if chained:
            st_s[g] = st_new
        else:
            sout_ref[n, g] = st_new.T

    order = [(n, g) for n in range(nc) for g in range(SSD_GROUPS)]
    tr_ahead = chunk_rows_t(0)
    ahead = state_free(*order[0])
    for i, (n, g) in enumerate(order):
        cur, tr = ahead, tr_ahead
        if i + 1 < len(order):
            if order[i + 1][0] != n:
                tr_ahead = chunk_rows_t(n + 1)
            ahead = state_free(*order[i + 1])
        group_step(n, g, *tr, *cur)
    front.flush()

    xs = jnp.concatenate([xbc_s[j] for j in range(SSD_DI // LANES)], axis=1)
    y = yacc_s[...] + de_ref[...] * xs
    yz = _rms(y * _silu(z_s[...]), snw_ref[...]).astype(BF16)
    y_ref[...] = x + jnp.dot(yz, wout_ref[...], preferred_element_type=F32)

    if chained:
        z_s[...] = pin_s[:, 0:SSD_DI]
        _conv_fill(xpad_s, pin_s.at[:, SSD_DI:], None, T=T, c=c, chained=True)
        dtp_s[...] = dtp_in
        xs_s[...] = x_in

        @pl.when(pl.program_id(0) == pl.num_programs(0) - 1)
        def _():
            for g in range(SSD_GROUPS):
                sout_ref[g] = st_s[g].T


def _const_spec(shape):
    nd = len(shape)
    return pl.BlockSpec(shape, lambda i, _nd=nd: (0,) * _nd, pipeline_mode=pl.Buffered(1))


def _row_specs(rows, T, lag):
    n = rows // T
    lag = int(lag)
    if lag == 0:
        spec = pl.BlockSpec((T, D_MODEL), lambda i: (i, 0))
        return (n,), spec, spec
    return ((n + lag,),
            pl.BlockSpec((T, D_MODEL), lambda i: (jnp.minimum(i, n - 1), 0)),
            pl.BlockSpec((T, D_MODEL), lambda i: (jnp.maximum(i - lag, 0), 0)))


def _params():
    return pltpu.CompilerParams(dimension_semantics=("arbitrary",), vmem_limit_bytes=VMEM_LIMIT_BYTES)


def _block_tri(T, c):
    r = np.arange(T)
    return jnp.asarray((r[:, None] // c == r[None, :] // c) & (r[:, None] >= r[None, :]), BF16)


def _eye():
    return jnp.asarray(np.eye(LANES), BF16)


def _gla_layer(x, s0, p, *, T, c, chained, final_norm_w=None):
    rows = x.shape[0]
    consts = [p["norm_w"], p["w_main"], p["w_glr"], p["w_gu"], p["b_gate"], p["gnorm_w"], p["w_out"],
              _block_tri(min(T, CUMSUM_ROWS), c)]
    args = [x] + consts
    grid, x_spec, y_spec = _row_specs(rows, T, 1 if chained else 0)
    specs = [x_spec] + [_const_spec(a.shape) for a in consts]
    if not chained:
        s_all, layer = s0
        args.append(s_all)
        specs.append(pl.BlockSpec((1,) + s_all.shape[1:], lambda i, _l=layer: (_l, 0, 0, 0, 0),
                                  pipeline_mode=pl.Buffered(1)))
    if final_norm_w is not None:
        args.append(final_norm_w)
        specs.append(_const_spec(final_norm_w.shape))
    st_shape = (GLA_HEADS, GLA_DK, GLA_DV) if chained else s0[0].shape[1:]
    scratch = [pltpu.VMEM((T, GLA_MAIN), F32), pltpu.VMEM((T, GLA_QK), F32), pltpu.VMEM((T, GLA_V), BF16)]
    if chained:
        scratch += [pltpu.VMEM((T, LANES), F32), pltpu.VMEM((T, D_MODEL), F32), pltpu.VMEM((T, GLA_MAIN), F32)]
    body = functools.partial(_gla_body, T=T, c=c, chained=chained, final_norm=final_norm_w is not None)
    return pl.pallas_call(
        body,
        grid=grid,
        in_specs=specs,
        out_specs=[y_spec, pl.BlockSpec(st_shape, lambda i, _n=len(st_shape): (0,) * _n)],
        out_shape=[jax.ShapeDtypeStruct((rows, D_MODEL), F32), jax.ShapeDtypeStruct(st_shape, F32)],
        scratch_shapes=scratch,
        compiler_params=_params(),
        name="gla_chained" if chained else "gla_batched",
    )(*args)


def _lru_layer(x, cbuf, h0, p, *, T, c, chained):
    rows = x.shape[0]
    W = LRU_WIDTH
    consts = [p["norm_w"], p["w_in"], p["conv_w"], p["conv_b"], p["w_band"], p["b_a"], p["b_x"], p["lam"], p["w_out"]]
    args = [x] + consts
    grid, x_spec, y_spec = _row_specs(rows, T, 2 if chained else 0)
    specs = [x_spec] + [_const_spec(a.shape) for a in consts]
    if chained:
        cb_shape, h_shape = (CONV_W - 1, W), (1, W)
        pad_rows = SUBLANES + T
    else:
        args += [cbuf, h0]
        specs += [_const_spec(cbuf.shape), _const_spec(h0.shape)]
        cb_shape, h_shape = cbuf.shape, h0.shape
        pad_rows = (T // c) * (SUBLANES + c)
    scan_rows = SUBLANES * _scan_pitch(T // SUBLANES)
    body = functools.partial(_lru_body, T=T, c=c, chained=chained)
    zero_map = lambda n: (lambda i: (0,) * n)
    return pl.pallas_call(
        body,
        grid=grid,
        in_specs=specs,
        out_specs=[y_spec, pl.BlockSpec(cb_shape, zero_map(len(cb_shape))),
                   pl.BlockSpec(h_shape, zero_map(len(h_shape)))],
        out_shape=[jax.ShapeDtypeStruct((rows, D_MODEL), F32), jax.ShapeDtypeStruct(cb_shape, F32),
                   jax.ShapeDtypeStruct(h_shape, F32)],
        scratch_shapes=[pltpu.VMEM((LRU_TILES, pad_rows, LANES), F32), pltpu.VMEM((LRU_TILES, T, LANES), F32)]
                       + [pltpu.VMEM((LRU_TILES, scan_rows, LANES), F32)] * 2 + [pltpu.VMEM((SUBLANES, W), F32)]
                       + [pltpu.VMEM((T, W), F32), pltpu.VMEM((T, D_MODEL), F32), pltpu.VMEM((T, D_MODEL), F32),
                          pltpu.VMEM((T, 2 * W), F32), pltpu.VMEM((T, W), BF16), pltpu.VMEM((1, W), F32)] * chained,
        compiler_params=_params(),
        name="lru_chained" if chained else "lru_batched",
    )(*args)


def _ssd_layer(x, cbuf, s0, p, *, T, c, chained):
    rows = x.shape[0]
    consts = [p["norm_w"], p["w_in"], p["w_dt"], p["conv_w"], p["conv_b"], p["dt_bias"], p["a_log"],
              p["d_exp"], p["snorm_w"], p["w_out"], _block_tri(T, c), _eye(), p["expand"]]
    args = [x] + consts
    grid, x_spec, y_spec = _row_specs(rows, T, 1 if chained else 0)
    specs = [x_spec] + [_const_spec(a.shape) for a in consts]
    if chained:
        cb_shape = (CONV_W - 1, SSD_CONV_DIM)
        st_shape = (SSD_GROUPS, SSD_GW, SSD_STATE)
        pad_rows = SUBLANES + T
    else:
        args += [cbuf, s0]
        specs += [_const_spec(cbuf.shape), _const_spec(s0.shape)]
        cb_shape, st_shape = cbuf.shape, s0.shape
        pad_rows = (T // c) * (SUBLANES + c)
    conv_tiles = SSD_CONV_DIM // LANES
    scratch = [pltpu.VMEM((conv_tiles, pad_rows, LANES), F32), pltpu.VMEM((conv_tiles, T, LANES), F32),
               pltpu.VMEM((T, SSD_DI), F32), pltpu.VMEM((T, LANES), F32), pltpu.VMEM((T, LANES), F32),
               pltpu.VMEM((T, SSD_DI), F32), pltpu.VMEM((T, SSD_DI), F32), pltpu.VMEM((T, SSD_DI), F32),
               pltpu.VMEM((SUBLANES, SSD_DI), F32)]
    if chained:
        scratch += [pltpu.VMEM((SSD_GROUPS, SSD_STATE, SSD_GW), F32), pltpu.VMEM((T, LANES), F32),
                    pltpu.VMEM((T, D_MODEL), F32), pltpu.VMEM((T, SSD_DI + SSD_CONV_DIM), F32)]
    body = functools.partial(_ssd_body, T=T, c=c, chained=chained)
    zero_map = lambda n: (lambda i: (0,) * n)
    return pl.pallas_call(
        body,
        grid=grid,
        in_specs=specs,
        out_specs=[y_spec, pl.BlockSpec(cb_shape, zero_map(len(cb_shape))),
                   pl.BlockSpec(st_shape, zero_map(len(st_shape)))],
        out_shape=[jax.ShapeDtypeStruct((rows, D_MODEL), F32), jax.ShapeDtypeStruct(cb_shape, F32),
                   jax.ShapeDtypeStruct(st_shape, F32)],
        scratch_shapes=scratch,
        compiler_params=_params(),
        name="ssd_chained" if chained else "ssd_batched",
    )(*args)


def _row(v):
    return v.reshape(1, -1).astype(F32)


def _pad_lanes(a, width=LANES):
    return jnp.pad(a, [(0, 0)] * (a.ndim - 1) + [(0, width - a.shape[-1])])


def _band_slabs(w):
    blk = np.arange(LRU_WIDTH) // LRU_BW
    on_diag = jnp.asarray(blk[:, None] == blk[None, :], w.dtype)
    dense = jnp.tile(w.reshape(LRU_WIDTH, LRU_BW), (1, LRU_BLOCKS)) * on_diag
    slabs = []
    for j in range(LRU_TILES):
        st = min(max(j - 1, 0), LRU_TILES - LRU_BAND)
        slabs.append(dense[st * LANES:(st + LRU_BAND) * LANES, j * LANES:(j + 1) * LANES])
    return jnp.stack(slabs)


def _gla_params(norm_w, w_in, w_gate_up, b_gate, gnorm_w, w_out):
    n_main = 2 * GLA_QK + 2 * GLA_V
    return {
        "norm_w": _row(norm_w),
        "w_main": w_in.astype(BF16),
        "w_glr": _pad_lanes(w_in[:, n_main:]).astype(BF16),
        "w_gu": jnp.pad(w_gate_up, ((0, LANES - GLA_RANK), (0, 0))).astype(BF16),
        "b_gate": _row(b_gate),
        "gnorm_w": _row(gnorm_w),
        "w_out": w_out.astype(BF16),
    }


def _lru_params(norm_w, w_in, conv_w, conv_b, w_a, b_a, w_x, b_x, lam, w_out):
    return {
        "norm_w": _row(norm_w),
        "w_in": w_in.astype(BF16),
        "conv_w": conv_w.astype(F32),
        "conv_b": _row(conv_b),
        "w_band": jnp.concatenate([_band_slabs(w_a), _band_slabs(w_x)], axis=-1).astype(BF16),
        "b_a": _row(b_a),
        "b_x": _row(b_x),
        "lam": _row(lam),
        "w_out": w_out.astype(BF16),
    }


def _ssd_params(norm_w, w_in, conv_w, conv_b, dt_bias, a_log, d_skip, snorm_w, w_out):
    heads = np.arange(SSD_DI) // SSD_HEADDIM
    lane = np.arange(LANES)
    expand = jnp.asarray((lane[:, None] % SSD_HEADS == heads[None, :]) & (lane[:, None] < 3 * SSD_HEADS), BF16)
    reps = LANES // SSD_HEADS
    return {
        "norm_w": _row(norm_w),
        "w_in": w_in.astype(BF16),
        "w_dt": jnp.tile(w_in[:, SSD_DI + SSD_CONV_DIM:], (1, reps)).astype(BF16),
        "conv_w": conv_w.astype(F32),
        "conv_b": _row(conv_b),
        "dt_bias": jnp.tile(_row(dt_bias), (1, reps)),
        "a_log": jnp.tile(_row(a_log), (1, reps)),
        "d_exp": _row(jnp.repeat(d_skip, SSD_HEADDIM)),
        "snorm_w": _row(snorm_w),
        "w_out": w_out.astype(BF16),
        "expand": expand,
    }


def _trunk(x, st_gla, st_lru_conv, st_lru_h, st_ssd_conv, st_ssd_h, layers, final_norm_w, *, tiles, c, chained):
    new = {"gla": [], "lru_conv": [], "lru_h": [], "ssd_conv": [], "ssd_h": []}
    n_layers = len(layers)
    for i, (kind, j, p) in enumerate(layers):
        T = tiles[kind]
        if kind == "gla":
            fnw = final_norm_w if i == n_layers - 1 else None
            x, s = _gla_layer(x, None if chained else (st_gla, j), p, T=T, c=c, chained=chained, final_norm_w=fnw)
            new["gla"].append(s)
        elif kind == "lru":
            x, cb, h = _lru_layer(x, None if chained else st_lru_conv[j], None if chained else st_lru_h[j], p,
                                  T=T, c=c, chained=chained)
            new["lru_conv"].append(cb)
            new["lru_h"].append(h)
        else:
            s0 = None if chained else st_ssd_h[j].reshape(-1, SSD_GROUPS, SSD_GW, SSD_STATE)
            x, cb, s = _ssd_layer(x, None if chained else st_ssd_conv[j], s0, p, T=T, c=c, chained=chained)
            new["ssd_conv"].append(cb)
            new["ssd_h"].append(s)
    return x, new


def kernel(x_prompt, x_sample, state_gla, state_lru_conv, state_lru_h, state_ssd_conv, state_ssd_h, norm_w, final_norm_w, gla_w_in, gla_w_gate_up, gla_b_gate, gla_norm_w, gla_w_out, lru_w_in, lru_conv_w, lru_conv_b, lru_w_a, lru_b_a, lru_w_x, lru_b_x, lru_lambda, lru_w_out, ssd_w_in, ssd_conv_w, ssd_conv_b, ssd_dt_bias, ssd_a_log, ssd_d, ssd_norm_w, ssd_w_out):
    depth = norm_w.shape[0]
    layers = []
    for i in range(depth):
        j = i // 3
        if i % 3 == 0:
            layers.append(("gla", j, _gla_params(norm_w[i], gla_w_in[j], gla_w_gate_up[j], gla_b_gate[j],
                                                 gla_norm_w[j], gla_w_out[j])))
        elif i % 3 == 1:
            layers.append(("lru", j, _lru_params(norm_w[i], lru_w_in[j], lru_conv_w[j], lru_conv_b[j], lru_w_a[j],
                                                 lru_b_a[j], lru_w_x[j], lru_b_x[j], lru_lambda[j], lru_w_out[j])))
        else:
            layers.append(("ssd", j, _ssd_params(norm_w[i], ssd_w_in[j], ssd_conv_w[j], ssd_conv_b[j],
                                                 ssd_dt_bias[j], ssd_a_log[j], ssd_d[j], ssd_norm_w[j],
                                                 ssd_w_out[j])))
    assert layers[-1][0] == "gla", "the final RMSNorm is fused into a GLA layer"
    fnw = _row(final_norm_w)

    bp, lp, _ = x_prompt.shape
    bs, ls, _ = x_sample.shape
    p_tiles = {k: min(t, lp) for k, t in PROMPT_TILE.items()}
    assert bp == 1 and all(lp % t == 0 for t in p_tiles.values()) and bs * ls == LANES and bs == SUBLANES

    yp, new_p = _trunk(x_prompt.reshape(lp, D_MODEL), None, None, None, None, None, layers, fnw,
                       tiles=p_tiles, c=min(CHUNK, lp), chained=True)
    ys, new_s = _trunk(x_sample.reshape(bs * ls, D_MODEL), state_gla, state_lru_conv, state_lru_h,
                       state_ssd_conv, state_ssd_h, layers, fnw, tiles=dict.fromkeys(PROMPT_TILE, bs * ls),
                       c=min(CHUNK, ls), chained=False)

    def stack(xs, shape):
        return jnp.stack(xs).reshape(shape)

    n_gla, n_lru, n_ssd = len(new_p["gla"]), len(new_p["lru_h"]), len(new_p["ssd_h"])
    return (
        yp.reshape(x_prompt.shape), ys.reshape(x_sample.shape),
        stack(new_p["gla"], (n_gla, 1, GLA_HEADS, GLA_DK, GLA_DV)),
        stack(new_p["lru_conv"], (n_lru, 1, CONV_W - 1, LRU_WIDTH)),
        stack(new_p["lru_h"], (n_lru, 1, LRU_WIDTH)),
        stack(new_p["ssd_conv"], (n_ssd, 1, CONV_W - 1, SSD_CONV_DIM)),
        stack(new_p["ssd_h"], (n_ssd, 1, SSD_HEADS, SSD_HEADDIM, SSD_STATE)),
        stack(new_s["gla"], (n_gla, bs, GLA_HEADS, GLA_DK, GLA_DV)),
        stack(new_s["lru_conv"], (n_lru, bs, CONV_W - 1, LRU_WIDTH)),
        stack(new_s["lru_h"], (n_lru, bs, LRU_WIDTH)),
        stack(new_s["ssd_conv"], (n_ssd, bs, CONV_W - 1, SSD_CONV_DIM)),
        stack(new_s["ssd_h"], (n_ssd, bs, SSD_HEADS, SSD_HEADDIM, SSD_STATE)),
    )
```

```python
import functools

import numpy as np
import jax
import jax.numpy as jnp
from jax import lax
from jax.experimental import pallas as pl
from jax.experimental.pallas import tpu as pltpu

F32 = jnp.float32
BF16 = jnp.bfloat16

D_MODEL = 1024
CHUNK = 64
EPS = 1e-6
CONV_W = 4

GLA_HEADS = 4
GLA_QK = D_MODEL // 2
GLA_V = D_MODEL
GLA_DK = GLA_QK // GLA_HEADS
GLA_DV = GLA_V // GLA_HEADS
GLA_RANK = 16
GLA_TAU = 16.0
GLA_MAIN = 2 * GLA_QK + 2 * GLA_V

LRU_WIDTH = 1408
LRU_BLOCKS = 16
LRU_BW = LRU_WIDTH // LRU_BLOCKS
LRU_C = 8.0

SSD_DI = 2 * D_MODEL
SSD_HEADDIM = 64
SSD_HEADS = SSD_DI // SSD_HEADDIM
SSD_STATE = 128
SSD_GROUPS = 4
SSD_HPG = SSD_HEADS // SSD_GROUPS
SSD_CONV_DIM = SSD_DI + 2 * SSD_GROUPS * SSD_STATE
SSD_GW = SSD_HPG * SSD_HEADDIM

LANES = 128
SUBLANES = 8
PROMPT_TILE = {"gla": 512, "lru": 256, "ssd": 256}
CONV_SLOT_WEIGHT = 1
ROW_BLOCK = 32
PROJ_SLICE = 256
LOOKAHEAD = 1
PROJ_ROWS = 512
CUMSUM_ROWS = 256
VMEM_LIMIT_BYTES = 56 * 1024 * 1024
LRU_TILES = LRU_WIDTH // LANES
LRU_BAND = 3


def _bdot(a, b):
    return jnp.dot(a.astype(BF16), b.astype(BF16), preferred_element_type=F32)


def _bdot_nt(a, b):
    return lax.dot_general(a.astype(BF16), b.astype(BF16), (((1,), (1,)), ((), ())),
                           preferred_element_type=F32)


def _bdot_tn(a, b):
    return lax.dot_general(a.astype(BF16), b.astype(BF16), (((0,), (0,)), ((), ())),
                           preferred_element_type=F32)


def _split3(x):
    hi = x.astype(BF16)
    r1 = x - hi.astype(F32)
    mid = r1.astype(BF16)
    lo = (r1 - mid.astype(F32)).astype(BF16)
    return hi, mid, lo


def _sel_dot(sel, x):
    hi, mid, lo = _split3(x)
    d = lambda p: jnp.dot(sel, p, preferred_element_type=F32)
    return d(hi) + d(mid) + d(lo)


def _expand_heads(x, expand):
    hi = x.astype(BF16).astype(F32)
    r1 = x - hi
    mid = r1.astype(BF16).astype(F32)
    lane = lax.broadcasted_iota(jnp.int32, x.shape, 1)
    packed = jnp.where(lane < SSD_HEADS, hi, jnp.where(lane < 2 * SSD_HEADS, mid, r1 - mid))
    return jnp.dot(packed.astype(BF16), expand, preferred_element_type=F32)


def _sel_dot_nt(sel, x):
    hi, mid, lo = _split3(x)
    d = lambda p: lax.dot_general(sel, p, (((1,), (1,)), ((), ())), preferred_element_type=F32)
    return d(hi) + d(mid) + d(lo)


def _rms(x, w):
    return x * lax.rsqrt(jnp.mean(x * x, axis=-1, keepdims=True) + EPS) * w


def _softplus(x):
    return jnp.maximum(x, 0.0) + jnp.log1p(jnp.exp(-jnp.abs(x)))


def _sigmoid(x):
    return 0.5 * jnp.tanh(0.5 * x) + 0.5


def _silu(x):
    return x * _sigmoid(x)


class _Interleave:
    def __init__(self, pieces, slots):
        self._pieces, self._slots, self._calls, self._total = list(pieces), slots, 0, None

    def take(self, k):
        for _ in range(min(k, len(self._pieces))):
            self._pieces.pop(0)()

    def step(self):
        if self._total is None:
            self._total = len(self._pieces)
        self._calls += 1
        left_after = self._total - self._calls * self._total // self._slots
        self.take(len(self._pieces) - max(left_after, 0))

    def flush(self):
        self.take(len(self._pieces))


def _pipelined(items, produce, consume, depth=None):
    depth = LOOKAHEAD if depth is None else depth
    ready = [produce(it) for it in items[:depth]]
    for i, it in enumerate(items):
        if i + depth < len(items):
            ready.append(produce(items[i + depth]))
        consume(it, ready.pop(0))


def _proj_pieces(dst_ref, hb, w_ref, n_cols):
    rows = hb.shape[0]
    rstep = min(rows, PROJ_ROWS)

    def piece(r0, lo):
        hi = min(lo + PROJ_SLICE, n_cols)
        dst_ref[r0:r0 + rstep, lo:hi] = jnp.dot(hb[r0:r0 + rstep, :], w_ref[:, lo:hi], preferred_element_type=F32)

    return [functools.partial(piece, r0, lo) for lo in range(0, n_cols, PROJ_SLICE) for r0 in range(0, rows, rstep)]


def _scan_pitch(seg):
    return seg if (seg // SUBLANES) % 2 == 1 else seg + SUBLANES


_HDR = SUBLANES
_PAD0 = _HDR - (CONV_W - 1)


def _conv_seqs(T, c, chained):
    return [(0, 0, T)] if chained else [(b * (_HDR + c), b * c, c) for b in range(T // c)]


def _conv_fill(xpad_s, xb, cbuf_in, *, T, c, chained):
    nt = xpad_s.shape[0]
    for b, (base, r0, n) in enumerate(_conv_seqs(T, c, chained)):
        prev = None if chained else cbuf_in[b]
        for j in range(nt):
            lanes = slice(j * LANES, (j + 1) * LANES)
            if prev is not None:
                xpad_s[j, base + _PAD0:base + _HDR, :] = prev[:, lanes]
            xpad_s[j, base + _HDR:base + _HDR + n, :] = xb[r0:r0 + n, lanes]


def _conv_apply(xpad_s, xc_s, cw_ref, cb_ref, cbuf_out, *, T, c, chained, act=None, after_tile=None):
    nt = xpad_s.shape[0]
    blk = 2 * SUBLANES
    seqs = _conv_seqs(T, c, chained)
    for j in range(nt):
        lanes = slice(j * LANES, (j + 1) * LANES)
        w = [jnp.broadcast_to(cw_ref[k:k + 1, lanes], (SUBLANES, LANES)) for k in range(CONV_W)]
        bias = jnp.broadcast_to(cb_ref[:, lanes], (SUBLANES, LANES))
        for b, (base, r0, n) in enumerate(seqs):
            for i in range(n // blk):
                p0 = base + _PAD0 + i * blk
                win = [xpad_s[j, pl.ds(p0 + m, SUBLANES, stride=2), :] for m in range(CONV_W + 1)]
                for par in range(2):
                    y = bias
                    for k in range(CONV_W):
                        y = y + w[k] * win[par + k]
                    xc_s[j, pl.ds(r0 + i * blk + par, SUBLANES, stride=2), :] = y if act is None else act(y)
            tail = xpad_s[j, base + _HDR + n - (CONV_W - 1):base + _HDR + n, :]
            if chained:
                cbuf_out[:, lanes] = tail
                xpad_s[j, _PAD0:_HDR, :] = tail
            else:
                cbuf_out[b, :, lanes] = tail
        if after_tile is not None:
            after_tile()


def _gla_body(*refs, T, c, chained, final_norm):
    refs = list(refs)
    (x_ref, nw_ref, wm_ref, wglr_ref, wgu_ref, bg_ref, gnw_ref, wout_ref, tri_ref) = refs[:9]
    refs = refs[9:]
    s0_ref = None if chained else refs.pop(0)
    fnw_ref = refs.pop(0) if final_norm else None
    if chained:
        y_ref, sout_ref, proj_s, bcum_s, o_s, glr_s, xs_s, pin_s = refs
    else:
        y_ref, sout_ref, proj_s, bcum_s, o_s = refs
    nc = T // c

    if chained:
        @pl.when(pl.program_id(0) <= 1)
        def _():
            sout_ref[...] = jnp.zeros(sout_ref.shape, F32)

        @pl.when(pl.program_id(0) == 0)
        def _():
            proj_s[...] = jnp.zeros(proj_s.shape, F32)
            glr_s[...] = jnp.zeros(glr_s.shape, F32)
            xs_s[...] = jnp.zeros(xs_s.shape, F32)

    x_in = x_ref[...]
    hb = _rms(x_in, nw_ref[...]).astype(BF16)
    glr_in = jnp.dot(hb, wglr_ref[...], preferred_element_type=F32)
    if chained:
        x = xs_s[...]
        glr = glr_s[...]
        front = _Interleave(_proj_pieces(pin_s, hb, wm_ref, GLA_MAIN), nc * GLA_HEADS)
    else:
        x, glr = x_in, glr_in
        proj_s[...] = jnp.dot(hb, wm_ref[:, 0:GLA_MAIN], preferred_element_type=F32)
        front = _Interleave([], nc * GLA_HEADS)

    zg = _bdot(glr, wgu_ref[...]) + bg_ref[...]
    front.take(2)
    log_a = -_softplus(-zg) / GLA_TAU
    tb = tri_ref.shape[0]
    for b in range(T // tb):
        bcum_s[b * tb:(b + 1) * tb, :] = _sel_dot(tri_ref[...], log_a[b * tb:(b + 1) * tb, :])

    tril = lax.broadcasted_iota(jnp.int32, (c, c), 0) >= lax.broadcasted_iota(jnp.int32, (c, c), 1)
    k_off, v_off, g_off = GLA_QK, 2 * GLA_QK, 2 * GLA_QK + GLA_V

    dec_cols = []
    for hd in range(GLA_HEADS):
        kc = hd * GLA_DK
        last_rows = [bcum_s[n * c + c - 1:n * c + c, kc:kc + GLA_DK] for n in range(nc)]
        blk = jnp.concatenate(last_rows + [jnp.zeros((GLA_DK - nc, GLA_DK), F32)], axis=0)
        dec_cols.append(jnp.exp(blk.T))

    def state_free(n, hd):
        rows = pl.ds(n * c, c)
        kc = hd * GLA_DK
        vc = hd * GLA_DV
        bc = bcum_s[rows, kc:kc + GLA_DK]
        bl = bc[c - 1:c, :]
        q = proj_s[rows, kc:kc + GLA_DK]
        k = proj_s[rows, k_off + kc:k_off + kc + GLA_DK]
        v = proj_s[rows, v_off + vc:v_off + vc + GLA_DV].astype(BF16)
        qd = (q * (GLA_DK ** -0.5) * jnp.exp(bc)).astype(BF16)
        kd = k * jnp.exp(-bc)
        ke = k * jnp.exp(bl - bc)
        att = jnp.where(tril, _bdot_nt(qd, kd), 0.0).astype(BF16)
        return qd, v, att, _bdot_tn(ke, v)

    def state_step(n, hd, qd, v, att, ds):
        rows = pl.ds(n * c, c)
        vc = hd * GLA_DV
        g = proj_s[rows, g_off + vc:g_off + vc + GLA_DV]
        s_prev = sout_ref[hd] if chained else s0_ref[0, n, hd]
        o = _bdot(att, v) + _bdot(qd, s_prev)
        s_new = dec_cols[hd][:, n:n + 1] * s_prev + ds
        if chained:
            sout_ref[hd] = s_new
        else:
            sout_ref[n, hd] = s_new
        o = o * lax.rsqrt(jnp.mean(o * o, axis=-1, keepdims=True) + EPS) * gnw_ref[...]
        o_s[rows, vc:vc + GLA_DV] = (o * _silu(g)).astype(BF16)

    def consume(item, res):
        front.step()
        state_step(*item, *res)

    _pipelined([(n, hd) for n in range(nc) for hd in range(GLA_HEADS)], lambda it: state_free(*it), consume)
    front.flush()
    out = x + jnp.dot(o_s[...], wout_ref[...], preferred_element_type=F32)
    if final_norm:
        out = _rms(out, fnw_ref[...])
    y_ref[...] = out
    if chained:
        proj_s[...] = pin_s[...]
        glr_s[...] = glr_in
        xs_s[...] = x_in


def _out_slice(y_ref, res_ref, lhs_ref, w_ref, lo):
    hi = min(lo + PROJ_SLICE, w_ref.shape[1])
    y_ref[:, lo:hi] = res_ref[:, lo:hi] + jnp.dot(lhs_ref[...], w_ref[:, lo:hi], preferred_element_type=F32)


def _lru_body(*refs, T, c, chained):
    refs = list(refs)
    (x_ref, nw_ref, win_ref, cw_ref, cb_ref, wband_ref, ba_ref, bx_ref, lam_ref, wout_ref) = refs[:10]
    refs = refs[10:]
    if chained:
        cbuf_in = h0_ref = None
        (y_ref, cbuf_out, h_out, xpad_s, xc_s, a_s, u_s, hin_s,
         gate_s, xs1_s, xs2_s, pin_s, gated_s, hc_s) = refs
    else:
        cbuf_in = refs.pop(0)
        h0_ref = refs.pop(0)
        y_ref, cbuf_out, h_out, xpad_s, xc_s, a_s, u_s, hin_s = refs
    W = LRU_WIDTH
    seg = T // SUBLANES
    tiles = range(LRU_TILES)
    lanes = lambda j: slice(j * LANES, (j + 1) * LANES)

    if chained:
        step = pl.program_id(0)

        @pl.when(step <= 1)
        def _():
            hc_s[...] = jnp.zeros(hc_s.shape, F32)

        @pl.when(step == 0)
        def _():
            xpad_s[...] = jnp.zeros(xpad_s.shape, F32)
            gate_s[...] = jnp.zeros(gate_s.shape, F32)
            xs1_s[...] = jnp.zeros(xs1_s.shape, F32)
            xs2_s[...] = jnp.zeros(xs2_s.shape, F32)
            gated_s[...] = jnp.zeros(gated_s.shape, BF16)

    x_in = x_ref[...]
    hb = _rms(x_in, nw_ref[...]).astype(BF16)
    if chained:
        pieces = [functools.partial(_out_slice, y_ref, xs2_s, gated_s, wout_ref, lo)
                  for lo in range(0, D_MODEL, PROJ_SLICE)]
        pieces += _proj_pieces(pin_s, hb, win_ref, 2 * W)
        n_conv, n_scan = 2, 5
        back = _Interleave(pieces[:n_conv], LRU_TILES)
        mid = _Interleave(pieces[n_conv:len(pieces) - n_scan], LRU_TILES)
        front = _Interleave(pieces[len(pieces) - n_scan:], 2 * seg)
    else:
        proj = jnp.dot(hb, win_ref[...], preferred_element_type=F32)
        _conv_fill(xpad_s, proj, cbuf_in, T=T, c=c, chained=False)
        back = mid = front = _Interleave([], 1)
    _conv_apply(xpad_s, xc_s, cw_ref, cb_ref, cbuf_out, T=T, c=c, chained=chained, after_tile=back.step)
    back.flush()

    pitch = _scan_pitch(seg)
    neg_c_sp = -LRU_C * _softplus(-lam_ref[...])
    def gate_pre(j):
        st = min(max(j - 1, 0), LRU_TILES - LRU_BAND)
        xin = jnp.concatenate([xc_s[st + m] for m in range(LRU_BAND)], axis=1)
        return _bdot(xin, wband_ref[j])

    def gate_apply(j, pre):
        mid.step()
        rb = min(seg, ROW_BLOCK)
        for s in range(SUBLANES):
            for o in range(0, seg, rb):
                rows = slice(s * seg + o, s * seg + o + rb)
                dst = slice(s * pitch + o, s * pitch + o + rb)
                r = _sigmoid(pre[rows, :LANES] + ba_ref[:, lanes(j)])
                i = _sigmoid(pre[rows, LANES:] + bx_ref[:, lanes(j)])
                log_a = r * neg_c_sp[:, lanes(j)]
                th = jnp.tanh(log_a)
                a_s[j, dst, :] = jnp.exp(log_a)
                u_s[j, dst, :] = ((jnp.sqrt(jnp.maximum(-2.0 * th, 0.0)) * lax.rsqrt(1.0 - th))
                                  * (i * xc_s[j, rows, :]))

    _pipelined(list(tiles), gate_pre, gate_apply)
    mid.flush()

    def seg_rows(t):
        return pl.ds(t, SUBLANES, stride=pitch)

    if chained:
        hs = [jnp.zeros((SUBLANES, LANES), F32) for _ in tiles]
        ps = [jnp.ones((SUBLANES, LANES), F32) for _ in tiles]
        for t in range(seg):
            for j in tiles:
                a = a_s[j, seg_rows(t), :]
                hs[j] = a * hs[j] + u_s[j, seg_rows(t), :]
                ps[j] = ps[j] * a
            front.step()
        h_end = jnp.concatenate(hs, axis=1)
        p_end = jnp.concatenate(ps, axis=1)
        hc = hc_s[...]
        for s in range(SUBLANES):
            hin_s[s:s + 1, :] = hc
            hc = h_end[s:s + 1, :] + p_end[s:s + 1, :] * hc
        hc_s[...] = hc
        h_init = hin_s[...]
    else:
        h_init = h0_ref[...]

    hs = [h_init[:, lanes(j)] for j in tiles]
    for t in range(seg):
        for j in tiles:
            hs[j] = a_s[j, seg_rows(t), :] * hs[j] + u_s[j, seg_rows(t), :]
            u_s[j, seg_rows(t), :] = hs[j]
        front.step()
    front.flush()

    def gated_tile(j, gate_j):
        h = jnp.concatenate([u_s[j, s * pitch:s * pitch + seg, :] for s in range(SUBLANES)], axis=0)
        return (h * _silu(gate_j)).astype(BF16)

    if chained:
        for j in tiles:
            for s in range(SUBLANES):
                rows = slice(s * seg, (s + 1) * seg)
                gated_s[rows, lanes(j)] = (u_s[j, s * pitch:s * pitch + seg, :]
                                           * _silu(gate_s[rows, lanes(j)])).astype(BF16)
        xs2_s[...] = xs1_s[...]
        xs1_s[...] = x_in
        _conv_fill(xpad_s, pin_s, None, T=T, c=c, chained=True)
        gate_s[...] = pin_s[:, W:]

        @pl.when(step == pl.num_programs(0) - 2)
        def _():
            h_out[...] = hc_s[...]
    else:
        h_out[...] = jnp.concatenate(hs, axis=1)
        gated = jnp.concatenate([gated_tile(j, proj[:, W + j * LANES:W + (j + 1) * LANES]) for j in tiles], axis=1)
        y_ref[...] = x_in + jnp.dot(gated, wout_ref[...], preferred_element_type=F32)


def _ssd_body(*refs, T, c, chained):
    refs = list(refs)
    (x_ref, nw_ref, win_ref, wdt_ref, cw_ref, cb_ref, dtb_ref, alog_ref, de_ref, snw_ref,
     wout_ref, tri_ref, eye_ref, exp_ref) = refs[:14]
    refs = refs[14:]
    n_zx = SSD_DI + SSD_CONV_DIM
    if chained:
        cbuf_in = s0_ref = None
    else:
        cbuf_in = refs.pop(0)
        s0_ref = refs.pop(0)
    if chained:
        (y_ref, cbuf_out, sout_ref, xpad_s, xbc_s, z_s, cum_s, dt_s, ecum_s, xw_s, yacc_s, cdec_s, st_s,
         dtp_s, xs_s, pin_s) = refs
    else:
        (y_ref, cbuf_out, sout_ref, xpad_s, xbc_s, z_s, cum_s, dt_s, ecum_s, xw_s, yacc_s, cdec_s) = refs
        st_s = None
    nc = T // c
    b_off = SSD_DI
    c_off = SSD_DI + SSD_GROUPS * SSD_STATE
    n_slots = (CONV_SLOT_WEIGHT * (SSD_CONV_DIM // LANES) + SSD_DI // LANES
               + nc * SSD_GROUPS * (SSD_HPG // 2))

    if chained:
        @pl.when(pl.program_id(0) <= 1)
        def _():
            st_s[...] = jnp.zeros(st_s.shape, F32)

        @pl.when(pl.program_id(0) == 0)
        def _():
            xpad_s[...] = jnp.zeros(xpad_s.shape, F32)
            z_s[...] = jnp.zeros(z_s.shape, F32)
            dtp_s[...] = jnp.zeros(dtp_s.shape, F32)
            xs_s[...] = jnp.zeros(xs_s.shape, F32)

    x_in = x_ref[...]
    hb = _rms(x_in, nw_ref[...]).astype(BF16)
    dtp_in = jnp.dot(hb, wdt_ref[...], preferred_element_type=F32)
    if chained:
        x = xs_s[...]
        dt_pre = dtp_s[...]
        front = _Interleave(_proj_pieces(pin_s, hb, win_ref, n_zx), n_slots)
    else:
        x, dt_pre = x_in, dtp_in
        z_s[...] = jnp.dot(hb, win_ref[:, 0:SSD_DI], preferred_element_type=F32)
        _conv_fill(xpad_s, jnp.dot(hb, win_ref[:, SSD_DI:n_zx], preferred_element_type=F32), cbuf_in,
                   T=T, c=c, chained=False)
        front = _Interleave([], n_slots)

    dt = _softplus(dt_pre + dtb_ref[...])
    a_neg = -jnp.exp(alog_ref[...])
    cum = _sel_dot(tri_ref[...], dt * a_neg)
    cum_s[...] = cum
    dt_s[...] = dt
    lasts = [cum[n * c + c - 1:n * c + c, :] for n in range(nc)]
    cl = jnp.concatenate([jnp.broadcast_to(l, (c, LANES)) for l in lasts], axis=0)
    ecum_s[...] = _expand_heads(jnp.exp(cum), exp_ref[...])
    wb_e = _expand_heads(jnp.exp(cl - cum) * dt, exp_ref[...])
    _conv_apply(xpad_s, xbc_s, cw_ref, cb_ref, cbuf_out, T=T, c=c, chained=chained, act=_silu,
                after_tile=lambda: [front.step() for _ in range(CONV_SLOT_WEIGHT)])
    for j in range(SSD_DI // LANES):
        xw_s[:, j * LANES:(j + 1) * LANES] = xbc_s[j] * wb_e[:, j * LANES:(j + 1) * LANES]
        front.step()
    cl_rows = jnp.concatenate(lasts + [jnp.zeros((SUBLANES - nc, LANES), F32)] * (nc < SUBLANES), axis=0)
    cdec_s[...] = _expand_heads(jnp.exp(cl_rows), exp_ref[...])

    tril = lax.broadcasted_iota(jnp.int32, (c, c), 0) >= lax.broadcasted_iota(jnp.int32, (c, c), 1)
    low_half = lax.broadcasted_iota(jnp.int32, (c, LANES), 1) < SSD_HEADDIM

    def chunk_rows_t(n):
        rows = pl.ds(n * c, c)
        return _sel_dot_nt(eye_ref[...], cum_s[rows, :]), _sel_dot_nt(eye_ref[...], dt_s[rows, :])

    def state_free(n, g):
        rows = pl.ds(n * c, c)
        bm = xbc_s[b_off // LANES + g, rows, :].astype(BF16)
        cm = xbc_s[c_off // LANES + g, rows, :].astype(BF16)
        cb = _bdot_nt(cm, bm)
        ds = _bdot_tn(bm, xw_s[rows, g * SSD_GW:(g + 1) * SSD_GW])
        return cm, cb, ds

    def group_step(n, g, cum_t, dt_t, cm, cb, ds):
        rows = pl.ds(n * c, c)
        gl = g * SSD_GW
        cum_c = cum_s[rows, :]
        if chained:
            st_prev = st_s[g]
        else:
            st_prev = s0_ref[n, g].T
        y_inter = _bdot(cm, st_prev) * ecum_s[rows, gl:gl + SSD_GW]
        for pr in range(SSD_HPG // 2):
            h0 = g * SSD_HPG + 2 * pr
            xl = h0 * SSD_HEADDIM
            xp = xbc_s[h0 // 2, rows, :]
            acc = y_inter[:, pr * LANES:(pr + 1) * LANES]
            for e in range(2):
                h = h0 + e
                seg = cum_c[:, h:h + 1] - cum_t[h:h + 1, :]
                dec = jnp.exp(jnp.where(tril, seg, -jnp.inf))
                wgt = cb * dec * dt_t[h:h + 1, :]
                xm = jnp.where(low_half if e == 0 else jnp.logical_not(low_half), xp, 0.0)
                acc = acc + _bdot(wgt, xm)
            yacc_s[rows, xl:xl + LANES] = acc
            front.step()
        st_new = st_prev * cdec_s[pl.ds(n, 1), gl:gl + SSD_GW] + ds
        if chained:
            st_s[g] = st_new
        else:
            sout_ref[n, g] = st_new.T

    transposed = {}

    def produce(item):
        n, g = item
        if n not in transposed:
            transposed[n] = chunk_rows_t(n)
        return transposed[n] + state_free(n, g)

    _pipelined([(n, g) for n in range(nc) for g in range(SSD_GROUPS)], produce,
               lambda item, res: group_step(*item, *res))
    front.flush()

    xs = jnp.concatenate([xbc_s[j] for j in range(SSD_DI // LANES)], axis=1)
    y = yacc_s[...] + de_ref[...] * xs
    yz = _rms(y * _silu(z_s[...]), snw_ref[...]).astype(BF16)
    y_ref[...] = x + jnp.dot(yz, wout_ref[...], preferred_element_type=F32)

    if chained:
        z_s[...] = pin_s[:, 0:SSD_DI]
        _conv_fill(xpad_s, pin_s.at[:, SSD_DI:], None, T=T, c=c, chained=True)
        dtp_s[...] = dtp_in
        xs_s[...] = x_in

        @pl.when(pl.program_id(0) == pl.num_programs(0) - 1)
        def _():
            for g in range(SSD_GROUPS):
                sout_ref[g] = st_s[g].T


def _const_spec(shape):
    nd = len(shape)
    return pl.BlockSpec(shape, lambda i, _nd=nd: (0,) * _nd, pipeline_mode=pl.Buffered(1))


def _row_specs(rows, T, lag):
    n = rows // T
    lag = int(lag)
    if lag == 0:
        spec = pl.BlockSpec((T, D_MODEL), lambda i: (i, 0))
        return (n,), spec, spec
    return ((n + lag,),
            pl.BlockSpec((T, D_MODEL), lambda i: (jnp.minimum(i, n - 1), 0)),
            pl.BlockSpec((T, D_MODEL), lambda i: (jnp.maximum(i - lag, 0), 0)))


def _params():
    return pltpu.CompilerParams(dimension_semantics=("arbitrary",), vmem_limit_bytes=VMEM_LIMIT_BYTES)


def _block_tri(T, c):
    r = np.arange(T)
    return jnp.asarray((r[:, None] // c == r[None, :] // c) & (r[:, None] >= r[None, :]), BF16)


def _eye():
    return jnp.asarray(np.eye(LANES), BF16)


def _gla_layer(x, s0, p, *, T, c, chained, final_norm_w=None):
    rows = x.shape[0]
    consts = [p["norm_w"], p["w_main"], p["w_glr"], p["w_gu"], p["b_gate"], p["gnorm_w"], p["w_out"],
              _block_tri(min(T, CUMSUM_ROWS), c)]
    args = [x] + consts
    grid, x_spec, y_spec = _row_specs(rows, T, 1 if chained else 0)
    specs = [x_spec] + [_const_spec(a.shape) for a in consts]
    if not chained:
        s_all, layer = s0
        args.append(s_all)
        specs.append(pl.BlockSpec((1,) + s_all.shape[1:], lambda i, _l=layer: (_l, 0, 0, 0, 0),
                                  pipeline_mode=pl.Buffered(1)))
    if final_norm_w is not None:
        args.append(final_norm_w)
        specs.append(_const_spec(final_norm_w.shape))
    st_shape = (GLA_HEADS, GLA_DK, GLA_DV) if chained else s0[0].shape[1:]
    scratch = [pltpu.VMEM((T, GLA_MAIN), F32), pltpu.VMEM((T, GLA_QK), F32), pltpu.VMEM((T, GLA_V), BF16)]
    if chained:
        scratch += [pltpu.VMEM((T, LANES), F32), pltpu.VMEM((T, D_MODEL), F32), pltpu.VMEM((T, GLA_MAIN), F32)]
    body = functools.partial(_gla_body, T=T, c=c, chained=chained, final_norm=final_norm_w is not None)
    return pl.pallas_call(
        body,
        grid=grid,
        in_specs=specs,
        out_specs=[y_spec, pl.BlockSpec(st_shape, lambda i, _n=len(st_shape): (0,) * _n)],
        out_shape=[jax.ShapeDtypeStruct((rows, D_MODEL), F32), jax.ShapeDtypeStruct(st_shape, F32)],
        scratch_shapes=scratch,
        compiler_params=_params(),
        name="gla_chained" if chained else "gla_batched",
    )(*args)


def _lru_layer(x, cbuf, h0, p, *, T, c, chained):
    rows = x.shape[0]
    W = LRU_WIDTH
    consts = [p["norm_w"], p["w_in"], p["conv_w"], p["conv_b"], p["w_band"], p["b_a"], p["b_x"], p["lam"], p["w_out"]]
    args = [x] + consts
    grid, x_spec, y_spec = _row_specs(rows, T, 2 if chained else 0)
    specs = [x_spec] + [_const_spec(a.shape) for a in consts]
    if chained:
        cb_shape, h_shape = (CONV_W - 1, W), (1, W)
        pad_rows = SUBLANES + T
    else:
        args += [cbuf, h0]
        specs += [_const_spec(cbuf.shape), _const_spec(h0.shape)]
        cb_shape, h_shape = cbuf.shape, h0.shape
        pad_rows = (T // c) * (SUBLANES + c)
    scan_rows = SUBLANES * _scan_pitch(T // SUBLANES)
    body = functools.partial(_lru_body, T=T, c=c, chained=chained)
    zero_map = lambda n: (lambda i: (0,) * n)
    return pl.pallas_call(
        body,
        grid=grid,
        in_specs=specs,
        out_specs=[y_spec, pl.BlockSpec(cb_shape, zero_map(len(cb_shape))),
                   pl.BlockSpec(h_shape, zero_map(len(h_shape)))],
        out_shape=[jax.ShapeDtypeStruct((rows, D_MODEL), F32), jax.ShapeDtypeStruct(cb_shape, F32),
                   jax.ShapeDtypeStruct(h_shape, F32)],
        scratch_shapes=[pltpu.VMEM((LRU_TILES, pad_rows, LANES), F32), pltpu.VMEM((LRU_TILES, T, LANES), F32)]
                       + [pltpu.VMEM((LRU_TILES, scan_rows, LANES), F32)] * 2 + [pltpu.VMEM((SUBLANES, W), F32)]
                       + [pltpu.VMEM((T, W), F32), pltpu.VMEM((T, D_MODEL), F32), pltpu.VMEM((T, D_MODEL), F32),
                          pltpu.VMEM((T, 2 * W), F32), pltpu.VMEM((T, W), BF16), pltpu.VMEM((1, W), F32)] * chained,
        compiler_params=_params(),
        name="lru_chained" if chained else "lru_batched",
    )(*args)


def _ssd_layer(x, cbuf, s0, p, *, T, c, chained):
    rows = x.shape[0]
    consts = [p["norm_w"], p["w_in"], p["w_dt"], p["conv_w"], p["conv_b"], p["dt_bias"], p["a_log"],
              p["d_exp"], p["snorm_w"], p["w_out"], _block_tri(T, c), _eye(), p["expand"]]
    args = [x] + consts
    grid, x_spec, y_spec = _row_specs(rows, T, 1 if chained else 0)
    specs = [x_spec] + [_const_spec(a.shape) for a in consts]
    if chained:
        cb_shape = (CONV_W - 1, SSD_CONV_DIM)
        st_shape = (SSD_GROUPS, SSD_GW, SSD_STATE)
        pad_rows = SUBLANES + T
    else:
        args += [cbuf, s0]
        specs += [_const_spec(cbuf.shape), _const_spec(s0.shape)]
        cb_shape, st_shape = cbuf.shape, s0.shape
        pad_rows = (T // c) * (SUBLANES + c)
    conv_tiles = SSD_CONV_DIM // LANES
    scratch = [pltpu.VMEM((conv_tiles, pad_rows, LANES), F32), pltpu.VMEM((conv_tiles, T, LANES), F32),
               pltpu.VMEM((T, SSD_DI), F32), pltpu.VMEM((T, LANES), F32), pltpu.VMEM((T, LANES), F32),
               pltpu.VMEM((T, SSD_DI), F32), pltpu.VMEM((T, SSD_DI), F32), pltpu.VMEM((T, SSD_DI), F32),
               pltpu.VMEM((SUBLANES, SSD_DI), F32)]
    if chained:
        scratch += [pltpu.VMEM((SSD_GROUPS, SSD_STATE, SSD_GW), F32), pltpu.VMEM((T, LANES), F32),
                    pltpu.VMEM((T, D_MODEL), F32), pltpu.VMEM((T, SSD_DI + SSD_CONV_DIM), F32)]
    body = functools.partial(_ssd_body, T=T, c=c, chained=chained)
    zero_map = lambda n: (lambda i: (0,) * n)
    return pl.pallas_call(
        body,
        grid=grid,
        in_specs=specs,
        out_specs=[y_spec, pl.BlockSpec(cb_shape, zero_map(len(cb_shape))),
                   pl.BlockSpec(st_shape, zero_map(len(st_shape)))],
        out_shape=[jax.ShapeDtypeStruct((rows, D_MODEL), F32), jax.ShapeDtypeStruct(cb_shape, F32),
                   jax.ShapeDtypeStruct(st_shape, F32)],
        scratch_shapes=scratch,
        compiler_params=_params(),
        name="ssd_chained" if chained else "ssd_batched",
    )(*args)


def _row(v):
    return v.reshape(1, -1).astype(F32)


def _pad_lanes(a, width=LANES):
    return jnp.pad(a, [(0, 0)] * (a.ndim - 1) + [(0, width - a.shape[-1])])


def _band_slabs(w):
    blk = np.arange(LRU_WIDTH) // LRU_BW
    on_diag = jnp.asarray(blk[:, None] == blk[None, :], w.dtype)
    dense = jnp.tile(w.reshape(LRU_WIDTH, LRU_BW), (1, LRU_BLOCKS)) * on_diag
    slabs = []
    for j in range(LRU_TILES):
        st = min(max(j - 1, 0), LRU_TILES - LRU_BAND)
        slabs.append(dense[st * LANES:(st + LRU_BAND) * LANES, j * LANES:(j + 1) * LANES])
    return jnp.stack(slabs)


def _gla_params(norm_w, w_in, w_gate_up, b_gate, gnorm_w, w_out):
    n_main = 2 * GLA_QK + 2 * GLA_V
    return {
        "norm_w": _row(norm_w),
        "w_main": w_in.astype(BF16),
        "w_glr": _pad_lanes(w_in[:, n_main:]).astype(BF16),
        "w_gu": jnp.pad(w_gate_up, ((0, LANES - GLA_RANK), (0, 0))).astype(BF16),
        "b_gate": _row(b_gate),
        "gnorm_w": _row(gnorm_w),
        "w_out": w_out.astype(BF16),
    }


def _lru_params(norm_w, w_in, conv_w, conv_b, w_a, b_a, w_x, b_x, lam, w_out):
    return {
        "norm_w": _row(norm_w),
        "w_in": w_in.astype(BF16),
        "conv_w": conv_w.astype(F32),
        "conv_b": _row(conv_b),
        "w_band": jnp.concatenate([_band_slabs(w_a), _band_slabs(w_x)], axis=-1).astype(BF16),
        "b_a": _row(b_a),
        "b_x": _row(b_x),
        "lam": _row(lam),
        "w_out": w_out.astype(BF16),
    }


def _ssd_params(norm_w, w_in, conv_w, conv_b, dt_bias, a_log, d_skip, snorm_w, w_out):
    heads = np.arange(SSD_DI) // SSD_HEADDIM
    lane = np.arange(LANES)
    expand = jnp.asarray((lane[:, None] % SSD_HEADS == heads[None, :]) & (lane[:, None] < 3 * SSD_HEADS), BF16)
    reps = LANES // SSD_HEADS
    return {
        "norm_w": _row(norm_w),
        "w_in": w_in.astype(BF16),
        "w_dt": jnp.tile(w_in[:, SSD_DI + SSD_CONV_DIM:], (1, reps)).astype(BF16),
        "conv_w": conv_w.astype(F32),
        "conv_b": _row(conv_b),
        "dt_bias": jnp.tile(_row(dt_bias), (1, reps)),
        "a_log": jnp.tile(_row(a_log), (1, reps)),
        "d_exp": _row(jnp.repeat(d_skip, SSD_HEADDIM)),
        "snorm_w": _row(snorm_w),
        "w_out": w_out.astype(BF16),
        "expand": expand,
    }


def _trunk(x, st_gla, st_lru_conv, st_lru_h, st_ssd_conv, st_ssd_h, layers, final_norm_w, *, tiles, c, chained):
    new = {"gla": [], "lru_conv": [], "lru_h": [], "ssd_conv": [], "ssd_h": []}
    n_layers = len(layers)
    for i, (kind, j, p) in enumerate(layers):
        T = tiles[kind]
        if kind == "gla":
            fnw = final_norm_w if i == n_layers - 1 else None
            x, s = _gla_layer(x, None if chained else (st_gla, j), p, T=T, c=c, chained=chained, final_norm_w=fnw)
            new["gla"].append(s)
        elif kind == "lru":
            x, cb, h = _lru_layer(x, None if chained else st_lru_conv[j], None if chained else st_lru_h[j], p,
                                  T=T, c=c, chained=chained)
            new["lru_conv"].append(cb)
            new["lru_h"].append(h)
        else:
            s0 = None if chained else st_ssd_h[j].reshape(-1, SSD_GROUPS, SSD_GW, SSD_STATE)
            x, cb, s = _ssd_layer(x, None if chained else st_ssd_conv[j], s0, p, T=T, c=c, chained=chained)
            new["ssd_conv"].append(cb)
            new["ssd_h"].append(s)
    return x, new


def kernel(x_prompt, x_sample, state_gla, state_lru_conv, state_lru_h, state_ssd_conv, state_ssd_h, norm_w, final_norm_w, gla_w_in, gla_w_gate_up, gla_b_gate, gla_norm_w, gla_w_out, lru_w_in, lru_conv_w, lru_conv_b, lru_w_a, lru_b_a, lru_w_x, lru_b_x, lru_lambda, lru_w_out, ssd_w_in, ssd_conv_w, ssd_conv_b, ssd_dt_bias, ssd_a_log, ssd_d, ssd_norm_w, ssd_w_out):
    depth = norm_w.shape[0]
    layers = []
    for i in range(depth):
        j = i // 3
        if i % 3 == 0:
            layers.append(("gla", j, _gla_params(norm_w[i], gla_w_in[j], gla_w_gate_up[j], gla_b_gate[j],
                                                 gla_norm_w[j], gla_w_out[j])))
        elif i % 3 == 1:
            layers.append(("lru", j, _lru_params(norm_w[i], lru_w_in[j], lru_conv_w[j], lru_conv_b[j], lru_w_a[j],
                                                 lru_b_a[j], lru_w_x[j], lru_b_x[j], lru_lambda[j], lru_w_out[j])))
        else:
            layers.append(("ssd", j, _ssd_params(norm_w[i], ssd_w_in[j], ssd_conv_w[j], ssd_conv_b[j],
                                                 ssd_dt_bias[j], ssd_a_log[j], ssd_d[j], ssd_norm_w[j],
                                                 ssd_w_out[j])))
    assert layers[-1][0] == "gla", "the final RMSNorm is fused into a GLA layer"
    fnw = _row(final_norm_w)

    bp, lp, _ = x_prompt.shape
    bs, ls, _ = x_sample.shape
    p_tiles = {k: min(t, lp) for k, t in PROMPT_TILE.items()}
    assert bp == 1 and all(lp % t == 0 for t in p_tiles.values()) and bs * ls == LANES and bs == SUBLANES

    yp, new_p = _trunk(x_prompt.reshape(lp, D_MODEL), None, None, None, None, None, layers, fnw,
                       tiles=p_tiles, c=min(CHUNK, lp), chained=True)
    ys, new_s = _trunk(x_sample.reshape(bs * ls, D_MODEL), state_gla, state_lru_conv, state_lru_h,
                       state_ssd_conv, state_ssd_h, layers, fnw, tiles=dict.fromkeys(PROMPT_TILE, bs * ls),
                       c=min(CHUNK, ls), chained=False)

    def stack(xs, shape):
        return jnp.stack(xs).reshape(shape)

    n_gla, n_lru, n_ssd = len(new_p["gla"]), len(new_p["lru_h"]), len(new_p["ssd_h"])
    return (
        yp.reshape(x_prompt.shape), ys.reshape(x_sample.shape),
        stack(new_p["gla"], (n_gla, 1, GLA_HEADS, GLA_DK, GLA_DV)),
        stack(new_p["lru_conv"], (n_lru, 1, CONV_W - 1, LRU_WIDTH)),
        stack(new_p["lru_h"], (n_lru, 1, LRU_WIDTH)),
        stack(new_p["ssd_conv"], (n_ssd, 1, CONV_W - 1, SSD_CONV_DIM)),
        stack(new_p["ssd_h"], (n_ssd, 1, SSD_HEADS, SSD_HEADDIM, SSD_STATE)),
        stack(new_s["gla"], (n_gla, bs, GLA_HEADS, GLA_DK, GLA_DV)),
        stack(new_s["lru_conv"], (n_lru, bs, CONV_W - 1, LRU_WIDTH)),
        stack(new_s["lru_h"], (n_lru, bs, LRU_WIDTH)),
        stack(new_s["ssd_conv"], (n_ssd, bs, CONV_W - 1, SSD_CONV_DIM)),
        stack(new_s["ssd_h"], (n_ssd, bs, SSD_HEADS, SSD_HEADDIM, SSD_STATE)),
    )
```
